```python
import math
import jax, jax.numpy as jnp
from jax import lax
import numpy as np

D_MODEL = 1024
BATCH = 16
SEQ = 2048
DEPTH = 4
DEC_BATCH = 128
DEC_SEQ = 4
PAST_LEN = 8192
PAGE_SIZE = 128

N_EVEN = (DEPTH + 1) // 2
N_ODD = DEPTH // 2
D_A = D_MODEL // 2
CONV_A = 3
D_B = D_MODEL // 2
B_HEADS = 8
B_HEAD_DIM = D_B // B_HEADS
CONV_B = 4
LRU_C = 8.0
D_C = D_MODEL // 2
CONV_C = 31
MLA_HEADS = 8
QK_NOPE = 64
QK_ROPE = 32
V_DIM = D_MODEL // (2 * MLA_HEADS)
Q_RANK = 3 * D_MODEL // 8
KV_RANK = D_MODEL // 4
ROPE_THETA = 10000.0
ATTN_SCALE = (QK_NOPE + QK_ROPE) ** -0.5
Q_BLOCK = 128
N_EXPERTS = 16
N_GROUPS = 4
EXPERTS_PER_GROUP = N_EXPERTS // N_GROUPS
TOP_K = 2
D_EXPERT = D_MODEL // 2
MOE_BLOCK = 128
ALPHA = (2 * DEPTH) ** 0.25
BETA = (8 * DEPTH) ** -0.25
EVEN_IN = 3 * D_A + 2 * D_B
ODD_IN = 2 * D_C + Q_RANK + KV_RANK + QK_ROPE
MIX_OUT_EVEN = D_A + D_B
MIX_OUT_ODD = D_C + MLA_HEADS * V_DIM

kernel_name = 'hybrid_conv_rglru_conformer_mla_moe_step'


def layer_norm(x, g, b, eps=1e-5):
    xf = x.astype(jnp.float32)
    mu = xf.mean(-1, keepdims=True)
    var = jnp.square(xf - mu).mean(-1, keepdims=True)
    return ((xf - mu) * lax.rsqrt(var + eps) * g + b).astype(x.dtype)


def rms_norm(x, g, eps=1e-6):
    xf = x.astype(jnp.float32)
    return (xf * lax.rsqrt(jnp.square(xf).mean(-1, keepdims=True) + eps) * g).astype(x.dtype)


def causal_dwconv(x, buf, w, b=None):
    xp = jnp.concatenate([buf.astype(x.dtype), x], axis=1)
    width, ch = w.shape
    y = lax.conv_general_dilated(xp, w[:, None, :].astype(x.dtype), window_strides=(1,), padding='VALID',
                                 dimension_numbers=('NWC', 'WIO', 'NWC'), feature_group_count=ch)
    if b is not None:
        y = y + b
    return y, xp[:, xp.shape[1] - (width - 1):]


def apply_rope(x, pos):
    half = x.shape[-1] // 2
    inv = ROPE_THETA ** (-jnp.arange(half, dtype=jnp.float32) / half)
    ang = pos.astype(jnp.float32)[:, None] * inv
    ang = ang.reshape(ang.shape[0], *([1] * (x.ndim - 3)), half)
    cos, sin = jnp.cos(ang).astype(x.dtype), jnp.sin(ang).astype(x.dtype)
    x1, x2 = x[..., :half], x[..., half:]
    return jnp.concatenate([x1 * cos - x2 * sin, x1 * sin + x2 * cos], axis=-1)


def rg_lru(u, h0, wa, ba, wx, bx, lam):
    bsz, t, dr = u.shape
    uh = u.reshape(bsz, t, B_HEADS, B_HEAD_DIM)
    r = jax.nn.sigmoid(jnp.einsum('bthi,hij->bthj', uh, wa).reshape(bsz, t, dr) + ba)
    i = jax.nn.sigmoid(jnp.einsum('bthi,hij->bthj', uh, wx).reshape(bsz, t, dr) + bx)
    log_a = (-LRU_C * r.astype(jnp.float32)) * jax.nn.softplus(-lam.astype(jnp.float32))
    a = jnp.exp(log_a)
    b = jnp.sqrt(-jnp.expm1(2.0 * log_a)) * (i * u).astype(jnp.float32)

    def step(h, ab):
        a_t, b_t = ab
        h = a_t * h + b_t
        return h, h

    h_last, hs = lax.scan(step, h0.astype(jnp.float32), (jnp.swapaxes(a, 0, 1), jnp.swapaxes(b, 0, 1)))
    return jnp.swapaxes(hs, 0, 1).astype(u.dtype), h_last.astype(u.dtype)


def even_mixer(x, buf_a, buf_b, h0, w_in, conv_a_w, conv_b_w, conv_b_b, lru_wa, lru_ba, lru_wx, lru_bx,
               lru_lambda, w_out):
    z = x @ w_in
    g_b, g_c, v_a, gate_b, v_b = jnp.split(z, [D_A, 2 * D_A, 3 * D_A, 3 * D_A + D_B], axis=-1)
    u_a, new_buf_a = causal_dwconv(g_c * v_a, buf_a, conv_a_w)
    y_a = g_b * u_a
    u_b, new_buf_b = causal_dwconv(v_b, buf_b, conv_b_w, conv_b_b)
    y_lru, h_last = rg_lru(u_b, h0, lru_wa, lru_ba, lru_wx, lru_bx, lru_lambda)
    y_b = jax.nn.gelu(gate_b) * y_lru
    return jnp.concatenate([y_a, y_b], axis=-1) @ w_out, new_buf_a, new_buf_b, h_last


def mla_attend(q_lat, q_pe, ckv, kpe, q_pos, k_pos):
    def block(args):
        ql, qp, qpos = args
        s = (jnp.einsum('bthr,bsr->bhts', ql, ckv) + jnp.einsum('bthp,bsp->bhts', qp, kpe)).astype(jnp.float32)
        s = jnp.where(k_pos[None, None, None, :] <= qpos[None, None, :, None], s * ATTN_SCALE, -jnp.inf)
        p = jax.nn.softmax(s, axis=-1).astype(ckv.dtype)
        return jnp.einsum('bhts,bsr->bthr', p, ckv)

    t = q_lat.shape[1]
    if t > Q_BLOCK and t % Q_BLOCK == 0:
        nb = t // Q_BLOCK

        def split(a):
            return jnp.moveaxis(a.reshape(a.shape[0], nb, Q_BLOCK, *a.shape[2:]), 1, 0)

        o = lax.map(block, (split(q_lat), split(q_pe), q_pos.reshape(nb, Q_BLOCK)))
        return jnp.moveaxis(o, 0, 1).reshape(q_lat.shape)
    return block((q_lat, q_pe, q_pos))


def odd_mixer(x, buf_c, past_ckv, past_kpe, pos0, w_in, conv_c_w, conv_c_b, ln_c_g, ln_c_b, q_norm_g, w_q_b,
              kv_norm_g, w_uk, w_uv, w_out):
    bsz, t, _ = x.shape
    z = x @ w_in
    glu_a, glu_b, q_a, kv_a, k_pe = jnp.split(
        z, [D_C, 2 * D_C, 2 * D_C + Q_RANK, 2 * D_C + Q_RANK + KV_RANK], axis=-1)
    u_c, new_buf_c = causal_dwconv(glu_a * jax.nn.sigmoid(glu_b), buf_c, conv_c_w, conv_c_b)
    y_c = jax.nn.silu(layer_norm(u_c, ln_c_g, ln_c_b))
    pos = pos0 + jnp.arange(t)
    q = (rms_norm(q_a, q_norm_g) @ w_q_b).reshape(bsz, t, MLA_HEADS, QK_NOPE + QK_ROPE)
    q_nope, q_pe = q[..., :QK_NOPE], apply_rope(q[..., QK_NOPE:], pos)
    ckv_new = rms_norm(kv_a, kv_norm_g)
    kpe_new = apply_rope(k_pe, pos)
    if past_ckv is None:
        ckv_all, kpe_all, k_pos = ckv_new, kpe_new, pos
    else:
        ckv_all = jnp.concatenate([past_ckv.astype(x.dtype), ckv_new], axis=1)
        kpe_all = jnp.concatenate([past_kpe.astype(x.dtype), kpe_new], axis=1)
        k_pos = jnp.arange(past_ckv.shape[1] + t)
    q_lat = jnp.einsum('bthn,rhn->bthr', q_nope, w_uk)
    o_lat = mla_attend(q_lat, q_pe, ckv_all, kpe_all, pos, k_pos)
    y_d = jnp.einsum('bthr,rhv->bthv', o_lat, w_uv).reshape(bsz, t, MLA_HEADS * V_DIM)
    return jnp.concatenate([y_c, y_d], axis=-1) @ w_out, new_buf_c, ckv_new, kpe_new


def moe(x, w_router, b_router, w_exp_in, w_exp_out):
    shape = x.shape
    xt = x.reshape(-1, shape[-1])
    n_tok = xt.shape[0]
    scores = jax.nn.sigmoid((xt @ w_router).astype(jnp.float32))
    sel = (scores + b_router.astype(jnp.float32)).reshape(n_tok, N_GROUPS, EXPERTS_PER_GROUP)
    group_score = lax.top_k(sel, 2)[0].sum(-1)
    grp = jnp.argmax(group_score, axis=-1).astype(jnp.int32)
    in_grp = jnp.take_along_axis(sel, grp[:, None, None], axis=1)[:, 0]
    _, local = lax.top_k(in_grp, TOP_K)
    expert_idx = grp[:, None] * EXPERTS_PER_GROUP + local.astype(jnp.int32)
    gate = jnp.take_along_axis(scores, expert_idx, axis=1)
    gate = gate / gate.sum(-1, keepdims=True)
    n_assign = n_tok * TOP_K
    flat_e = expert_idx.reshape(-1)
    order = jnp.argsort(flat_e).astype(jnp.int32)
    sorted_e = flat_e[order]
    counts = jnp.bincount(flat_e, length=N_EXPERTS)
    starts = jnp.cumsum(counts) - counts
    padded = (counts + MOE_BLOCK - 1) // MOE_BLOCK * MOE_BLOCK
    pad_ends = jnp.cumsum(padded)
    pad_starts = pad_ends - padded
    dest_sorted = (pad_starts[sorted_e] + jnp.arange(n_assign) - starts[sorted_e]).astype(jnp.int32)
    pad_total = (n_assign + N_EXPERTS * (MOE_BLOCK - 1) + MOE_BLOCK - 1) // MOE_BLOCK * MOE_BLOCK
    n_blocks = pad_total // MOE_BLOCK
    tok_buf = jnp.full((pad_total,), n_tok, jnp.int32).at[dest_sorted].set(order // TOP_K)
    dest = jnp.zeros((n_assign,), jnp.int32).at[order].set(dest_sorted)
    blk_e = jnp.minimum(jnp.searchsorted(pad_ends, jnp.arange(n_blocks) * MOE_BLOCK, side='right'),
                        N_EXPERTS - 1)
    x_pad = jnp.concatenate([xt, jnp.zeros((1, xt.shape[1]), xt.dtype)], axis=0)
    xb = x_pad[tok_buf].reshape(n_blocks, MOE_BLOCK, xt.shape[1])

    def expert_block(args):
        xblk, e = args
        hg, hu = jnp.split(xblk @ w_exp_in[e], 2, axis=-1)
        return (jax.nn.silu(hg) * hu) @ w_exp_out[e]

    yb = lax.map(expert_block, (xb, blk_e)).reshape(pad_total, -1)
    y_assign = yb[dest].reshape(n_tok, TOP_K, -1)
    return jnp.einsum('tk,tkd->td', gate.astype(x.dtype), y_assign).reshape(shape)


def setup_inputs(seed: int = 0) -> dict:
    key = jax.random.key(seed)
    ks = iter(jax.random.split(key, 48))

    def nrm(shape, scale):
        return jax.random.normal(next(ks), shape, jnp.float32) * scale

    n_pages = PAST_LEN // PAGE_SIZE
    n_used = DEC_BATCH * n_pages
    n_phys = n_used + max(1, n_used // 4)
    page_table = jax.random.permutation(next(ks), n_phys)[:n_used].reshape(DEC_BATCH, n_pages).astype(jnp.int32)
    u = jax.random.uniform(next(ks), (N_EVEN, D_B), jnp.float32, 0.9, 0.999)
    a_base = u ** (1.0 / LRU_C)
    lru_lambda = jnp.log(a_base) - jnp.log1p(-a_base)
    return {
        'x_prompt': nrm((BATCH, SEQ, D_MODEL), 1.0),
        'x_sample': nrm((DEC_BATCH, DEC_SEQ, D_MODEL), 1.0),
        'state_conv_a': nrm((N_EVEN, DEC_BATCH, CONV_A - 1, D_A), 1.0),
        'state_conv_b': nrm((N_EVEN, DEC_BATCH, CONV_B - 1, D_B), 1.0),
        'state_rglru_h': nrm((N_EVEN, DEC_BATCH, D_B), 0.5),
        'state_conv_c': nrm((N_ODD, DEC_BATCH, CONV_C - 1, D_C), 1.0),
        'cache_ckv': nrm((n_phys, N_ODD, PAGE_SIZE, KV_RANK), 1.0),
        'cache_kpe': nrm((n_phys, N_ODD, PAGE_SIZE, QK_ROPE), 1.0),
        'page_table': page_table,
        'w_in_even': nrm((N_EVEN, D_MODEL, EVEN_IN), D_MODEL ** -0.5),
        'conv_a_w': nrm((N_EVEN, CONV_A, D_A), CONV_A ** -0.5),
        'conv_b_w': nrm((N_EVEN, CONV_B, D_B), CONV_B ** -0.5),
        'conv_b_b': nrm((N_EVEN, D_B), 0.02),
        'lru_wa': nrm((N_EVEN, B_HEADS, B_HEAD_DIM, B_HEAD_DIM), B_HEAD_DIM ** -0.5),
        'lru_ba': nrm((N_EVEN, D_B), 0.1),
        'lru_wx': nrm((N_EVEN, B_HEADS, B_HEAD_DIM, B_HEAD_DIM), B_HEAD_DIM ** -0.5),
        'lru_bx': nrm((N_EVEN, D_B), 0.1),
        'lru_lambda': lru_lambda,
        'w_out_even': nrm((N_EVEN, MIX_OUT_EVEN, D_MODEL), BETA * MIX_OUT_EVEN ** -0.5),
        'w_in_odd': nrm((N_ODD, D_MODEL, ODD_IN), D_MODEL ** -0.5),
        'conv_c_w': nrm((N_ODD, CONV_C, D_C), CONV_C ** -0.5),
        'conv_c_b': nrm((N_ODD, D_C), 0.02),
        'ln_c_g': 1.0 + nrm((N_ODD, D_C), 0.02),
        'ln_c_b': nrm((N_ODD, D_C), 0.02),
        'q_norm_g': 1.0 + nrm((N_ODD, Q_RANK), 0.02),
        'w_q_b': nrm((N_ODD, Q_RANK, MLA_HEADS * (QK_NOPE + QK_ROPE)), Q_RANK ** -0.5),
        'kv_norm_g': 1.0 + nrm((N_ODD, KV_RANK), 0.02),
        'w_uk': nrm((N_ODD, KV_RANK, MLA_HEADS, QK_NOPE), KV_RANK ** -0.5),
        'w_uv': nrm((N_ODD, KV_RANK, MLA_HEADS, V_DIM), BETA * KV_RANK ** -0.5),
        'w_out_odd': nrm((N_ODD, MIX_OUT_ODD, D_MODEL), BETA * MIX_OUT_ODD ** -0.5),
        'ln_mix_g': 1.0 + nrm((DEPTH, D_MODEL), 0.02),
        'ln_mix_b': nrm((DEPTH, D_MODEL), 0.02),
        'ln_ffn_g': 1.0 + nrm((DEPTH, D_MODEL), 0.02),
        'ln_ffn_b': nrm((DEPTH, D_MODEL), 0.02),
        'w_router': nrm((D_MODEL, N_EXPERTS), D_MODEL ** -0.5),
        'b_router': nrm((N_EXPERTS,), 0.01),
        'w_exp_in': nrm((DEPTH, N_EXPERTS, D_MODEL, 2 * D_EXPERT), BETA * D_MODEL ** -0.5),
        'w_exp_out': nrm((DEPTH, N_EXPERTS, D_EXPERT, D_MODEL), BETA * D_EXPERT ** -0.5),
    }


def reference(x_prompt, x_sample, state_conv_a, state_conv_b, state_rglru_h, state_conv_c, cache_ckv, cache_kpe,
              page_table, w_in_even, conv_a_w, conv_b_w, conv_b_b, lru_wa, lru_ba, lru_wx, lru_bx, lru_lambda,
              w_out_even, w_in_odd, conv_c_w, conv_c_b, ln_c_g, ln_c_b, q_norm_g, w_q_b, kv_norm_g, w_uk, w_uv,
              w_out_odd, ln_mix_g, ln_mix_b, ln_ffn_g, ln_ffn_b, w_router, b_router, w_exp_in, w_exp_out):
    bp, dbs = x_prompt.shape[0], x_sample.shape[0]
    past_len = page_table.shape[1] * PAGE_SIZE
    dt = x_prompt.dtype
    xp, xs = x_prompt, x_sample
    ca_p, ca_s, cb_p, cb_s, h_p, h_s = [], [], [], [], [], []
    cc_p, cc_s, ckv_p, ckv_s, kpe_p, kpe_s = [], [], [], [], [], []
    for l in range(DEPTH):
        j = l // 2
        if l % 2 == 0:
            ev = (w_in_even[j], conv_a_w[j], conv_b_w[j], conv_b_b[j], lru_wa[j], lru_ba[j], lru_wx[j], lru_bx[j],
                  lru_lambda[j], w_out_even[j])
            mp, a1, b1, h1 = even_mixer(xp, jnp.zeros((bp, CONV_A - 1, D_A), dt), jnp.zeros((bp, CONV_B - 1, D_B), dt),
                                        jnp.zeros((bp, D_B), dt), *ev)
            ms, a2, b2, h2 = even_mixer(xs, state_conv_a[j], state_conv_b[j], state_rglru_h[j], *ev)
            ca_p.append(a1); ca_s.append(a2); cb_p.append(b1); cb_s.append(b2); h_p.append(h1); h_s.append(h2)
        else:
            od = (w_in_odd[j], conv_c_w[j], conv_c_b[j], ln_c_g[j], ln_c_b[j], q_norm_g[j], w_q_b[j], kv_norm_g[j],
                  w_uk[j], w_uv[j], w_out_odd[j])
            mp, c1, k1, p1 = odd_mixer(xp, jnp.zeros((bp, CONV_C - 1, D_C), dt), None, None, 0, *od)
            past_ckv = cache_ckv[page_table, j].reshape(dbs, past_len, KV_RANK)
            past_kpe = cache_kpe[page_table, j].reshape(dbs, past_len, QK_ROPE)
            ms, c2, k2, p2 = odd_mixer(xs, state_conv_c[j], past_ckv, past_kpe, past_len, *od)
            cc_p.append(c1); cc_s.append(c2); ckv_p.append(k1); ckv_s.append(k2); kpe_p.append(p1); kpe_s.append(p2)
        xp = layer_norm(ALPHA * xp + mp, ln_mix_g[l], ln_mix_b[l])
        xs = layer_norm(ALPHA * xs + ms, ln_mix_g[l], ln_mix_b[l])
        xp = layer_norm(ALPHA * xp + moe(xp, w_router, b_router, w_exp_in[l], w_exp_out[l]), ln_ffn_g[l], ln_ffn_b[l])
        xs = layer_norm(ALPHA * xs + moe(xs, w_router, b_router, w_exp_in[l], w_exp_out[l]), ln_ffn_g[l], ln_ffn_b[l])
    y_prompt, y_sample = xp, xs
    new_conv_a_prompt, new_conv_a_sample = jnp.stack(ca_p), jnp.stack(ca_s)
    new_conv_b_prompt, new_conv_b_sample = jnp.stack(cb_p), jnp.stack(cb_s)
    new_h_prompt, new_h_sample = jnp.stack(h_p), jnp.stack(h_s)
    new_conv_c_prompt, new_conv_c_sample = jnp.stack(cc_p), jnp.stack(cc_s)
    new_ckv_prompt, new_ckv_sample = jnp.stack(ckv_p, axis=1), jnp.stack(ckv_s, axis=1)
    new_kpe_prompt, new_kpe_sample = jnp.stack(kpe_p, axis=1), jnp.stack(kpe_s, axis=1)
    return (y_prompt, y_sample, new_conv_a_prompt, new_conv_a_sample, new_conv_b_prompt, new_conv_b_sample,
            new_h_prompt, new_h_sample, new_conv_c_prompt, new_conv_c_sample, new_ckv_prompt, new_ckv_sample,
            new_kpe_prompt, new_kpe_sample)
```

```python
import functools
import math

import jax
import jax.numpy as jnp
from jax import lax
from jax.experimental import pallas as pl
from jax.experimental.pallas import tpu as pltpu

F32 = jnp.float32
BF16 = jnp.bfloat16

DEPTH = 4
N_HEADS = 8
QK_NOPE = 64
QK_ROPE = 32
V_DIM = 64
N_EXPERTS = 16
N_GROUPS = 4
EXPERTS_PER_GROUP = 4
LRU_C = 8.0
ROPE_THETA = 10000.0
ATTN_SCALE = (QK_NOPE + QK_ROPE) ** -0.5
ALPHA = (2 * DEPTH) ** 0.25
PAGE_SIZE = 128

SUBLANES = 8
LANES = 128
VMEM_LIMIT_BYTES = 56 * 1024 * 1024

TIME_STEPS_PER_TILE = 32
TOKEN_TILE = 512
ATTN_TILE = 512
PAGES_PER_STEP = 8
NEG_INF = float("-inf")


def _params(semantics):
    return pltpu.CompilerParams(dimension_semantics=semantics, vmem_limit_bytes=VMEM_LIMIT_BYTES)


def _full(shape):
    nd = len(shape)
    return pl.BlockSpec(shape, lambda *_: (0,) * nd)


def _layer_norm(x, g, b, eps=1e-5):
    mu = jnp.mean(x, axis=-1, keepdims=True)
    xc = x - mu
    var = jnp.mean(xc * xc, axis=-1, keepdims=True)
    return xc * lax.rsqrt(var + eps) * g + b


def _rms_norm(x, g, eps=1e-6):
    return x * lax.rsqrt(jnp.mean(x * x, axis=-1, keepdims=True) + eps) * g


def _sigmoid(x):
    return 1.0 / (1.0 + jnp.exp(-x))


def _dot(a, b):
    return jnp.dot(a, b, preferred_element_type=F32)


def _dot_nt(a, b):
    return lax.dot_general(a, b, (((1,), (1,)), ((), ())), preferred_element_type=F32)


def _even_kernel(x_ref, win_ref, caw_ref, cbw_ref, cbb_ref, wg_ref, ba_ref, bx_ref, lam_ref, wout_ref,
                 g_ref, b_ref, sa_ref, sb_ref, h0_ref,
                 o_ref, na_ref, nb_ref, hl_ref,
                 ua_ext, vb_ext, h_sc, hs_sc, *, nb, tt):
    tm = tt * nb
    dh = caw_ref.shape[1]
    half = dh // 2

    @pl.when(pl.program_id(0) == 0)
    def _():
        ua_ext[0:2 * nb, :] = sa_ref[...]
        vb_ext[0:3 * nb, :] = sb_ref[...]
        h_sc[...] = h0_ref[...]

    x = x_ref[...]
    xb = x.astype(BF16)

    def proj(j):
        return _dot(xb, win_ref[:, j * dh:(j + 1) * dh])

    ua_ext[2 * nb:2 * nb + tm, :] = proj(1) * proj(2)
    caw = caw_ref[...]
    conv_a = (caw[0:1] * ua_ext[0:tm, :] + caw[1:2] * ua_ext[nb:nb + tm, :]
              + caw[2:3] * ua_ext[2 * nb:2 * nb + tm, :])
    y_a = proj(0) * conv_a
    tail_a = ua_ext[tm:tm + 2 * nb, :]
    na_ref[...] = tail_a
    ua_ext[0:2 * nb, :] = tail_a

    vb_ext[3 * nb:3 * nb + tm, :] = proj(4)
    cbw = cbw_ref[...]
    u_b = (cbb_ref[...] + cbw[0:1] * vb_ext[0:tm, :] + cbw[1:2] * vb_ext[nb:nb + tm, :]
           + cbw[2:3] * vb_ext[2 * nb:2 * nb + tm, :] + cbw[3:4] * vb_ext[3 * nb:3 * nb + tm, :])
    tail_b = vb_ext[tm:tm + 3 * nb, :]
    nb_ref[...] = tail_b
    vb_ext[0:3 * nb, :] = tail_b

    ub16 = u_b.astype(BF16)
    gk0 = _dot(ub16[:, :half], wg_ref[0])
    gk1 = _dot(ub16[:, half:], wg_ref[1])
    r = _sigmoid(jnp.concatenate([gk0[:, :half], gk1[:, :half]], axis=1) + ba_ref[...])
    ig = _sigmoid(jnp.concatenate([gk0[:, half:], gk1[:, half:]], axis=1) + bx_ref[...])
    nlam = -lam_ref[...]
    softplus = jnp.maximum(nlam, 0.0) + jnp.log(1.0 + jnp.exp(-jnp.abs(nlam)))
    log_a = (-LRU_C * r) * softplus
    a = jnp.exp(log_a)
    bterm = jnp.sqrt(1.0 - a * a) * (ig * u_b)

    h = h_sc[...]
    for t in range(tt):
        h = a[t * nb:(t + 1) * nb, :] * h + bterm[t * nb:(t + 1) * nb, :]
        hs_sc[t * nb:(t + 1) * nb, :] = h
    h_sc[...] = h
    hl_ref[...] = h
    y_b = jax.nn.gelu(proj(3), approximate=True) * hs_sc[...]

    m = _dot(y_a.astype(BF16), wout_ref[0:dh, :]) + _dot(y_b.astype(BF16), wout_ref[dh:2 * dh, :])
    o_ref[...] = _layer_norm(ALPHA * x + m, g_ref[...], b_ref[...])


def _even_layer(x, w, sa, sb, h0, *, nb, n_t):
    m_rows, d = x.shape
    dh = w["caw"].shape[1]
    tt = min(n_t, TIME_STEPS_PER_TILE)
    assert n_t % tt == 0 and nb % SUBLANES == 0
    tm = tt * nb
    kern = functools.partial(_even_kernel, nb=nb, tt=tt)
    row = lambda i: (i, 0)
    in_specs = [pl.BlockSpec((tm, d), row), _full(w["win"].shape), _full(w["caw"].shape), _full(w["cbw"].shape),
                _full(w["cbb"].shape), _full(w["wg"].shape), _full(w["ba"].shape), _full(w["bx"].shape),
                _full(w["lam"].shape), _full(w["wout"].shape), _full(w["g"].shape), _full(w["b"].shape),
                _full(sa.shape), _full(sb.shape), _full(h0.shape)]
    out_shape = (jax.ShapeDtypeStruct((m_rows, d), F32), jax.ShapeDtypeStruct(sa.shape, F32),
                 jax.ShapeDtypeStruct(sb.shape, F32), jax.ShapeDtypeStruct(h0.shape, F32))
    out_specs = (pl.BlockSpec((tm, d), row), _full(sa.shape), _full(sb.shape), _full(h0.shape))
    scratch = [pltpu.VMEM((tm + 2 * nb, dh), F32), pltpu.VMEM((tm + 3 * nb, dh), F32),
               pltpu.VMEM((nb, dh), F32), pltpu.VMEM((tm, dh), F32)]
    return pl.pallas_call(
        kern, grid=(m_rows // tm,), in_specs=in_specs, out_specs=out_specs, out_shape=out_shape,
        scratch_shapes=scratch, compiler_params=_params(("arbitrary",)), name="even_mixer",
    )(x, w["win"], w["caw"], w["cbw"], w["cbb"], w["wg"], w["ba"], w["bx"], w["lam"], w["wout"], w["g"], w["b"],
      sa, sb, h0)


def _odd_pre_kernel(x_ref, win_ref, ccw_ref, ccb_ref, lcg_ref, lcb_ref, qg_ref, wqb_ref, kvg_ref, wuk_ref,
                    cos_ref, sin_ref, sc_ref,
                    yc_ref, q_ref, k_ref, ckv_ref, kpe_ref, nc_ref,
                    c_ext, *, nb, tt, width):
    tm = tt * nb
    dc = ccw_ref.shape[1]
    q_rank = qg_ref.shape[1]
    kv_rank = kvg_ref.shape[1]
    pe_all = N_HEADS * QK_ROPE
    hist = (width - 1) * nb

    @pl.when(pl.program_id(0) == 0)
    def _():
        c_ext[0:hist, :] = sc_ref[...]

    xb = x_ref[...].astype(BF16)
    o_q = 2 * dc
    o_kv = o_q + q_rank
    o_pe = o_kv + kv_rank

    glu = _dot(xb, win_ref[:, 0:dc]) * _sigmoid(_dot(xb, win_ref[:, dc:2 * dc]))
    c_ext[hist:hist + tm, :] = glu
    ccw = ccw_ref[...]
    u_c = ccb_ref[...] + ccw[0:1] * c_ext[0:tm, :]
    for k in range(1, width):
        u_c = u_c + ccw[k:k + 1] * c_ext[k * nb:k * nb + tm, :]
    tail = c_ext[tm:tm + hist, :]
    nc_ref[...] = tail
    c_ext[0:hist, :] = tail
    ln = _layer_norm(u_c, lcg_ref[...], lcb_ref[...])
    yc_ref[...] = ln * _sigmoid(ln)

    cos = jnp.broadcast_to(cos_ref[...][:, None, :], (tt, nb, pe_all)).reshape(tm, pe_all)
    sin = jnp.broadcast_to(sin_ref[...][:, None, :], (tt, nb, pe_all)).reshape(tm, pe_all)
    lane = lax.broadcasted_iota(jnp.int32, (1, pe_all), 1)
    first_half = (lane % QK_ROPE) < (QK_ROPE // 2)
    head_of_lane = lane // QK_ROPE

    def rope(v):
        swapped = jnp.where(first_half, pltpu.roll(v, pe_all - QK_ROPE // 2, 1), pltpu.roll(v, QK_ROPE // 2, 1))
        return v * cos + swapped * sin

    ckv = _rms_norm(_dot(xb, win_ref[:, o_kv:o_kv + kv_rank]), kvg_ref[...])
    kpe = rope(_dot(xb, win_ref[:, o_pe:o_pe + pe_all]))
    ckv_ref[...] = ckv
    kpe_ref[...] = kpe[:, 0:QK_ROPE]
    k_ref[:, 0:kv_rank] = ckv.astype(BF16)
    k_ref[:, kv_rank:kv_rank + pe_all] = kpe.astype(BF16)

    qn = _rms_norm(_dot(xb, win_ref[:, o_q:o_q + q_rank]), qg_ref[...]).astype(BF16)
    nope_all = N_HEADS * QK_NOPE
    q_nope = (_dot(qn, wqb_ref[:, 0:nope_all]) * ATTN_SCALE).astype(BF16)
    q_pe = rope(_dot(qn, wqb_ref[:, nope_all:nope_all + pe_all]) * ATTN_SCALE)
    for p in range(N_HEADS // 2):
        lat2 = _dot(q_nope[:, p * 2 * QK_NOPE:(p + 1) * 2 * QK_NOPE], wuk_ref[p])
        for s in range(2):
            h = 2 * p + s
            q_ref[h, :, 0:kv_rank] = lat2[:, s * kv_rank:(s + 1) * kv_rank].astype(BF16)
            q_ref[h, :, kv_rank:kv_rank + pe_all] = jnp.where(head_of_lane == h, q_pe, 0.0).astype(BF16)


def _odd_pre(x, w, sc, cos, sin, *, nb, n_t):
    m_rows, d = x.shape
    width, dc = w["ccw"].shape
    kv_rank = w["kvg"].shape[1]
    pe_all = N_HEADS * QK_ROPE
    kq = kv_rank + pe_all
    tt = min(n_t, TIME_STEPS_PER_TILE)
    assert n_t % tt == 0 and nb % SUBLANES == 0
    tm = tt * nb
    kern = functools.partial(_odd_pre_kernel, nb=nb, tt=tt, width=width)
    row = lambda i: (i, 0)
    in_specs = [pl.BlockSpec((tm, d), row)] + [_full(w[k].shape) for k in
                                                ("win", "ccw", "ccb", "lcg", "lcb", "qg", "wqb", "kvg", "wuk")]
    in_specs += [pl.BlockSpec((tt, pe_all), row), pl.BlockSpec((tt, pe_all), row), _full(sc.shape)]
    out_shape = (jax.ShapeDtypeStruct((m_rows, dc), F32),
                 jax.ShapeDtypeStruct((N_HEADS, m_rows, kq), BF16),
                 jax.ShapeDtypeStruct((m_rows, kq), BF16),
                 jax.ShapeDtypeStruct((m_rows, kv_rank), F32),
                 jax.ShapeDtypeStruct((m_rows, QK_ROPE), F32),
                 jax.ShapeDtypeStruct(sc.shape, F32))
    out_specs = (pl.BlockSpec((tm, dc), row), pl.BlockSpec((N_HEADS, tm, kq), lambda i: (0, i, 0)),
                 pl.BlockSpec((tm, kq), row), pl.BlockSpec((tm, kv_rank), row), pl.BlockSpec((tm, QK_ROPE), row),
                 _full(sc.shape))
    scratch = [pltpu.VMEM((tm + (width - 1) * nb, dc), F32)]
    return pl.pallas_call(
        kern, grid=(m_rows // tm,), in_specs=in_specs, out_specs=out_specs, out_shape=out_shape,
        scratch_shapes=scratch, compiler_params=_params(("arbitrary",)), name="odd_pre",
    )(x, w["win"], w["ccw"], w["ccb"], w["lcg"], w["lcb"], w["qg"], w["wqb"], w["kvg"], w["wuk"], cos, sin, sc)


def _softmax_step(s, v16, m_sc, l_sc, acc_sc, h):
    m_prev = m_sc[h]
    m_new = jnp.maximum(m_prev, jnp.max(s, axis=-1, keepdims=True))
    alpha = jnp.exp(m_prev - m_new)
    p = jnp.exp(s - m_new)
    l_sc[h] = alpha * l_sc[h] + jnp.sum(p, axis=-1, keepdims=True)
    acc_sc[h] = alpha * acc_sc[h] + _dot(p.astype(BF16), v16)
    m_sc[h] = m_new


def _prompt_attn_kernel(qi_ref, ki_ref, q_ref, k_ref, wuv_ref, o_ref, m_sc, l_sc, acc_sc, *, tile, kv_rank):
    j = pl.program_id(1)
    qi = qi_ref[j]
    ki = ki_ref[j]

    @pl.when(ki == 0)
    def _():
        m_sc[...] = jnp.full(m_sc.shape, NEG_INF, F32)
        l_sc[...] = jnp.zeros(l_sc.shape, F32)
        acc_sc[...] = jnp.zeros(acc_sc.shape, F32)

    k = k_ref[...]
    v16 = k[:, 0:kv_rank]
    row = qi * tile + lax.broadcasted_iota(jnp.int32, (tile, tile), 0)
    col = ki * tile + lax.broadcasted_iota(jnp.int32, (tile, tile), 1)
    visible = col <= row

    def head(h, carry):
        s = jnp.where(visible, _dot_nt(q_ref[h], k), NEG_INF)
        _softmax_step(s, v16, m_sc, l_sc, acc_sc, h)
        return carry

    lax.fori_loop(0, N_HEADS, head, 0)

    @pl.when(ki == qi)
    def _():
        for p in range(N_HEADS // 2):
            o2 = jnp.concatenate([acc_sc[2 * p] / l_sc[2 * p], acc_sc[2 * p + 1] / l_sc[2 * p + 1]], axis=1)
            o_ref[:, p * 2 * V_DIM:(p + 1) * 2 * V_DIM] = _dot(o2.astype(BF16), wuv_ref[p])


def _prompt_attention(q, k, wuv, *, nb, n_t):
    n_heads, m_rows, kq = q.shape
    kv_rank = wuv.shape[1] // 2
    dv = N_HEADS * V_DIM
    tile = min(n_t, ATTN_TILE)
    assert n_t % tile == 0
    nq = n_t // tile
    pairs = [(a, b) for a in range(nq) for b in range(a + 1)]
    qi_tab = jnp.asarray([p[0] for p in pairs], jnp.int32)
    ki_tab = jnp.asarray([p[1] for p in pairs], jnp.int32)
    q3 = q.reshape(n_heads, n_t, nb * kq)
    k2 = k.reshape(n_t, nb * kq)
    kern = functools.partial(_prompt_attn_kernel, tile=tile, kv_rank=kv_rank)
    grid_spec = pltpu.PrefetchScalarGridSpec(
        num_scalar_prefetch=2, grid=(nb, len(pairs)),
        in_specs=[pl.BlockSpec((n_heads, tile, kq), lambda b, j, qi, ki: (0, qi[j], b)),
                  pl.BlockSpec((tile, kq), lambda b, j, qi, ki: (ki[j], b)),
                  pl.BlockSpec(wuv.shape, lambda b, j, qi, ki: (0, 0, 0))],
        out_specs=pl.BlockSpec((tile, dv), lambda b, j, qi, ki: (qi[j], b)),
        scratch_shapes=[pltpu.VMEM((n_heads, tile, 1), F32), pltpu.VMEM((n_heads, tile, 1), F32),
                        pltpu.VMEM((n_heads, tile, kv_rank), F32)])
    out = pl.pallas_call(
        kern, grid_spec=grid_spec, out_shape=jax.ShapeDtypeStruct((n_t, nb * dv), F32),
        compiler_params=_params(("arbitrary", "arbitrary")), name="prompt_attention",
    )(qi_tab, ki_tab, q3, k2, wuv)
    return out.reshape(m_rows, dv)


def _sample_attn_kernel(pt_ref, q_ref, kn_ref, tsel_ref, wuv_ref, *rest, n_pages_step, kv_rank, n_new, rows_per_head):
    ckv_refs = rest[:n_pages_step]
    kpe_refs = rest[n_pages_step:2 * n_pages_step]
    o_ref, qc_sc, m_sc, l_sc, acc_sc = rest[2 * n_pages_step:]
    g = pl.program_id(1)

    @pl.when(g == 0)
    def _():
        m_sc[...] = jnp.full(m_sc.shape, NEG_INF, F32)
        l_sc[...] = jnp.zeros(l_sc.shape, F32)
        acc_sc[...] = jnp.zeros(acc_sc.shape, F32)
        qc_sc[...] = _dot(q_ref[:, kv_rank:], tsel_ref[...]).astype(BF16)

    q = q_ref[...]
    ck = jnp.concatenate([r[...] for r in ckv_refs], axis=0).astype(BF16)
    kp = jnp.concatenate([r[...] for r in kpe_refs], axis=0).astype(BF16)
    s = _dot_nt(q[:, 0:kv_rank], ck) + _dot_nt(qc_sc[...], kp)
    _softmax_step(s, ck, m_sc, l_sc, acc_sc, 0)

    @pl.when(g == pl.num_programs(1) - 1)
    def _():
        kn = kn_ref[...]
        s_new = _dot_nt(q, kn)
        t_row = lax.broadcasted_iota(jnp.int32, s_new.shape, 0) % rows_per_head
        t_col = lax.broadcasted_iota(jnp.int32, s_new.shape, 1)
        s_new = jnp.where((t_col <= t_row) & (t_col < n_new), s_new, NEG_INF)
        _softmax_step(s_new, kn[:, 0:kv_rank], m_sc, l_sc, acc_sc, 0)
        o = acc_sc[0] / l_sc[0]
        rp = rows_per_head
        for p in range(N_HEADS // 2):
            o2 = jnp.concatenate([o[2 * p * rp:(2 * p + 1) * rp], o[(2 * p + 1) * rp:(2 * p + 2) * rp]], axis=1)
            o_ref[:, p * 2 * V_DIM:(p + 1) * 2 * V_DIM] = _dot(o2.astype(BF16), wuv_ref[p])


def _sample_attention(q, k, wuv, cache_ckv, cache_kpe, page_table, layer, *, nb, n_t):
    n_heads, m_rows, kq = q.shape
    kv_rank = wuv.shape[1] // 2
    dv = N_HEADS * V_DIM
    n_pages = page_table.shape[1]
    gp = min(PAGES_PER_STEP, n_pages)
    assert n_pages % gp == 0 and n_t <= SUBLANES
    rp = SUBLANES
    qb = q.reshape(n_heads, n_t, nb, kq).transpose(2, 0, 1, 3)
    qb = jnp.pad(qb, ((0, 0), (0, 0), (0, rp - n_t), (0, 0))).reshape(nb, n_heads * rp, kq)
    kb = jnp.pad(k.reshape(n_t, nb, kq).transpose(1, 0, 2), ((0, 0), (0, rp - n_t), (0, 0)))
    tsel = jnp.tile(jnp.eye(QK_ROPE, dtype=BF16), (N_HEADS, 1))
    kern = functools.partial(_sample_attn_kernel, n_pages_step=gp, kv_rank=kv_rank, n_new=n_t, rows_per_head=rp)

    def page_map(i):
        return lambda b, g, pt: (pt[b, g * gp + i], layer, 0, 0)

    in_specs = [pl.BlockSpec((None, n_heads * rp, kq), lambda b, g, pt: (b, 0, 0)),
                pl.BlockSpec((None, rp, kq), lambda b, g, pt: (b, 0, 0)),
                pl.BlockSpec(tsel.shape, lambda b, g, pt: (0, 0)),
                pl.BlockSpec(wuv.shape, lambda b, g, pt: (0, 0, 0))]
    in_specs += [pl.BlockSpec((None, None, PAGE_SIZE, kv_rank), page_map(i)) for i in range(gp)]
    in_specs += [pl.BlockSpec((None, None, PAGE_SIZE, QK_ROPE), page_map(i)) for i in range(gp)]
    grid_spec = pltpu.PrefetchScalarGridSpec(
        num_scalar_prefetch=1, grid=(nb, n_pages // gp), in_specs=in_specs,
        out_specs=pl.BlockSpec((None, rp, dv), lambda b, g, pt: (b, 0, 0)),
        scratch_shapes=[pltpu.VMEM((n_heads * rp, QK_ROPE), BF16), pltpu.VMEM((1, n_heads * rp, 1), F32),
                        pltpu.VMEM((1, n_heads * rp, 1), F32), pltpu.VMEM((1, n_heads * rp, kv_rank), F32)])
    out = pl.pallas_call(
        kern, grid_spec=grid_spec, out_shape=jax.ShapeDtypeStruct((nb, rp, dv), F32),
        compiler_params=_params(("arbitrary", "arbitrary")), name="sample_attention",
    )(page_table, qb, kb, tsel, wuv, *([cache_ckv] * gp), *([cache_kpe] * gp))
    return out[:, :n_t].transpose(1, 0, 2).reshape(m_rows, dv)


def _odd_post_kernel(x_ref, yc_ref, yd_ref, wout_ref, g_ref, b_ref, o_ref):
    dc = yc_ref.shape[1]
    m = _dot(yc_ref[...].astype(BF16), wout_ref[0:dc, :]) + _dot(yd_ref[...].astype(BF16), wout_ref[dc:, :])
    o_ref[...] = _layer_norm(ALPHA * x_ref[...] + m, g_ref[...], b_ref[...])


def _token_tile(m_rows):
    tm = min(m_rows, TOKEN_TILE)
    assert m_rows % tm == 0
    return tm


def _odd_post(x, yc, yd, wout, g, b):
    m_rows, d = x.shape
    tm = _token_tile(m_rows)
    row = lambda i: (i, 0)
    return pl.pallas_call(
        _odd_post_kernel, grid=(m_rows // tm,),
        in_specs=[pl.BlockSpec((tm, d), row), pl.BlockSpec((tm, yc.shape[1]), row), pl.BlockSpec((tm, yd.shape[1]), row),
                  _full(wout.shape), _full(g.shape), _full(b.shape)],
        out_specs=pl.BlockSpec((tm, d), row), out_shape=jax.ShapeDtypeStruct((m_rows, d), F32),
        compiler_params=_params(("parallel",)), name="odd_post",
    )(x, yc, yd, wout, g, b)


def _router_kernel(x_ref, whi_ref, wlo_ref, br_ref, tri_ref, o_ref, cnt_ref, base_sc):
    tm = x_ref.shape[0]

    @pl.when(pl.program_id(0) == 0)
    def _():
        base_sc[...] = jnp.zeros(base_sc.shape, F32)

    x = x_ref[...]
    x_hi = x.astype(BF16)
    x_lo = (x - x_hi.astype(F32)).astype(BF16)
    logits = _dot(x_hi, whi_ref[...]) + (_dot(x_lo, whi_ref[...]) + _dot(x_hi, wlo_ref[...]))
    scores = _sigmoid(logits.T[0:N_EXPERTS, :])
    sel = scores + br_ref[...]

    def row(a, e):
        return a[e:e + 1, :]

    best_gs = None
    grp = None
    for gidx in range(N_GROUPS):
        v = [row(sel, gidx * EXPERTS_PER_GROUP + i) for i in range(EXPERTS_PER_GROUP)]
        gs = None
        for i in range(EXPERTS_PER_GROUP):
            for j in range(i + 1, EXPERTS_PER_GROUP):
                pair = v[i] + v[j]
                gs = pair if gs is None else jnp.maximum(gs, pair)
        if best_gs is None:
            best_gs, grp = gs, jnp.zeros(gs.shape, jnp.int32)
        else:
            better = gs > best_gs
            grp = jnp.where(better, gidx, grp)
            best_gs = jnp.where(better, gs, best_gs)

    def pick(a, i):
        out = row(a, i)
        for gidx in range(1, N_GROUPS):
            out = jnp.where(grp == gidx, row(a, gidx * EXPERTS_PER_GROUP + i), out)
        return out

    cand = [pick(sel, i) for i in range(EXPERTS_PER_GROUP)]
    cand_score = [pick(scores, i) for i in range(EXPERTS_PER_GROUP)]

    def argmax_first(vals, exclude=None):
        best, idx = None, None
        for i, v in enumerate(vals):
            if exclude is not None:
                v = jnp.where(exclude == i, NEG_INF, v)
            if best is None:
                best, idx = v, jnp.zeros(v.shape, jnp.int32)
            else:
                better = v > best
                idx = jnp.where(better, i, idx)
                best = jnp.where(better, v, best)
        return idx

    loc1 = argmax_first(cand)
    loc2 = argmax_first(cand, exclude=loc1)

    def take(vals, idx):
        out = vals[0]
        for i in range(1, len(vals)):
            out = jnp.where(idx == i, vals[i], out)
        return out

    g1 = take(cand_score, loc1)
    g2 = take(cand_score, loc2)
    gsum = g1 + g2
    e1 = grp * EXPERTS_PER_GROUP + loc1
    e2 = grp * EXPERTS_PER_GROUP + loc2

    e_iota = lax.broadcasted_iota(jnp.int32, (N_EXPERTS, tm), 0)
    hit1 = e_iota == e1
    hit2 = e_iota == e2
    both = jnp.where(hit1 | hit2, 1.0, 0.0)
    before = _dot(both.astype(BF16), tri_ref[...]) + base_sc[...]
    r1 = jnp.sum(jnp.where(hit1, before, 0.0), axis=0, keepdims=True)
    r2 = jnp.sum(jnp.where(hit2, before, 0.0), axis=0, keepdims=True)
    base_sc[...] = base_sc[...] + jnp.sum(both, axis=1, keepdims=True)
    cnt_ref[...] = jnp.broadcast_to(base_sc[...], cnt_ref.shape)

    zero = jnp.zeros((1, tm), F32)
    o_ref[...] = jnp.concatenate([e1.astype(F32), e2.astype(F32), g1 / gsum, g2 / gsum, r1, r2, zero, zero], axis=0)


def _router(x, whi, wlo, br):
    m_rows, d = x.shape
    tm = _token_tile(m_rows)
    tri = jnp.triu(jnp.ones((tm, tm), BF16), k=1)
    return pl.pallas_call(
        _router_kernel, grid=(m_rows // tm,),
        in_specs=[pl.BlockSpec((tm, d), lambda i: (i, 0)), _full(whi.shape), _full(wlo.shape), _full(br.shape),
                  _full(tri.shape)],
        out_specs=(pl.BlockSpec((SUBLANES, tm), lambda i: (0, i)), _full((N_EXPERTS, LANES))),
        out_shape=(jax.ShapeDtypeStruct((SUBLANES, m_rows), F32), jax.ShapeDtypeStruct((N_EXPERTS, LANES), F32)),
        scratch_shapes=[pltpu.VMEM((N_EXPERTS, 1), F32)],
        compiler_params=_params(("arbitrary",)), name="router",
    )(x, whi, wlo, br, tri)


def _row_copy(src_ref, src_row, dst_ref, dst_row, sem):
    return pltpu.make_async_copy(src_ref.at[pl.ds(src_row, 1), :], dst_ref.at[pl.ds(dst_row, 1), :], sem)


def _dispatch_kernel(dest_ref, x_ref, buf_in_ref, buf_ref, sem):
    del buf_in_ref
    tm = x_ref.shape[0]

    def issue(r, carry):
        _row_copy(x_ref, r, buf_ref, dest_ref[0, 0, r], sem).start()
        _row_copy(x_ref, r, buf_ref, dest_ref[0, 0, tm + r], sem).start()
        return carry

    lax.fori_loop(0, tm, issue, 0)

    def drain(r, carry):
        _row_copy(x_ref, 0, buf_ref, 0, sem).wait()
        _row_copy(x_ref, 0, buf_ref, 0, sem).wait()
        return carry

    lax.fori_loop(0, tm, drain, 0)


def _dispatch(x, dest_tiles, n_slots):
    m_rows, d = x.shape
    tm = dest_tiles.shape[2] // 2
    buf0 = jnp.zeros((n_slots, d), F32)
    return pl.pallas_call(
        _dispatch_kernel, grid=(m_rows // tm,),
        in_specs=[pl.BlockSpec((1, 1, 2 * tm), lambda i: (i, 0, 0), memory_space=pltpu.SMEM),
                  pl.BlockSpec((tm, d), lambda i: (i, 0)),
                  pl.BlockSpec(memory_space=pl.ANY)],
        out_specs=pl.BlockSpec(memory_space=pl.ANY),
        out_shape=jax.ShapeDtypeStruct((n_slots, d), F32),
        scratch_shapes=[pltpu.SemaphoreType.DMA(())],
        input_output_aliases={2: 0},
        compiler_params=_params(("arbitrary",)), name="moe_dispatch",
    )(dest_tiles, x, buf0)


def _expert_kernel(te_ref, tv_ref, x_ref, win_ref, wout_ref, o_ref):
    i = pl.program_id(0)
    de = wout_ref.shape[0]

    @pl.when(tv_ref[i] != 0)
    def _():
        h = _dot(x_ref[...].astype(BF16), win_ref[...])
        hg = h[:, 0:de]
        act = hg * _sigmoid(hg) * h[:, de:2 * de]
        o_ref[...] = _dot(act.astype(BF16), wout_ref[...])

    @pl.when(tv_ref[i] == 0)
    def _():
        o_ref[...] = jnp.zeros(o_ref.shape, F32)


def _experts(buf, tile_expert, tile_valid, w_in, w_out, tile_rows):
    n_slots, d = buf.shape
    _, _, de2 = w_in.shape
    grid_spec = pltpu.PrefetchScalarGridSpec(
        num_scalar_prefetch=2, grid=(n_slots // tile_rows,),
        in_specs=[pl.BlockSpec((tile_rows, d), lambda i, te, tv: (i, 0)),
                  pl.BlockSpec((None, d, de2), lambda i, te, tv: (te[i], 0, 0)),
                  pl.BlockSpec((None, de2 // 2, d), lambda i, te, tv: (te[i], 0, 0))],
        out_specs=pl.BlockSpec((tile_rows, d), lambda i, te, tv: (i, 0)))
    return pl.pallas_call(
        _expert_kernel, grid_spec=grid_spec, out_shape=jax.ShapeDtypeStruct((n_slots, d), F32),
        compiler_params=_params(("arbitrary",)), name="moe_experts",
    )(tile_expert, tile_valid, buf, w_in, w_out)


def _combine_kernel(dest_ref, x_ref, gate_ref, g_ref, b_ref, y_ref, o_ref, y1_sc, y2_sc, sem):
    tm = x_ref.shape[0]

    def issue(r, carry):
        _row_copy(y_ref, dest_ref[0, 0, r], y1_sc, r, sem).start()
        _row_copy(y_ref, dest_ref[0, 0, tm + r], y2_sc, r, sem).start()
        return carry

    lax.fori_loop(0, tm, issue, 0)

    def drain(r, carry):
        _row_copy(y_ref, 0, y1_sc, 0, sem).wait()
        _row_copy(y_ref, 0, y2_sc, 0, sem).wait()
        return carry

    lax.fori_loop(0, tm, drain, 0)
    gate = gate_ref[...]
    y = gate[:, 0:1] * y1_sc[...] + gate[:, 1:2] * y2_sc[...]
    o_ref[...] = _layer_norm(ALPHA * x_ref[...] + y, g_ref[...], b_ref[...])


def _combine(x, gates, dest_tiles, y, g, b):
    m_rows, d = x.shape
    tm = dest_tiles.shape[2] // 2
    return pl.pallas_call(
        _combine_kernel, grid=(m_rows // tm,),
        in_specs=[pl.BlockSpec((1, 1, 2 * tm), lambda i: (i, 0, 0), memory_space=pltpu.SMEM),
                  pl.BlockSpec((tm, d), lambda i: (i, 0)),
                  pl.BlockSpec((tm, 2), lambda i: (i, 0)),
                  _full(g.shape), _full(b.shape),
                  pl.BlockSpec(memory_space=pl.ANY)],
        out_specs=pl.BlockSpec((tm, d), lambda i: (i, 0)),
        out_shape=jax.ShapeDtypeStruct((m_rows, d), F32),
        scratch_shapes=[pltpu.VMEM((tm, d), F32), pltpu.VMEM((tm, d), F32), pltpu.SemaphoreType.DMA(())],
        compiler_params=_params(("arbitrary",)), name="moe_combine",
    )(dest_tiles, x, gates, g, b, y)


def _moe_layer(x, rw, w_in, w_out, g, b, expert_tile):
    m_rows, _ = x.shape
    tm = _token_tile(m_rows)
    route, counts = _router(x, rw["whi"], rw["wlo"], rw["br"])
    eids = route[0:2].astype(jnp.int32)
    gates = route[2:4].T
    rank = route[4:6].astype(jnp.int32)
    counts = counts[:, 0].astype(jnp.int32)
    padded = (counts + expert_tile - 1) // expert_tile * expert_tile
    pad_ends = jnp.cumsum(padded)
    pad_starts = pad_ends - padded
    dest = pad_starts[eids] + rank
    n_tiles = -(-(2 * m_rows + N_EXPERTS * (expert_tile - 1)) // expert_tile)
    tile_start = jnp.arange(n_tiles, dtype=jnp.int32) * expert_tile
    tile_expert = jnp.minimum(jnp.searchsorted(pad_ends, tile_start, side="right"), N_EXPERTS - 1).astype(jnp.int32)
    tile_valid = (tile_start < pad_ends[-1]).astype(jnp.int32)
    dest_tiles = dest.reshape(2, m_rows // tm, tm).transpose(1, 0, 2).reshape(m_rows // tm, 1, 2 * tm)
    buf = _dispatch(x, dest_tiles, n_tiles * expert_tile)
    y = _experts(buf, tile_expert, tile_valid, w_in, w_out, expert_tile)
    return _combine(x, gates, dest_tiles, y, g, b)


def _block_diag(w):
    h, n, _ = w.shape
    eye = jnp.eye(h, dtype=w.dtype)
    return (eye[:, None, :, None] * w[:, :, None, :]).reshape(h * n, h * n)


def _pair_blocks(w):
    h, a, b = w.shape
    w = w.reshape(h // 2, 2, a, b)
    eye = jnp.eye(2, dtype=w.dtype)
    return (eye[None, :, None, :, None] * w[:, :, :, None, :]).reshape(h // 2, 2 * a, 2 * b)


def _time_major(a):
    b, t, c = a.shape
    return a.transpose(1, 0, 2).reshape(t * b, c)


def _batch_major(a, nb):
    return a.reshape(-1, nb, a.shape[-1]).transpose(1, 0, 2)


def _rope_tables(pos):
    half = QK_ROPE // 2
    inv = ROPE_THETA ** (-jnp.arange(half, dtype=F32) / half)
    ang = pos.astype(F32)[:, None] * inv
    cos, sin = jnp.cos(ang), jnp.sin(ang)
    cos_t = jnp.tile(jnp.concatenate([cos, cos], axis=1), (1, N_HEADS))
    sin_t = jnp.tile(jnp.concatenate([-sin, sin], axis=1), (1, N_HEADS))
    return cos_t, sin_t


def kernel(x_prompt, x_sample, state_conv_a, state_conv_b, state_rglru_h, state_conv_c, cache_ckv, cache_kpe, page_table, w_in_even, conv_a_w, conv_b_w, conv_b_b, lru_wa, lru_ba, lru_wx, lru_bx, lru_lambda, w_out_even, w_in_odd, conv_c_w, conv_c_b, ln_c_g, ln_c_b, q_norm_g, w_q_b, kv_norm_g, w_uk, w_uv, w_out_odd, ln_mix_g, ln_mix_b, ln_ffn_g, ln_ffn_b, w_router, b_router, w_exp_in, w_exp_out):
    bp, n_tp, d = x_prompt.shape
    bs, n_ts, _ = x_sample.shape
    past_len = page_table.shape[1] * PAGE_SIZE
    dc = conv_c_w.shape[2]
    q_rank = q_norm_g.shape[1]
    kv_rank = kv_norm_g.shape[1]

    xp = _time_major(x_prompt)
    xs = _time_major(x_sample)
    row2 = lambda v: v.reshape(1, -1)

    wr = jnp.pad(w_router, ((0, 0), (0, LANES - N_EXPERTS)))
    wr_hi = wr.astype(BF16)
    router_w = {"whi": wr_hi, "wlo": (wr - wr_hi.astype(F32)).astype(BF16), "br": b_router.reshape(N_EXPERTS, 1)}
    rope_p = _rope_tables(jnp.arange(n_tp))
    rope_s = _rope_tables(past_len + jnp.arange(n_ts))

    outs = {k: [] for k in ("ca_p", "ca_s", "cb_p", "cb_s", "h_p", "h_s", "cc_p", "cc_s", "ckv_p", "ckv_s", "kpe_p", "kpe_s")}
    for l in range(DEPTH):
        j = l // 2
        lg, lb = row2(ln_mix_g[l]), row2(ln_mix_b[l])
        if l % 2 == 0:
            half = lru_wa.shape[1] * lru_wa.shape[2] * lru_wa.shape[3] // 2 // lru_wa.shape[2]
            wa_bd, wx_bd = _block_diag(lru_wa[j]), _block_diag(lru_wx[j])
            hc = wa_bd.shape[0] // 2
            wg = jnp.stack([jnp.concatenate([wa_bd[s * hc:(s + 1) * hc, s * hc:(s + 1) * hc],
                                             wx_bd[s * hc:(s + 1) * hc, s * hc:(s + 1) * hc]], axis=1)
                            for s in range(2)]).astype(BF16)
            del half
            w = {"win": w_in_even[j].astype(BF16), "caw": conv_a_w[j], "cbw": conv_b_w[j], "cbb": row2(conv_b_b[j]),
                 "wg": wg, "ba": row2(lru_ba[j]), "bx": row2(lru_bx[j]), "lam": row2(lru_lambda[j]),
                 "wout": w_out_even[j].astype(BF16), "g": lg, "b": lb}
            da = conv_a_w.shape[2]
            db = conv_b_w.shape[2]
            xp, a1, b1, h1 = _even_layer(xp, w, jnp.zeros((2 * bp, da), F32), jnp.zeros((3 * bp, db), F32),
                                         jnp.zeros((bp, db), F32), nb=bp, n_t=n_tp)
            xs, a2, b2, h2 = _even_layer(xs, w, _time_major(state_conv_a[j]), _time_major(state_conv_b[j]),
                                         state_rglru_h[j], nb=bs, n_t=n_ts)
            outs["ca_p"].append(_batch_major(a1, bp)); outs["ca_s"].append(_batch_major(a2, bs))
            outs["cb_p"].append(_batch_major(b1, bp)); outs["cb_s"].append(_batch_major(b2, bs))
            outs["h_p"].append(h1); outs["h_s"].append(h2)
        else:
            nope_all = N_HEADS * QK_NOPE
            wq = w_q_b[j].reshape(q_rank, N_HEADS, QK_NOPE + QK_ROPE)
            wqb = jnp.concatenate([wq[:, :, :QK_NOPE].reshape(q_rank, nope_all),
                                   wq[:, :, QK_NOPE:].reshape(q_rank, N_HEADS * QK_ROPE)], axis=1).astype(BF16)
            o_pe = 2 * dc + q_rank + kv_rank
            win = jnp.concatenate([w_in_odd[j][:, :o_pe], jnp.tile(w_in_odd[j][:, o_pe:], (1, N_HEADS))], axis=1).astype(BF16)
            wuk = _pair_blocks(w_uk[j].transpose(1, 2, 0)).astype(BF16)
            wuv = _pair_blocks(w_uv[j].transpose(1, 0, 2)).astype(BF16)
            w = {"win": win, "ccw": conv_c_w[j], "ccb": row2(conv_c_b[j]), "lcg": row2(ln_c_g[j]), "lcb": row2(ln_c_b[j]),
                 "qg": row2(q_norm_g[j]), "wqb": wqb, "kvg": row2(kv_norm_g[j]), "wuk": wuk}
            width = conv_c_w.shape[1]
            wout = w_out_odd[j].astype(BF16)
            ycp, qp, kp, ckv1, kpe1, c1 = _odd_pre(xp, w, jnp.zeros(((width - 1) * bp, dc), F32), *rope_p, nb=bp, n_t=n_tp)
            ycs, qs, ks, ckv2, kpe2, c2 = _odd_pre(xs, w, _time_major(state_conv_c[j]), *rope_s, nb=bs, n_t=n_ts)
            ydp = _prompt_attention(qp, kp, wuv, nb=bp, n_t=n_tp)
            yds = _sample_attention(qs, ks, wuv, cache_ckv, cache_kpe, page_table, j, nb=bs, n_t=n_ts)
            xp = _odd_post(xp, ycp, ydp, wout, lg, lb)
            xs = _odd_post(xs, ycs, yds, wout, lg, lb)
            outs["cc_p"].append(_batch_major(c1, bp)); outs["cc_s"].append(_batch_major(c2, bs))
            outs["ckv_p"].append(_batch_major(ckv1, bp)); outs["ckv_s"].append(_batch_major(ckv2, bs))
            outs["kpe_p"].append(_batch_major(kpe1, bp)); outs["kpe_s"].append(_batch_major(kpe2, bs))
        we_in, we_out = w_exp_in[l].astype(BF16), w_exp_out[l].astype(BF16)
        fg, fb = row2(ln_ffn_g[l]), row2(ln_ffn_b[l])
        xp = _moe_layer(xp, router_w, we_in, we_out, fg, fb, expert_tile=256)
        xs = _moe_layer(xs, router_w, we_in, we_out, fg, fb, expert_tile=64)

    y_prompt = _batch_major(xp, bp)
    y_sample = _batch_major(xs, bs)
    st = lambda k: jnp.stack(outs[k])
    st1 = lambda k: jnp.stack(outs[k], axis=1)
    return (y_prompt, y_sample, st("ca_p"), st("ca_s"), st("cb_p"), st("cb_s"), st("h_p"), st("h_s"),
            st("cc_p"), st("cc_s"), st1("ckv_p"), st1("ckv_s"), st1("kpe_p"), st1("kpe_s"))
```

```python
import functools

import jax
import jax.numpy as jnp
from jax import lax
from jax.experimental import pallas as pl
from jax.experimental.pallas import tpu as pltpu

F32 = jnp.float32
BF16 = jnp.bfloat16

DEPTH = 4
N_HEADS = 8
QK_NOPE = 64
QK_ROPE = 32
V_DIM = 64
N_EXPERTS = 16
N_GROUPS = 4
EXPERTS_PER_GROUP = 4
LRU_C = 8.0
ROPE_THETA = 10000.0
ATTN_SCALE = (QK_NOPE + QK_ROPE) ** -0.5
ALPHA = (2 * DEPTH) ** 0.25
PAGE_SIZE = 128

SUBLANES = 8
LANES = 128
VMEM_LIMIT_BYTES = 56 * 1024 * 1024

TIME_STEPS_PER_TILE = 32
TOKEN_TILE = 512
ATTN_TILE = 512
PAGES_PER_STEP = 64
ROW_DMA_UNROLL = 8
NEG_INF = float("-inf")


def _params(semantics):
    return pltpu.CompilerParams(dimension_semantics=semantics, vmem_limit_bytes=VMEM_LIMIT_BYTES)


def _full(shape):
    nd = len(shape)
    return pl.BlockSpec(shape, lambda *_: (0,) * nd)


def _layer_norm(x, g, b, eps=1e-5):
    mu = jnp.mean(x, axis=-1, keepdims=True)
    xc = x - mu
    var = jnp.mean(xc * xc, axis=-1, keepdims=True)
    return xc * lax.rsqrt(var + eps) * g + b


def _rms_norm(x, g, eps=1e-6):
    return x * lax.rsqrt(jnp.mean(x * x, axis=-1, keepdims=True) + eps) * g


def _sigmoid(x):
    return 1.0 / (1.0 + jnp.exp(-x))


def _dot(a, b):
    return jnp.dot(a, b, preferred_element_type=F32)


def _dot_nt(a, b):
    return lax.dot_general(a, b, (((1,), (1,)), ((), ())), preferred_element_type=F32)


def _seq_tile(nb, n_t):
    tt = min(n_t, TIME_STEPS_PER_TILE)
    assert n_t % tt == 0 and nb % SUBLANES == 0
    return tt


def _even_kernel(x_ref, win_ref, caw_ref, cbw_ref, cbb_ref, wg_ref, ba_ref, bx_ref, lam_ref, wout_ref,
                 g_ref, b_ref, sa_ref, sb_ref, h0_ref,
                 o_ref, na_ref, nb_ref, hl_ref,
                 ua_ext, vb_ext, h_sc, hs_sc, *, nb, tt):
    tm = tt * nb
    dh = caw_ref.shape[1]
    half = dh // 2

    @pl.when(pl.program_id(0) == 0)
    def _():
        ua_ext[0:2 * nb, :] = sa_ref[...]
        vb_ext[0:3 * nb, :] = sb_ref[...]
        h_sc[...] = h0_ref[...]

    x = x_ref[...]
    xb = x.astype(BF16)

    def proj(j):
        return _dot(xb, win_ref[:, j * dh:(j + 1) * dh])

    ua_ext[2 * nb:2 * nb + tm, :] = proj(1) * proj(2)
    caw = caw_ref[...]
    conv_a = (caw[0:1] * ua_ext[0:tm, :] + caw[1:2] * ua_ext[nb:nb + tm, :]
              + caw[2:3] * ua_ext[2 * nb:2 * nb + tm, :])
    y_a = proj(0) * conv_a
    tail_a = ua_ext[tm:tm + 2 * nb, :]
    na_ref[...] = tail_a
    ua_ext[0:2 * nb, :] = tail_a

    vb_ext[3 * nb:3 * nb + tm, :] = proj(4)
    cbw = cbw_ref[...]
    u_b = (cbb_ref[...] + cbw[0:1] * vb_ext[0:tm, :] + cbw[1:2] * vb_ext[nb:nb + tm, :]
           + cbw[2:3] * vb_ext[2 * nb:2 * nb + tm, :] + cbw[3:4] * vb_ext[3 * nb:3 * nb + tm, :])
    tail_b = vb_ext[tm:tm + 3 * nb, :]
    nb_ref[...] = tail_b
    vb_ext[0:3 * nb, :] = tail_b

    ub16 = u_b.astype(BF16)
    gk0 = _dot(ub16[:, :half], wg_ref[0])
    gk1 = _dot(ub16[:, half:], wg_ref[1])
    r = _sigmoid(jnp.concatenate([gk0[:, :half], gk1[:, :half]], axis=1) + ba_ref[...])
    ig = _sigmoid(jnp.concatenate([gk0[:, half:], gk1[:, half:]], axis=1) + bx_ref[...])
    nlam = -lam_ref[...]
    softplus = jnp.maximum(nlam, 0.0) + jnp.log(1.0 + jnp.exp(-jnp.abs(nlam)))
    log_a = (-LRU_C * r) * softplus
    a = jnp.exp(log_a)
    bterm = jnp.sqrt(1.0 - a * a) * (ig * u_b)

    h = h_sc[...]
    for t in range(tt):
        h = a[t * nb:(t + 1) * nb, :] * h + bterm[t * nb:(t + 1) * nb, :]
        hs_sc[t * nb:(t + 1) * nb, :] = h
    h_sc[...] = h
    hl_ref[...] = h
    y_b = jax.nn.gelu(proj(3), approximate=True) * hs_sc[...]

    m = _dot(y_a.astype(BF16), wout_ref[0:dh, :]) + _dot(y_b.astype(BF16), wout_ref[dh:2 * dh, :])
    o_ref[...] = _layer_norm(ALPHA * x + m, g_ref[...], b_ref[...])


def _even_layer(x, w, sa, sb, h0, *, nb, n_t):
    m_rows, d = x.shape
    dh = w["caw"].shape[1]
    tt = _seq_tile(nb, n_t)
    tm = tt * nb
    kern = functools.partial(_even_kernel, nb=nb, tt=tt)
    row = lambda i: (i, 0)
    in_specs = [pl.BlockSpec((tm, d), row), _full(w["win"].shape), _full(w["caw"].shape), _full(w["cbw"].shape),
                _full(w["cbb"].shape), _full(w["wg"].shape), _full(w["ba"].shape), _full(w["bx"].shape),
                _full(w["lam"].shape), _full(w["wout"].shape), _full(w["g"].shape), _full(w["b"].shape),
                _full(sa.shape), _full(sb.shape), _full(h0.shape)]
    out_shape = (jax.ShapeDtypeStruct((m_rows, d), F32), jax.ShapeDtypeStruct(sa.shape, F32),
                 jax.ShapeDtypeStruct(sb.shape, F32), jax.ShapeDtypeStruct(h0.shape, F32))
    out_specs = (pl.BlockSpec((tm, d), row), _full(sa.shape), _full(sb.shape), _full(h0.shape))
    scratch = [pltpu.VMEM((tm + 2 * nb, dh), F32), pltpu.VMEM((tm + 3 * nb, dh), F32),
               pltpu.VMEM((nb, dh), F32), pltpu.VMEM((tm, dh), F32)]
    return pl.pallas_call(
        kern, grid=(m_rows // tm,), in_specs=in_specs, out_specs=out_specs, out_shape=out_shape,
        scratch_shapes=scratch, compiler_params=_params(("arbitrary",)), name="even_mixer",
    )(x, w["win"], w["caw"], w["cbw"], w["cbb"], w["wg"], w["ba"], w["bx"], w["lam"], w["wout"], w["g"], w["b"],
      sa, sb, h0)


def _odd_pre_kernel(x_ref, win_ref, ccw_ref, ccb_ref, lcg_ref, lcb_ref, qg_ref, wqb_ref, kvg_ref, wuk_ref,
                    cos_ref, sin_ref, sc_ref,
                    yc_ref, q_ref, k_ref, ckv_ref, kpe_ref, nc_ref,
                    c_ext, rl_sc, *, nb, tt, width):
    tm = tt * nb
    dc = ccw_ref.shape[1]
    q_rank = qg_ref.shape[1]
    kv_rank = kvg_ref.shape[1]
    pe_all = N_HEADS * QK_ROPE
    hist = (width - 1) * nb

    @pl.when(pl.program_id(0) == 0)
    def _():
        c_ext[0:hist, :] = sc_ref[...]

    xb = x_ref[...].astype(BF16)
    o_q = 2 * dc
    o_kv = o_q + q_rank
    o_pe = o_kv + kv_rank

    glu = _dot(xb, win_ref[:, 0:dc]) * _sigmoid(_dot(xb, win_ref[:, dc:2 * dc]))
    c_ext[hist:hist + tm, :] = glu
    ccw = ccw_ref[...]
    u_c = ccb_ref[...] + ccw[0:1] * c_ext[0:tm, :]
    for k in range(1, width):
        u_c = u_c + ccw[k:k + 1] * c_ext[k * nb:k * nb + tm, :]
    tail = c_ext[tm:tm + hist, :]
    nc_ref[...] = tail
    c_ext[0:hist, :] = tail
    ln = _layer_norm(u_c, lcg_ref[...], lcb_ref[...])
    yc_ref[...] = ln * _sigmoid(ln)

    def to_sequences(val, store):
        n_lane_tiles = val.shape[1] // LANES
        for c in range(n_lane_tiles):
            rl_sc[c] = val[:, c * LANES:(c + 1) * LANES]

        def body(b, carry):
            store(b, jnp.concatenate([rl_sc[c, pl.ds(b, tt, stride=nb), :] for c in range(n_lane_tiles)], axis=1))
            return carry

        lax.fori_loop(0, nb, body, 0)

    cos = jnp.broadcast_to(cos_ref[...][:, None, :], (tt, nb, pe_all)).reshape(tm, pe_all)
    sin = jnp.broadcast_to(sin_ref[...][:, None, :], (tt, nb, pe_all)).reshape(tm, pe_all)
    lane = lax.broadcasted_iota(jnp.int32, (1, pe_all), 1)
    first_half = (lane % QK_ROPE) < (QK_ROPE // 2)
    head_of_lane = lane // QK_ROPE

    def rope(v):
        swapped = jnp.where(first_half, pltpu.roll(v, pe_all - QK_ROPE // 2, 1), pltpu.roll(v, QK_ROPE // 2, 1))
        return v * cos + swapped * sin

    ckv = _rms_norm(_dot(xb, win_ref[:, o_kv:o_kv + kv_rank]), kvg_ref[...])
    kpe = rope(_dot(xb, win_ref[:, o_pe:o_pe + pe_all]))

    def store_k(b, v):
        k_ref[b] = v.astype(BF16)
        ckv_ref[b] = v[:, 0:kv_rank]
        kpe_ref[b] = v[:, kv_rank:kv_rank + QK_ROPE]

    to_sequences(jnp.concatenate([ckv, kpe], axis=1), store_k)

    qn = _rms_norm(_dot(xb, win_ref[:, o_q:o_q + q_rank]), qg_ref[...]).astype(BF16)
    nope_all = N_HEADS * QK_NOPE
    q_nope = (_dot(qn, wqb_ref[:, 0:nope_all]) * ATTN_SCALE).astype(BF16)
    q_pe = rope(_dot(qn, wqb_ref[:, nope_all:nope_all + pe_all]) * ATTN_SCALE)
    for p in range(N_HEADS // 2):
        lat2 = _dot(q_nope[:, p * 2 * QK_NOPE:(p + 1) * 2 * QK_NOPE], wuk_ref[p])
        for s in range(2):
            h = 2 * p + s

            def store_q(b, v, h=h):
                q_ref[b, h] = v.astype(BF16)

            to_sequences(jnp.concatenate([lat2[:, s * kv_rank:(s + 1) * kv_rank],
                                          jnp.where(head_of_lane == h, q_pe, 0.0)], axis=1), store_q)


def _odd_pre(x, w, sc, cos, sin, *, nb, n_t):
    m_rows, d = x.shape
    width, dc = w["ccw"].shape
    kv_rank = w["kvg"].shape[1]
    pe_all = N_HEADS * QK_ROPE
    kq = kv_rank + pe_all
    tt = _seq_tile(nb, n_t)
    tm = tt * nb
    kern = functools.partial(_odd_pre_kernel, nb=nb, tt=tt, width=width)
    row = lambda i: (i, 0)
    seq = lambda i: (0, i, 0)
    in_specs = [pl.BlockSpec((tm, d), row)] + [_full(w[k].shape) for k in
                                                ("win", "ccw", "ccb", "lcg", "lcb", "qg", "wqb", "kvg", "wuk")]
    in_specs += [pl.BlockSpec((tt, pe_all), row), pl.BlockSpec((tt, pe_all), row), _full(sc.shape)]
    out_shape = (jax.ShapeDtypeStruct((m_rows, dc), F32),
                 jax.ShapeDtypeStruct((nb, N_HEADS, n_t, kq), BF16),
                 jax.ShapeDtypeStruct((nb, n_t, kq), BF16),
                 jax.ShapeDtypeStruct((nb, n_t, kv_rank), F32),
                 jax.ShapeDtypeStruct((nb, n_t, QK_ROPE), F32),
                 jax.ShapeDtypeStruct(sc.shape, F32))
    out_specs = (pl.BlockSpec((tm, dc), row), pl.BlockSpec((nb, N_HEADS, tt, kq), lambda i: (0, 0, i, 0)),
                 pl.BlockSpec((nb, tt, kq), seq), pl.BlockSpec((nb, tt, kv_rank), seq),
                 pl.BlockSpec((nb, tt, QK_ROPE), seq), _full(sc.shape))
    scratch = [pltpu.VMEM((tm + (width - 1) * nb, dc), F32), pltpu.VMEM((kq // LANES, tm, LANES), F32)]
    return pl.pallas_call(
        kern, grid=(m_rows // tm,), in_specs=in_specs, out_specs=out_specs, out_shape=out_shape,
        scratch_shapes=scratch, compiler_params=_params(("arbitrary",)), name="odd_pre",
    )(x, w["win"], w["ccw"], w["ccb"], w["lcg"], w["lcb"], w["qg"], w["wqb"], w["kvg"], w["wuk"], cos, sin, sc)


def _softmax_step(s, v16, m_sc, l_sc, acc_sc):
    m_prev = m_sc[...]
    m_new = jnp.maximum(m_prev, jnp.max(s, axis=-1, keepdims=True))
    alpha = jnp.exp(m_prev - m_new)
    p = jnp.exp(s - m_new)
    l_sc[...] = alpha * l_sc[...] + jnp.sum(p, axis=-1, keepdims=True)
    acc_sc[...] = alpha * acc_sc[...] + _dot(p.astype(BF16), v16)
    m_sc[...] = m_new


def _softmax_init(m_sc, l_sc, acc_sc):
    m_sc[...] = jnp.full(m_sc.shape, NEG_INF, F32)
    l_sc[...] = jnp.zeros(l_sc.shape, F32)
    acc_sc[...] = jnp.zeros(acc_sc.shape, F32)


def _prompt_attn_kernel(qi_ref, ki_ref, q_ref, k_ref, bias_ref, wuv_ref, o_ref, *state, kv_rank):
    m_sc, l_sc, acc_sc = state[0:N_HEADS], state[N_HEADS:2 * N_HEADS], state[2 * N_HEADS:3 * N_HEADS]
    j = pl.program_id(1)
    qi = qi_ref[j]
    ki = ki_ref[j]

    @pl.when(ki == 0)
    def _():
        for h in range(N_HEADS):
            _softmax_init(m_sc[h], l_sc[h], acc_sc[h])

    k = k_ref[...]
    v16 = k[:, 0:kv_rank]
    diag = (ki == qi).astype(jnp.int32)

    def scores(h):
        return _dot_nt(q_ref[h], k) + bias_ref[diag]

    s_next = scores(0)
    for h in range(N_HEADS):
        s = s_next
        if h + 1 < N_HEADS:
            s_next = scores(h + 1)
        _softmax_step(s, v16, m_sc[h], l_sc[h], acc_sc[h])

    @pl.when(ki == qi)
    def _():
        for p in range(N_HEADS // 2):
            o2 = jnp.concatenate([acc_sc[2 * p][...] / l_sc[2 * p][...],
                                  acc_sc[2 * p + 1][...] / l_sc[2 * p + 1][...]], axis=1)
            o_ref[:, p * 2 * V_DIM:(p + 1) * 2 * V_DIM] = _dot(o2.astype(BF16), wuv_ref[p])


def _prompt_attention(q, k, wuv):
    nb, n_heads, n_t, kq = q.shape
    kv_rank = wuv.shape[1] // 2
    dv = N_HEADS * V_DIM
    tile = min(n_t, ATTN_TILE)
    assert n_t % tile == 0
    nq = n_t // tile
    pairs = [(a, b) for a in range(nq) for b in range(a + 1)]
    qi_tab = jnp.asarray([p[0] for p in pairs], jnp.int32)
    ki_tab = jnp.asarray([p[1] for p in pairs], jnp.int32)
    causal = jnp.where(jnp.arange(tile)[None, :] <= jnp.arange(tile)[:, None], 0.0, NEG_INF).astype(F32)
    bias = jnp.stack([jnp.zeros((tile, tile), F32), causal])
    kern = functools.partial(_prompt_attn_kernel, kv_rank=kv_rank)
    grid_spec = pltpu.PrefetchScalarGridSpec(
        num_scalar_prefetch=2, grid=(nb, len(pairs)),
        in_specs=[pl.BlockSpec((None, n_heads, tile, kq), lambda b, j, qi, ki: (b, 0, qi[j], 0)),
                  pl.BlockSpec((None, tile, kq), lambda b, j, qi, ki: (b, ki[j], 0)),
                  pl.BlockSpec(bias.shape, lambda b, j, qi, ki: (0, 0, 0)),
                  pl.BlockSpec(wuv.shape, lambda b, j, qi, ki: (0, 0, 0))],
        out_specs=pl.BlockSpec((None, tile, dv), lambda b, j, qi, ki: (b, qi[j], 0)),
        scratch_shapes=([pltpu.VMEM((tile, 1), F32)] * (2 * n_heads) + [pltpu.VMEM((tile, kv_rank), F32)] * n_heads))
    return pl.pallas_call(
        kern, grid_spec=grid_spec, out_shape=jax.ShapeDtypeStruct((nb, n_t, dv), F32),
        compiler_params=_params(("arbitrary", "arbitrary")), name="prompt_attention",
    )(qi_tab, ki_tab, q, k, bias, wuv)


def _sample_attn_kernel(pt_ref, q_ref, kn_ref, tsel_ref, wuv_ref, ckv_hbm, kpe_hbm, o_ref,
                        ck_buf, kp_buf, sem, qc_sc, m_sc, l_sc, acc_sc, *,
                        n_pages_step, layer, kv_rank, n_new, rows_per_head):
    b = pl.program_id(0)
    g = pl.program_id(1)
    n_g = pl.num_programs(1)
    step = b * n_g + g
    n_steps = pl.num_programs(0) * n_g
    slot = step % 2

    def page_copies(seq, grp, sl):
        copies = []
        for i in range(n_pages_step):
            page = pt_ref[seq, grp * n_pages_step + i]
            copies.append(pltpu.make_async_copy(
                ckv_hbm.at[page, layer], ck_buf.at[sl, pl.ds(i * PAGE_SIZE, PAGE_SIZE), :], sem.at[sl]))
            copies.append(pltpu.make_async_copy(
                kpe_hbm.at[page, layer], kp_buf.at[sl, :, pl.ds(i * PAGE_SIZE, PAGE_SIZE)], sem.at[sl]))
        return copies

    @pl.when(step == 0)
    def _():
        for c in page_copies(0, 0, 0):
            c.start()

    @pl.when(step + 1 < n_steps)
    def _():
        nxt = step + 1
        for c in page_copies(nxt // n_g, nxt % n_g, 1 - slot):
            c.start()

    @pl.when(g == 0)
    def _():
        _softmax_init(m_sc, l_sc, acc_sc)
        qc_sc[...] = _dot(q_ref[:, kv_rank:], tsel_ref[...]).astype(BF16)

    for c in page_copies(b, g, slot):
        c.wait()

    q = q_ref[...]
    ck = ck_buf[slot].astype(BF16)
    kp_t = kp_buf[slot].astype(BF16)
    s = _dot_nt(q[:, 0:kv_rank], ck) + _dot(qc_sc[...], kp_t)
    _softmax_step(s, ck, m_sc, l_sc, acc_sc)

    @pl.when(g == pl.num_programs(1) - 1)
    def _():
        kn = kn_ref[...]
        s_new = _dot_nt(q, kn)
        t_row = lax.broadcasted_iota(jnp.int32, s_new.shape, 0) % rows_per_head
        t_col = lax.broadcasted_iota(jnp.int32, s_new.shape, 1)
        s_new = jnp.where((t_col <= t_row) & (t_col < n_new), s_new, NEG_INF)
        _softmax_step(s_new, kn[:, 0:kv_rank], m_sc, l_sc, acc_sc)
        o = acc_sc[...] / l_sc[...]
        rp = rows_per_head
        for p in range(N_HEADS // 2):
            o2 = jnp.concatenate([o[2 * p * rp:(2 * p + 1) * rp], o[(2 * p + 1) * rp:(2 * p + 2) * rp]], axis=1)
            o_ref[:, p * 2 * V_DIM:(p + 1) * 2 * V_DIM] = _dot(o2.astype(BF16), wuv_ref[p])


def _sample_attention(q, k, wuv, cache_ckv, cache_kpe_t, page_table, layer):
    nb, n_heads, n_t, kq = q.shape
    kv_rank = wuv.shape[1] // 2
    dv = N_HEADS * V_DIM
    n_pages = page_table.shape[1]
    gp = min(PAGES_PER_STEP, n_pages)
    assert n_pages % gp == 0 and n_t <= SUBLANES
    rp = SUBLANES
    qb = jnp.pad(q, ((0, 0), (0, 0), (0, rp - n_t), (0, 0))).reshape(nb, n_heads * rp, kq)
    kb = jnp.pad(k, ((0, 0), (0, rp - n_t), (0, 0)))
    tsel = jnp.tile(jnp.eye(QK_ROPE, dtype=BF16), (N_HEADS, 1))
    kern = functools.partial(_sample_attn_kernel, n_pages_step=gp, layer=layer, kv_rank=kv_rank, n_new=n_t,
                             rows_per_head=rp)
    in_specs = [pl.BlockSpec((None, n_heads * rp, kq), lambda b, g, pt: (b, 0, 0)),
                pl.BlockSpec((None, rp, kq), lambda b, g, pt: (b, 0, 0)),
                pl.BlockSpec(tsel.shape, lambda b, g, pt: (0, 0)),
                pl.BlockSpec(wuv.shape, lambda b, g, pt: (0, 0, 0)),
                pl.BlockSpec(memory_space=pl.ANY), pl.BlockSpec(memory_space=pl.ANY)]
    grid_spec = pltpu.PrefetchScalarGridSpec(
        num_scalar_prefetch=1, grid=(nb, n_pages // gp), in_specs=in_specs,
        out_specs=pl.BlockSpec((None, rp, dv), lambda b, g, pt: (b, 0, 0)),
        scratch_shapes=[pltpu.VMEM((2, gp * PAGE_SIZE, kv_rank), F32), pltpu.VMEM((2, QK_ROPE, gp * PAGE_SIZE), F32),
                        pltpu.SemaphoreType.DMA((2,)),
                        pltpu.VMEM((n_heads * rp, QK_ROPE), BF16), pltpu.VMEM((n_heads * rp, 1), F32),
                        pltpu.VMEM((n_heads * rp, 1), F32), pltpu.VMEM((n_heads * rp, kv_rank), F32)])
    out = pl.pallas_call(
        kern, grid_spec=grid_spec, out_shape=jax.ShapeDtypeStruct((nb, rp, dv), F32),
        compiler_params=_params(("arbitrary", "arbitrary")), name="sample_attention",
    )(page_table, qb, kb, tsel, wuv, cache_ckv, cache_kpe_t)
    return out[:, :n_t]


def _odd_post_kernel(x_ref, yc_ref, yd_ref, wout_ref, g_ref, b_ref, o_ref, rl_sc, *, nb, tt):
    dc = yc_ref.shape[1]

    n_lane_tiles = rl_sc.shape[0]

    def body(b, carry):
        v = yd_ref[b]
        for c in range(n_lane_tiles):
            rl_sc[c, pl.ds(b, tt, stride=nb), :] = v[:, c * LANES:(c + 1) * LANES]
        return carry

    lax.fori_loop(0, nb, body, 0)
    yd = jnp.concatenate([rl_sc[c] for c in range(n_lane_tiles)], axis=1)
    m = _dot(yc_ref[...].astype(BF16), wout_ref[0:dc, :]) + _dot(yd.astype(BF16), wout_ref[dc:, :])
    o_ref[...] = _layer_norm(ALPHA * x_ref[...] + m, g_ref[...], b_ref[...])


def _odd_post(x, yc, yd, wout, g, b, *, nb, n_t):
    m_rows, d = x.shape
    dv = yd.shape[2]
    tt = _seq_tile(nb, n_t)
    tm = tt * nb
    row = lambda i: (i, 0)
    kern = functools.partial(_odd_post_kernel, nb=nb, tt=tt)
    return pl.pallas_call(
        kern, grid=(m_rows // tm,),
        in_specs=[pl.BlockSpec((tm, d), row), pl.BlockSpec((tm, yc.shape[1]), row),
                  pl.BlockSpec((nb, tt, dv), lambda i: (0, i, 0)),
                  _full(wout.shape), _full(g.shape), _full(b.shape)],
        out_specs=pl.BlockSpec((tm, d), row), out_shape=jax.ShapeDtypeStruct((m_rows, d), F32),
        scratch_shapes=[pltpu.VMEM((dv // LANES, tm, LANES), F32)],
        compiler_params=_params(("arbitrary",)), name="odd_post",
    )(x, yc, yd, wout, g, b)


def _token_tile(m_rows):
    tm = min(m_rows, TOKEN_TILE)
    assert m_rows % tm == 0
    return tm


def _router_kernel(x_ref, whi_ref, wlo_ref, br_ref, tri_ref, o_ref, cnt_ref, base_sc):
    tm = x_ref.shape[0]

    @pl.when(pl.program_id(0) == 0)
    def _():
        base_sc[...] = jnp.zeros(base_sc.shape, F32)

    x = x_ref[...]
    x_hi = x.astype(BF16)
    x_lo = (x - x_hi.astype(F32)).astype(BF16)
    logits = _dot(x_hi, whi_ref[...]) + (_dot(x_lo, whi_ref[...]) + _dot(x_hi, wlo_ref[...]))
    scores = _sigmoid(logits.T[0:N_EXPERTS, :])
    sel = scores + br_ref[...]

    def row(a, e):
        return a[e:e + 1, :]

    best_gs = None
    grp = None
    for gidx in range(N_GROUPS):
        v = [row(sel, gidx * EXPERTS_PER_GROUP + i) for i in range(EXPERTS_PER_GROUP)]
        gs = None
        for i in range(EXPERTS_PER_GROUP):
            for j in range(i + 1, EXPERTS_PER_GROUP):
                pair = v[i] + v[j]
                gs = pair if gs is None else jnp.maximum(gs, pair)
        if best_gs is None:
            best_gs, grp = gs, jnp.zeros(gs.shape, jnp.int32)
        else:
            better = gs > best_gs
            grp = jnp.where(better, gidx, grp)
            best_gs = jnp.where(better, gs, best_gs)

    def pick(a, i):
        out = row(a, i)
        for gidx in range(1, N_GROUPS):
            out = jnp.where(grp == gidx, row(a, gidx * EXPERTS_PER_GROUP + i), out)
        return out

    cand = [pick(sel, i) for i in range(EXPERTS_PER_GROUP)]
    cand_score = [pick(scores, i) for i in range(EXPERTS_PER_GROUP)]

    def argmax_first(vals, exclude=None):
        best, idx = None, None
        for i, v in enumerate(vals):
            if exclude is not None:
                v = jnp.where(exclude == i, NEG_INF, v)
            if best is None:
                best, idx = v, jnp.zeros(v.shape, jnp.int32)
            else:
                better = v > best
                idx = jnp.where(better, i, idx)
                best = jnp.where(better, v, best)
        return idx

    loc1 = argmax_first(cand)
    loc2 = argmax_first(cand, exclude=loc1)

    def take(vals, idx):
        out = vals[0]
        for i in range(1, len(vals)):
            out = jnp.where(idx == i, vals[i], out)
        return out

    g1 = take(cand_score, loc1)
    g2 = take(cand_score, loc2)
    gsum = g1 + g2
    e1 = grp * EXPERTS_PER_GROUP + loc1
    e2 = grp * EXPERTS_PER_GROUP + loc2

    e_iota = lax.broadcasted_iota(jnp.int32, (N_EXPERTS, tm), 0)
    hit1 = e_iota == e1
    hit2 = e_iota == e2
    both = jnp.where(hit1 | hit2, 1.0, 0.0)
    before = _dot(both.astype(BF16), tri_ref[...]) + base_sc[...]
    r1 = jnp.sum(jnp.where(hit1, before, 0.0), axis=0, keepdims=True)
    r2 = jnp.sum(jnp.where(hit2, before, 0.0), axis=0, keepdims=True)
    base_sc[...] = base_sc[...] + jnp.sum(both, axis=1, keepdims=True)
    cnt_ref[...] = jnp.broadcast_to(base_sc[...], cnt_ref.shape)

    zero = jnp.zeros((1, tm), F32)
    o_ref[...] = jnp.concatenate([e1.astype(F32), e2.astype(F32), g1 / gsum, g2 / gsum, r1, r2, zero, zero], axis=0)


def _router(x, whi, wlo, br):
    m_rows, d = x.shape
    tm = _token_tile(m_rows)
    tri = jnp.triu(jnp.ones((tm, tm), BF16), k=1)
    return pl.pallas_call(
        _router_kernel, grid=(m_rows // tm,),
        in_specs=[pl.BlockSpec((tm, d), lambda i: (i, 0)), _full(whi.shape), _full(wlo.shape), _full(br.shape),
                  _full(tri.shape)],
        out_specs=(pl.BlockSpec((SUBLANES, tm), lambda i: (0, i)), _full((N_EXPERTS, LANES))),
        out_shape=(jax.ShapeDtypeStruct((SUBLANES, m_rows), F32), jax.ShapeDtypeStruct((N_EXPERTS, LANES), F32)),
        scratch_shapes=[pltpu.VMEM((N_EXPERTS, 1), F32)],
        compiler_params=_params(("arbitrary",)), name="router",
    )(x, whi, wlo, br, tri)


def _row_copy(src_ref, src_row, dst_ref, dst_row, sem):
    return pltpu.make_async_copy(src_ref.at[pl.ds(src_row, 1), :], dst_ref.at[pl.ds(dst_row, 1), :], sem)


def _dispatch_kernel(dest_ref, x_ref, buf_in_ref, buf_ref, sem):
    del buf_in_ref
    tm = x_ref.shape[0]

    def issue(i, carry):
        for u in range(ROW_DMA_UNROLL):
            r = i * ROW_DMA_UNROLL + u
            _row_copy(x_ref, r, buf_ref, dest_ref[0, 0, r], sem).start(priority=u % 2)
            _row_copy(x_ref, r, buf_ref, dest_ref[0, 0, tm + r], sem).start(priority=(u + 1) % 2)
        return carry

    lax.fori_loop(0, tm // ROW_DMA_UNROLL, issue, 0)
    for _ in range(2):
        pltpu.make_async_copy(x_ref, buf_ref.at[pl.ds(0, tm), :], sem).wait()


def _dispatch(x, dest_tiles, n_slots):
    m_rows, d = x.shape
    tm = dest_tiles.shape[2] // 2
    assert tm % ROW_DMA_UNROLL == 0
    buf0 = jnp.zeros((n_slots, d), F32)
    return pl.pallas_call(
        _dispatch_kernel, grid=(m_rows // tm,),
        in_specs=[pl.BlockSpec((1, 1, 2 * tm), lambda i: (i, 0, 0), memory_space=pltpu.SMEM),
                  pl.BlockSpec((tm, d), lambda i: (i, 0)),
                  pl.BlockSpec(memory_space=pl.ANY)],
        out_specs=pl.BlockSpec(memory_space=pl.ANY),
        out_shape=jax.ShapeDtypeStruct((n_slots, d), F32),
        scratch_shapes=[pltpu.SemaphoreType.DMA(())],
        input_output_aliases={2: 0},
        compiler_params=_params(("arbitrary",)), name="moe_dispatch",
    )(dest_tiles, x, buf0)


def _expert_kernel(te_ref, tv_ref, x_ref, win_ref, wout_ref, o_ref):
    i = pl.program_id(0)
    de = wout_ref.shape[0]

    @pl.when(tv_ref[i] != 0)
    def _():
        h = _dot(x_ref[...].astype(BF16), win_ref[...])
        hg = h[:, 0:de]
        act = hg * _sigmoid(hg) * h[:, de:2 * de]
        o_ref[...] = _dot(act.astype(BF16), wout_ref[...])

    @pl.when(tv_ref[i] == 0)
    def _():
        o_ref[...] = jnp.zeros(o_ref.shape, F32)


def _experts(buf, tile_expert, tile_valid, w_in, w_out, layer, tile_rows):
    n_slots, d = buf.shape
    de2 = w_in.shape[3]
    grid_spec = pltpu.PrefetchScalarGridSpec(
        num_scalar_prefetch=2, grid=(n_slots // tile_rows,),
        in_specs=[pl.BlockSpec((tile_rows, d), lambda i, te, tv: (i, 0)),
                  pl.BlockSpec((None, None, d, de2), lambda i, te, tv: (layer, te[i], 0, 0)),
                  pl.BlockSpec((None, None, de2 // 2, d), lambda i, te, tv: (layer, te[i], 0, 0))],
        out_specs=pl.BlockSpec((tile_rows, d), lambda i, te, tv: (i, 0)))
    return pl.pallas_call(
        _expert_kernel, grid_spec=grid_spec, out_shape=jax.ShapeDtypeStruct((n_slots, d), F32),
        compiler_params=_params(("arbitrary",)), name="moe_experts",
    )(tile_expert, tile_valid, buf, w_in, w_out)


def _combine_kernel(dest_ref, x_ref, gate_ref, g_ref, b_ref, y_ref, o_ref, y1_sc, y2_sc, sem):
    tm = x_ref.shape[0]

    def issue(i, carry):
        for u in range(ROW_DMA_UNROLL):
            r = i * ROW_DMA_UNROLL + u
            _row_copy(y_ref, dest_ref[0, 0, r], y1_sc, r, sem).start(priority=u % 2)
            _row_copy(y_ref, dest_ref[0, 0, tm + r], y2_sc, r, sem).start(priority=(u + 1) % 2)
        return carry

    lax.fori_loop(0, tm // ROW_DMA_UNROLL, issue, 0)
    pltpu.make_async_copy(y_ref.at[pl.ds(0, tm), :], y1_sc, sem).wait()
    pltpu.make_async_copy(y_ref.at[pl.ds(0, tm), :], y2_sc, sem).wait()
    gate = gate_ref[...]
    y = gate[:, 0:1] * y1_sc[...] + gate[:, 1:2] * y2_sc[...]
    o_ref[...] = _layer_norm(ALPHA * x_ref[...] + y, g_ref[...], b_ref[...])


def _combine(x, gates, dest_tiles, y, g, b):
    m_rows, d = x.shape
    tm = dest_tiles.shape[2] // 2
    assert tm % ROW_DMA_UNROLL == 0
    return pl.pallas_call(
        _combine_kernel, grid=(m_rows // tm,),
        in_specs=[pl.BlockSpec((1, 1, 2 * tm), lambda i: (i, 0, 0), memory_space=pltpu.SMEM),
                  pl.BlockSpec((tm, d), lambda i: (i, 0)),
                  pl.BlockSpec((tm, 2), lambda i: (i, 0)),
                  _full(g.shape), _full(b.shape),
                  pl.BlockSpec(memory_space=pl.ANY)],
        out_specs=pl.BlockSpec((tm, d), lambda i: (i, 0)),
        out_shape=jax.ShapeDtypeStruct((m_rows, d), F32),
        scratch_shapes=[pltpu.VMEM((tm, d), F32), pltpu.VMEM((tm, d), F32), pltpu.SemaphoreType.DMA(())],
        compiler_params=_params(("arbitrary",)), name="moe_combine",
    )(dest_tiles, x, gates, g, b, y)


def _moe_layer(x, rw, w_in, w_out, layer, g, b, expert_tile):
    m_rows, _ = x.shape
    tm = _token_tile(m_rows)
    route, counts = _router(x, rw["whi"], rw["wlo"], rw["br"])
    eids = route[0:2].astype(jnp.int32)
    gates = route[2:4].T
    rank = route[4:6].astype(jnp.int32)
    counts = counts[:, 0].astype(jnp.int32)
    padded = (counts + expert_tile - 1) // expert_tile * expert_tile
    pad_ends = jnp.cumsum(padded)
    pad_starts = pad_ends - padded
    e_range = jnp.arange(N_EXPERTS, dtype=jnp.int32)
    start_of = jnp.sum(jnp.where(eids[..., None] == e_range, pad_starts, 0), axis=-1)
    dest = start_of + rank
    n_tiles = -(-(2 * m_rows + N_EXPERTS * (expert_tile - 1)) // expert_tile)
    tile_start = jnp.arange(n_tiles, dtype=jnp.int32) * expert_tile
    tile_expert = jnp.minimum(jnp.sum((tile_start[:, None] >= pad_ends[None, :]).astype(jnp.int32), axis=1),
                              N_EXPERTS - 1)
    tile_valid = (tile_start < pad_ends[-1]).astype(jnp.int32)
    dest_tiles = dest.reshape(2, m_rows // tm, tm).transpose(1, 0, 2).reshape(m_rows // tm, 1, 2 * tm)
    buf = _dispatch(x, dest_tiles, n_tiles * expert_tile)
    y = _experts(buf, tile_expert, tile_valid, w_in, w_out, layer, expert_tile)
    return _combine(x, gates, dest_tiles, y, g, b)


def _block_diag(w):
    h, n, _ = w.shape
    eye = jnp.eye(h, dtype=w.dtype)
    return (eye[:, None, :, None] * w[:, :, None, :]).reshape(h * n, h * n)


def _pair_blocks(w):
    h, a, b = w.shape
    w = w.reshape(h // 2, 2, a, b)
    eye = jnp.eye(2, dtype=w.dtype)
    return (eye[None, :, None, :, None] * w[:, :, :, None, :]).reshape(h // 2, 2 * a, 2 * b)


def _time_major(a):
    b, t, c = a.shape
    return a.transpose(1, 0, 2).reshape(t * b, c)


def _batch_major(a, nb):
    return a.reshape(-1, nb, a.shape[-1]).transpose(1, 0, 2)


def _rope_tables(pos):
    half = QK_ROPE // 2
    inv = ROPE_THETA ** (-jnp.arange(half, dtype=F32) / half)
    ang = pos.astype(F32)[:, None] * inv
    cos, sin = jnp.cos(ang), jnp.sin(ang)
    cos_t = jnp.tile(jnp.concatenate([cos, cos], axis=1), (1, N_HEADS))
    sin_t = jnp.tile(jnp.concatenate([-sin, sin], axis=1), (1, N_HEADS))
    return cos_t, sin_t


def kernel(x_prompt, x_sample, state_conv_a, state_conv_b, state_rglru_h, state_conv_c, cache_ckv, cache_kpe, page_table, w_in_even, conv_a_w, conv_b_w, conv_b_b, lru_wa, lru_ba, lru_wx, lru_bx, lru_lambda, w_out_even, w_in_odd, conv_c_w, conv_c_b, ln_c_g, ln_c_b, q_norm_g, w_q_b, kv_norm_g, w_uk, w_uv, w_out_odd, ln_mix_g, ln_mix_b, ln_ffn_g, ln_ffn_b, w_router, b_router, w_exp_in, w_exp_out):
    bp, n_tp, d = x_prompt.shape
    bs, n_ts, _ = x_sample.shape
    past_len = page_table.shape[1] * PAGE_SIZE
    dc = conv_c_w.shape[2]
    q_rank = q_norm_g.shape[1]
    kv_rank = kv_norm_g.shape[1]

    xp = _time_major(x_prompt)
    xs = _time_major(x_sample)
    row2 = lambda v: v.reshape(1, -1)

    wr = jnp.pad(w_router, ((0, 0), (0, LANES - N_EXPERTS)))
    wr_hi = wr.astype(BF16)
    router_w = {"whi": wr_hi, "wlo": (wr - wr_hi.astype(F32)).astype(BF16), "br": b_router.reshape(N_EXPERTS, 1)}
    rope_p = _rope_tables(jnp.arange(n_tp))
    rope_s = _rope_tables(past_len + jnp.arange(n_ts))
    we_in, we_out = w_exp_in.astype(BF16), w_exp_out.astype(BF16)
    cache_kpe_t = jnp.swapaxes(cache_kpe, 2, 3)

    outs = {k: [] for k in ("ca_p", "ca_s", "cb_p", "cb_s", "h_p", "h_s", "cc_p", "cc_s", "ckv_p", "ckv_s", "kpe_p", "kpe_s")}
    for l in range(DEPTH):
        j = l // 2
        lg, lb = row2(ln_mix_g[l]), row2(ln_mix_b[l])
        if l % 2 == 0:
            wa_bd, wx_bd = _block_diag(lru_wa[j]), _block_diag(lru_wx[j])
            hc = wa_bd.shape[0] // 2
            wg = jnp.stack([jnp.concatenate([wa_bd[s * hc:(s + 1) * hc, s * hc:(s + 1) * hc],
                                             wx_bd[s * hc:(s + 1) * hc, s * hc:(s + 1) * hc]], axis=1)
                            for s in range(2)]).astype(BF16)
            w = {"win": w_in_even[j].astype(BF16), "caw": conv_a_w[j], "cbw": conv_b_w[j], "cbb": row2(conv_b_b[j]),
                 "wg": wg, "ba": row2(lru_ba[j]), "bx": row2(lru_bx[j]), "lam": row2(lru_lambda[j]),
                 "wout": w_out_even[j].astype(BF16), "g": lg, "b": lb}
            da = conv_a_w.shape[2]
            db = conv_b_w.shape[2]
            xp, a1, b1, h1 = _even_layer(xp, w, jnp.zeros((2 * bp, da), F32), jnp.zeros((3 * bp, db), F32),
                                         jnp.zeros((bp, db), F32), nb=bp, n_t=n_tp)
            xs, a2, b2, h2 = _even_layer(xs, w, _time_major(state_conv_a[j]), _time_major(state_conv_b[j]),
                                         state_rglru_h[j], nb=bs, n_t=n_ts)
            outs["ca_p"].append(_batch_major(a1, bp)); outs["ca_s"].append(_batch_major(a2, bs))
            outs["cb_p"].append(_batch_major(b1, bp)); outs["cb_s"].append(_batch_major(b2, bs))
            outs["h_p"].append(h1); outs["h_s"].append(h2)
        else:
            nope_all = N_HEADS * QK_NOPE
            wq = w_q_b[j].reshape(q_rank, N_HEADS, QK_NOPE + QK_ROPE)
            wqb = jnp.concatenate([wq[:, :, :QK_NOPE].reshape(q_rank, nope_all),
                                   wq[:, :, QK_NOPE:].reshape(q_rank, N_HEADS * QK_ROPE)], axis=1).astype(BF16)
            o_pe = 2 * dc + q_rank + kv_rank
            win = jnp.concatenate([w_in_odd[j][:, :o_pe], jnp.tile(w_in_odd[j][:, o_pe:], (1, N_HEADS))], axis=1).astype(BF16)
            wuk = _pair_blocks(w_uk[j].transpose(1, 2, 0)).astype(BF16)
            wuv = _pair_blocks(w_uv[j].transpose(1, 0, 2)).astype(BF16)
            w = {"win": win, "ccw": conv_c_w[j], "ccb": row2(conv_c_b[j]), "lcg": row2(ln_c_g[j]), "lcb": row2(ln_c_b[j]),
                 "qg": row2(q_norm_g[j]), "wqb": wqb, "kvg": row2(kv_norm_g[j]), "wuk": wuk}
            width = conv_c_w.shape[1]
            wout = w_out_odd[j].astype(BF16)
            ycp, qp, kp, ckv1, kpe1, c1 = _odd_pre(xp, w, jnp.zeros(((width - 1) * bp, dc), F32), *rope_p, nb=bp, n_t=n_tp)
            ycs, qs, ks, ckv2, kpe2, c2 = _odd_pre(xs, w, _time_major(state_conv_c[j]), *rope_s, nb=bs, n_t=n_ts)
            ydp = _prompt_attention(qp, kp, wuv)
            yds = _sample_attention(qs, ks, wuv, cache_ckv, cache_kpe_t, page_table, j)
            xp = _odd_post(xp, ycp, ydp, wout, lg, lb, nb=bp, n_t=n_tp)
            xs = _odd_post(xs, ycs, yds, wout, lg, lb, nb=bs, n_t=n_ts)
            outs["cc_p"].append(_batch_major(c1, bp)); outs["cc_s"].append(_batch_major(c2, bs))
            outs["ckv_p"].append(ckv1); outs["ckv_s"].append(ckv2)
            outs["kpe_p"].append(kpe1); outs["kpe_s"].append(kpe2)
        fg, fb = row2(ln_ffn_g[l]), row2(ln_ffn_b[l])
        xp = _moe_layer(xp, router_w, we_in, we_out, l, fg, fb, expert_tile=512)
        xs = _moe_layer(xs, router_w, we_in, we_out, l, fg, fb, expert_tile=64)

    y_prompt = _batch_major(xp, bp)
    y_sample = _batch_major(xs, bs)
    st = lambda k: jnp.stack(outs[k])
    st1 = lambda k: jnp.stack(outs[k], axis=1)
    return (y_prompt, y_sample, st("ca_p"), st("ca_s"), st("cb_p"), st("cb_s"), st("h_p"), st("h_s"),
            st("cc_p"), st("cc_s"), st1("ckv_p"), st1("ckv_s"), st1("kpe_p"), st1("kpe_s"))
```

```python
import functools

import jax
import jax.numpy as jnp
from jax import lax
from jax.experimental import pallas as pl
from jax.experimental.pallas import tpu as pltpu

F32 = jnp.float32
BF16 = jnp.bfloat16

DEPTH = 4
N_HEADS = 8
QK_NOPE = 64
QK_ROPE = 32
V_DIM = 64
N_EXPERTS = 16
N_GROUPS = 4
EXPERTS_PER_GROUP = 4
LRU_C = 8.0
ROPE_THETA = 10000.0
ATTN_SCALE = (QK_NOPE + QK_ROPE) ** -0.5
ALPHA = (2 * DEPTH) ** 0.25
PAGE_SIZE = 128

SUBLANES = 8
BF16_SUBLANES = 16
LANES = 128
VMEM_LIMIT_BYTES = 56 * 1024 * 1024

TIME_STEPS_PER_TILE = 32
TOKEN_TILE = 512
ATTN_TILE = 512
PAGES_PER_STEP = 64
ROW_DMA_UNROLL = 8
NEG_INF = float("-inf")


def _params(semantics):
    return pltpu.CompilerParams(dimension_semantics=semantics, vmem_limit_bytes=VMEM_LIMIT_BYTES)


def _full(shape):
    nd = len(shape)
    return pl.BlockSpec(shape, lambda *_: (0,) * nd)


def _layer_norm(x, g, b, eps=1e-5):
    mu = jnp.mean(x, axis=-1, keepdims=True)
    xc = x - mu
    var = jnp.mean(xc * xc, axis=-1, keepdims=True)
    return xc * lax.rsqrt(var + eps) * g + b


def _rms_norm(x, g, eps=1e-6):
    return x * lax.rsqrt(jnp.mean(x * x, axis=-1, keepdims=True) + eps) * g


def _sigmoid(x):
    return 1.0 / (1.0 + jnp.exp(-x))


def _dot(a, b):
    return jnp.dot(a, b, preferred_element_type=F32)


def _dot_nt(a, b):
    return lax.dot_general(a, b, (((1,), (1,)), ((), ())), preferred_element_type=F32)


def _seq_tile(nb, n_t):
    tt = min(n_t, TIME_STEPS_PER_TILE)
    assert n_t % tt == 0 and nb % SUBLANES == 0
    return tt


def _even_kernel(x_ref, win_ref, caw_ref, cbw_ref, cbb_ref, wg_ref, ba_ref, bx_ref, lam_ref, wout_ref,
                 g_ref, b_ref, sa_ref, sb_ref, h0_ref,
                 o_ref, na_ref, nb_ref, hl_ref,
                 ua_ext, vb_ext, h_sc, hs_sc, *, nb, tt):
    tm = tt * nb
    dh = caw_ref.shape[1]
    half = dh // 2

    @pl.when(pl.program_id(0) == 0)
    def _():
        ua_ext[0:2 * nb, :] = sa_ref[...]
        vb_ext[0:3 * nb, :] = sb_ref[...]
        h_sc[...] = h0_ref[...]

    x = x_ref[...]
    xb = x.astype(BF16)

    def proj(j):
        return _dot(xb, win_ref[:, j * dh:(j + 1) * dh])

    ua_ext[2 * nb:2 * nb + tm, :] = proj(1) * proj(2)
    caw = caw_ref[...]
    conv_a = (caw[0:1] * ua_ext[0:tm, :] + caw[1:2] * ua_ext[nb:nb + tm, :]
              + caw[2:3] * ua_ext[2 * nb:2 * nb + tm, :])
    y_a = proj(0) * conv_a
    tail_a = ua_ext[tm:tm + 2 * nb, :]
    na_ref[...] = tail_a
    ua_ext[0:2 * nb, :] = tail_a

    vb_ext[3 * nb:3 * nb + tm, :] = proj(4)
    cbw = cbw_ref[...]
    u_b = (cbb_ref[...] + cbw[0:1] * vb_ext[0:tm, :] + cbw[1:2] * vb_ext[nb:nb + tm, :]
           + cbw[2:3] * vb_ext[2 * nb:2 * nb + tm, :] + cbw[3:4] * vb_ext[3 * nb:3 * nb + tm, :])
    tail_b = vb_ext[tm:tm + 3 * nb, :]
    nb_ref[...] = tail_b
    vb_ext[0:3 * nb, :] = tail_b

    ub16 = u_b.astype(BF16)
    gk0 = _dot(ub16[:, :half], wg_ref[0])
    gk1 = _dot(ub16[:, half:], wg_ref[1])
    r = _sigmoid(jnp.concatenate([gk0[:, :half], gk1[:, :half]], axis=1) + ba_ref[...])
    ig = _sigmoid(jnp.concatenate([gk0[:, half:], gk1[:, half:]], axis=1) + bx_ref[...])
    nlam = -lam_ref[...]
    softplus = jnp.maximum(nlam, 0.0) + jnp.log(1.0 + jnp.exp(-jnp.abs(nlam)))
    log_a = (-LRU_C * r) * softplus
    a = jnp.exp(log_a)
    bterm = jnp.sqrt(1.0 - a * a) * (ig * u_b)

    h = h_sc[...]
    for t in range(tt):
        h = a[t * nb:(t + 1) * nb, :] * h + bterm[t * nb:(t + 1) * nb, :]
        hs_sc[t * nb:(t + 1) * nb, :] = h
    h_sc[...] = h
    hl_ref[...] = h
    y_b = jax.nn.gelu(proj(3), approximate=True) * hs_sc[...]

    m = _dot(y_a.astype(BF16), wout_ref[0:dh, :]) + _dot(y_b.astype(BF16), wout_ref[dh:2 * dh, :])
    o_ref[...] = _layer_norm(ALPHA * x + m, g_ref[...], b_ref[...])


def _even_layer(x, w, sa, sb, h0, *, nb, n_t):
    m_rows, d = x.shape
    dh = w["caw"].shape[1]
    tt = _seq_tile(nb, n_t)
    tm = tt * nb
    kern = functools.partial(_even_kernel, nb=nb, tt=tt)
    row = lambda i: (i, 0)
    in_specs = [pl.BlockSpec((tm, d), row), _full(w["win"].shape), _full(w["caw"].shape), _full(w["cbw"].shape),
                _full(w["cbb"].shape), _full(w["wg"].shape), _full(w["ba"].shape), _full(w["bx"].shape),
                _full(w["lam"].shape), _full(w["wout"].shape), _full(w["g"].shape), _full(w["b"].shape),
                _full(sa.shape), _full(sb.shape), _full(h0.shape)]
    out_shape = (jax.ShapeDtypeStruct((m_rows, d), F32), jax.ShapeDtypeStruct(sa.shape, F32),
                 jax.ShapeDtypeStruct(sb.shape, F32), jax.ShapeDtypeStruct(h0.shape, F32))
    out_specs = (pl.BlockSpec((tm, d), row), _full(sa.shape), _full(sb.shape), _full(h0.shape))
    scratch = [pltpu.VMEM((tm + 2 * nb, dh), F32), pltpu.VMEM((tm + 3 * nb, dh), F32),
               pltpu.VMEM((nb, dh), F32), pltpu.VMEM((tm, dh), F32)]
    return pl.pallas_call(
        kern, grid=(m_rows // tm,), in_specs=in_specs, out_specs=out_specs, out_shape=out_shape,
        scratch_shapes=scratch, compiler_params=_params(("arbitrary",)), name="even_mixer",
    )(x, w["win"], w["caw"], w["cbw"], w["cbb"], w["wg"], w["ba"], w["bx"], w["lam"], w["wout"], w["g"], w["b"],
      sa, sb, h0)


def _odd_pre_kernel(x_ref, win_ref, ccw_ref, ccb_ref, lcg_ref, lcb_ref, qg_ref, wqb_ref, kvg_ref, wuk_ref,
                    cos_ref, sin_ref, sc_ref, perm_ref,
                    yc_ref, q_ref, k_ref, ckv_ref, kpe_ref, nc_ref,
                    c_ext, rl_sc, *, nb, tt, width):
    tm = tt * nb
    dc = ccw_ref.shape[1]
    q_rank = qg_ref.shape[1]
    kv_rank = kvg_ref.shape[1]
    pe_all = N_HEADS * QK_ROPE
    hist = (width - 1) * nb

    @pl.when(pl.program_id(0) == 0)
    def _():
        c_ext[0:hist, :] = sc_ref[...]

    xb = x_ref[...].astype(BF16)
    o_q = 2 * dc
    o_kv = o_q + q_rank
    o_pe = o_kv + kv_rank

    glu = _dot(xb, win_ref[:, 0:dc]) * _sigmoid(_dot(xb, win_ref[:, dc:2 * dc]))
    c_ext[hist:hist + tm, :] = glu
    ccw = ccw_ref[...]
    u_c = ccb_ref[...] + ccw[0:1] * c_ext[0:tm, :]
    for k in range(1, width):
        u_c = u_c + ccw[k:k + 1] * c_ext[k * nb:k * nb + tm, :]
    tail = c_ext[tm:tm + hist, :]
    nc_ref[...] = tail
    c_ext[0:hist, :] = tail
    ln = _layer_norm(u_c, lcg_ref[...], lcb_ref[...])
    yc_ref[...] = ln * _sigmoid(ln)

    def to_sequences(val, store):
        n_lane_tiles = val.shape[1] // LANES
        for c in range(n_lane_tiles):
            rl_sc[c] = val[:, c * LANES:(c + 1) * LANES]

        def body(b, carry):
            store(b, jnp.concatenate([rl_sc[c, pl.ds(b, tt, stride=nb), :] for c in range(n_lane_tiles)], axis=1))
            return carry

        lax.fori_loop(0, nb, body, 0)

    cos = jnp.broadcast_to(cos_ref[...][:, None, :], (tt, nb, pe_all)).reshape(tm, pe_all)
    sin = jnp.broadcast_to(sin_ref[...][:, None, :], (tt, nb, pe_all)).reshape(tm, pe_all)
    lane = lax.broadcasted_iota(jnp.int32, (1, pe_all), 1)
    first_half = (lane % QK_ROPE) < (QK_ROPE // 2)
    head_of_lane = lane // QK_ROPE

    def rope(v):
        swapped = jnp.where(first_half, pltpu.roll(v, pe_all - QK_ROPE // 2, 1), pltpu.roll(v, QK_ROPE // 2, 1))
        return v * cos + swapped * sin

    permute_on_mxu = tt % BF16_SUBLANES == 0

    def to_sequences_bf16(val, store):
        if not permute_on_mxu:
            to_sequences(val, lambda b, v: store(b, v.astype(BF16)))
            return
        pv = _dot(perm_ref[...], val.astype(BF16)).astype(BF16)
        for b in range(nb):
            store(b, pv[b * tt:(b + 1) * tt])

    ckv = _rms_norm(_dot(xb, win_ref[:, o_kv:o_kv + kv_rank]), kvg_ref[...])
    kpe = rope(_dot(xb, win_ref[:, o_pe:o_pe + pe_all]))
    k_all = jnp.concatenate([ckv, kpe], axis=1)

    def store_k32(b, v):
        ckv_ref[b] = v[:, 0:kv_rank]
        kpe_ref[b] = v[:, kv_rank:kv_rank + QK_ROPE]

    def store_k16(b, v):
        k_ref[b] = v

    def store_k(b, v):
        store_k32(b, v)
        store_k16(b, v.astype(BF16))

    if permute_on_mxu:
        to_sequences(k_all, store_k32)
        to_sequences_bf16(k_all, store_k16)
    else:
        to_sequences(k_all, store_k)

    qn = _rms_norm(_dot(xb, win_ref[:, o_q:o_q + q_rank]), qg_ref[...]).astype(BF16)
    nope_all = N_HEADS * QK_NOPE
    q_nope = (_dot(qn, wqb_ref[:, 0:nope_all]) * ATTN_SCALE).astype(BF16)
    q_pe = rope(_dot(qn, wqb_ref[:, nope_all:nope_all + pe_all]) * ATTN_SCALE)
    for p in range(N_HEADS // 2):
        lat2 = _dot(q_nope[:, p * 2 * QK_NOPE:(p + 1) * 2 * QK_NOPE], wuk_ref[p])
        for s in range(2):
            h = 2 * p + s

            def store_q(b, v, h=h):
                q_ref[b, h] = v

            to_sequences_bf16(jnp.concatenate([lat2[:, s * kv_rank:(s + 1) * kv_rank],
                                               jnp.where(head_of_lane == h, q_pe, 0.0)], axis=1), store_q)


def _odd_pre(x, w, sc, cos, sin, *, nb, n_t):
    m_rows, d = x.shape
    width, dc = w["ccw"].shape
    kv_rank = w["kvg"].shape[1]
    pe_all = N_HEADS * QK_ROPE
    kq = kv_rank + pe_all
    tt = _seq_tile(nb, n_t)
    tm = tt * nb
    kern = functools.partial(_odd_pre_kernel, nb=nb, tt=tt, width=width)
    row = lambda i: (i, 0)
    seq = lambda i: (0, i, 0)
    in_specs = [pl.BlockSpec((tm, d), row)] + [_full(w[k].shape) for k in
                                                ("win", "ccw", "ccb", "lcg", "lcb", "qg", "wqb", "kvg", "wuk")]
    out_row = jnp.arange(tm, dtype=jnp.int32)
    src_row = (out_row % tt) * nb + out_row // tt
    perm = (src_row[:, None] == out_row[None, :]).astype(BF16)
    in_specs += [pl.BlockSpec((tt, pe_all), row), pl.BlockSpec((tt, pe_all), row), _full(sc.shape), _full(perm.shape)]
    out_shape = (jax.ShapeDtypeStruct((m_rows, dc), F32),
                 jax.ShapeDtypeStruct((nb, N_HEADS, n_t, kq), BF16),
                 jax.ShapeDtypeStruct((nb, n_t, kq), BF16),
                 jax.ShapeDtypeStruct((nb, n_t, kv_rank), F32),
                 jax.ShapeDtypeStruct((nb, n_t, QK_ROPE), F32),
                 jax.ShapeDtypeStruct(sc.shape, F32))
    out_specs = (pl.BlockSpec((tm, dc), row), pl.BlockSpec((nb, N_HEADS, tt, kq), lambda i: (0, 0, i, 0)),
                 pl.BlockSpec((nb, tt, kq), seq), pl.BlockSpec((nb, tt, kv_rank), seq),
                 pl.BlockSpec((nb, tt, QK_ROPE), seq), _full(sc.shape))
    scratch = [pltpu.VMEM((tm + (width - 1) * nb, dc), F32), pltpu.VMEM((kq // LANES, tm, LANES), F32)]
    return pl.pallas_call(
        kern, grid=(m_rows // tm,), in_specs=in_specs, out_specs=out_specs, out_shape=out_shape,
        scratch_shapes=scratch, compiler_params=_params(("arbitrary",)), name="odd_pre",
    )(x, w["win"], w["ccw"], w["ccb"], w["lcg"], w["lcb"], w["qg"], w["wqb"], w["kvg"], w["wuk"], cos, sin, sc, perm)


def _softmax_step(s, v16, m_sc, l_sc, acc_sc):
    m_prev = m_sc[...]
    m_new = jnp.maximum(m_prev, jnp.max(s, axis=-1, keepdims=True))
    alpha = jnp.exp(m_prev - m_new)
    p = jnp.exp(s - m_new)
    l_sc[...] = alpha * l_sc[...] + jnp.sum(p, axis=-1, keepdims=True)
    acc_sc[...] = alpha * acc_sc[...] + _dot(p.astype(BF16), v16)
    m_sc[...] = m_new


def _softmax_init(m_sc, l_sc, acc_sc):
    m_sc[...] = jnp.full(m_sc.shape, NEG_INF, F32)
    l_sc[...] = jnp.zeros(l_sc.shape, F32)
    acc_sc[...] = jnp.zeros(acc_sc.shape, F32)


def _prompt_attn_kernel(qi_ref, ki_ref, q_ref, k_ref, bias_ref, wuv_ref, o_ref, *state, kv_rank):
    m_sc, l_sc, acc_sc = state[0:N_HEADS], state[N_HEADS:2 * N_HEADS], state[2 * N_HEADS:3 * N_HEADS]
    j = pl.program_id(1)
    qi = qi_ref[j]
    ki = ki_ref[j]

    @pl.when(ki == 0)
    def _():
        for h in range(N_HEADS):
            _softmax_init(m_sc[h], l_sc[h], acc_sc[h])

    k = k_ref[...]
    v16 = k[:, 0:kv_rank]
    diag = (ki == qi).astype(jnp.int32)

    def scores(h):
        return _dot_nt(q_ref[h], k) + bias_ref[diag]

    s_next = scores(0)
    for h in range(N_HEADS):
        s = s_next
        if h + 1 < N_HEADS:
            s_next = scores(h + 1)
        _softmax_step(s, v16, m_sc[h], l_sc[h], acc_sc[h])

    @pl.when(ki == qi)
    def _():
        for p in range(N_HEADS // 2):
            o2 = jnp.concatenate([acc_sc[2 * p][...] / l_sc[2 * p][...],
                                  acc_sc[2 * p + 1][...] / l_sc[2 * p + 1][...]], axis=1)
            o_ref[:, p * 2 * V_DIM:(p + 1) * 2 * V_DIM] = _dot(o2.astype(BF16), wuv_ref[p])


def _prompt_attention(q, k, wuv):
    nb, n_heads, n_t, kq = q.shape
    kv_rank = wuv.shape[1] // 2
    dv = N_HEADS * V_DIM
    tile = min(n_t, ATTN_TILE)
    assert n_t % tile == 0
    nq = n_t // tile
    pairs = [(a, b) for a in range(nq) for b in range(a + 1)]
    qi_tab = jnp.asarray([p[0] for p in pairs], jnp.int32)
    ki_tab = jnp.asarray([p[1] for p in pairs], jnp.int32)
    causal = jnp.where(jnp.arange(tile)[None, :] <= jnp.arange(tile)[:, None], 0.0, NEG_INF).astype(F32)
    bias = jnp.stack([jnp.zeros((tile, tile), F32), causal])
    kern = functools.partial(_prompt_attn_kernel, kv_rank=kv_rank)
    grid_spec = pltpu.PrefetchScalarGridSpec(
        num_scalar_prefetch=2, grid=(nb, len(pairs)),
        in_specs=[pl.BlockSpec((None, n_heads, tile, kq), lambda b, j, qi, ki: (b, 0, qi[j], 0)),
                  pl.BlockSpec((None, tile, kq), lambda b, j, qi, ki: (b, ki[j], 0)),
                  pl.BlockSpec(bias.shape, lambda b, j, qi, ki: (0, 0, 0)),
                  pl.BlockSpec(wuv.shape, lambda b, j, qi, ki: (0, 0, 0))],
        out_specs=pl.BlockSpec((None, tile, dv), lambda b, j, qi, ki: (b, qi[j], 0)),
        scratch_shapes=([pltpu.VMEM((tile, 1), F32)] * (2 * n_heads) + [pltpu.VMEM((tile, kv_rank), F32)] * n_heads))
    return pl.pallas_call(
        kern, grid_spec=grid_spec, out_shape=jax.ShapeDtypeStruct((nb, n_t, dv), F32),
        compiler_params=_params(("arbitrary", "arbitrary")), name="prompt_attention",
    )(qi_tab, ki_tab, q, k, bias, wuv)


def _sample_attn_kernel(pt_ref, q_ref, kn_ref, tsel_ref, wuv_ref, ckv_hbm, kpe_hbm, o_ref,
                        ck_buf, kp_buf, sem, qc_sc, m_sc, l_sc, acc_sc, *,
                        n_pages_step, layer, kv_rank, n_new, rows_per_head):
    b = pl.program_id(0)
    g = pl.program_id(1)
    n_g = pl.num_programs(1)
    step = b * n_g + g
    n_steps = pl.num_programs(0) * n_g
    slot = step % 2

    def page_copies(seq, grp, sl):
        copies = []
        for i in range(n_pages_step):
            page = pt_ref[seq, grp * n_pages_step + i]
            copies.append(pltpu.make_async_copy(
                ckv_hbm.at[page, layer], ck_buf.at[sl, pl.ds(i * PAGE_SIZE, PAGE_SIZE), :], sem.at[sl]))
            copies.append(pltpu.make_async_copy(
                kpe_hbm.at[page, layer], kp_buf.at[sl, :, pl.ds(i * PAGE_SIZE, PAGE_SIZE)], sem.at[sl]))
        return copies

    @pl.when(step == 0)
    def _():
        for c in page_copies(0, 0, 0):
            c.start()

    @pl.when(step + 1 < n_steps)
    def _():
        nxt = step + 1
        for c in page_copies(nxt // n_g, nxt % n_g, 1 - slot):
            c.start()

    @pl.when(g == 0)
    def _():
        _softmax_init(m_sc, l_sc, acc_sc)
        qc_sc[...] = _dot(q_ref[:, kv_rank:], tsel_ref[...]).astype(BF16)

    for c in page_copies(b, g, slot):
        c.wait()

    q = q_ref[...]
    ck = ck_buf[slot].astype(BF16)
    kp_t = kp_buf[slot].astype(BF16)
    s = _dot_nt(q[:, 0:kv_rank], ck) + _dot(qc_sc[...], kp_t)
    _softmax_step(s, ck, m_sc, l_sc, acc_sc)

    @pl.when(g == pl.num_programs(1) - 1)
    def _():
        kn = kn_ref[...]
        s_new = _dot_nt(q, kn)
        t_row = lax.broadcasted_iota(jnp.int32, s_new.shape, 0) % rows_per_head
        t_col = lax.broadcasted_iota(jnp.int32, s_new.shape, 1)
        s_new = jnp.where((t_col <= t_row) & (t_col < n_new), s_new, NEG_INF)
        _softmax_step(s_new, kn[:, 0:kv_rank], m_sc, l_sc, acc_sc)
        o = acc_sc[...] / l_sc[...]
        rp = rows_per_head
        for p in range(N_HEADS // 2):
            o2 = jnp.concatenate([o[2 * p * rp:(2 * p + 1) * rp], o[(2 * p + 1) * rp:(2 * p + 2) * rp]], axis=1)
            o_ref[:, p * 2 * V_DIM:(p + 1) * 2 * V_DIM] = _dot(o2.astype(BF16), wuv_ref[p])


def _sample_attention(q, k, wuv, cache_ckv, cache_kpe_t, page_table, layer):
    nb, n_heads, n_t, kq = q.shape
    kv_rank = wuv.shape[1] // 2
    dv = N_HEADS * V_DIM
    n_pages = page_table.shape[1]
    gp = min(PAGES_PER_STEP, n_pages)
    assert n_pages % gp == 0 and n_t <= SUBLANES
    rp = SUBLANES
    qb = jnp.pad(q, ((0, 0), (0, 0), (0, rp - n_t), (0, 0))).reshape(nb, n_heads * rp, kq)
    kb = jnp.pad(k, ((0, 0), (0, rp - n_t), (0, 0)))
    tsel = jnp.tile(jnp.eye(QK_ROPE, dtype=BF16), (N_HEADS, 1))
    kern = functools.partial(_sample_attn_kernel, n_pages_step=gp, layer=layer, kv_rank=kv_rank, n_new=n_t,
                             rows_per_head=rp)
    in_specs = [pl.BlockSpec((None, n_heads * rp, kq), lambda b, g, pt: (b, 0, 0)),
                pl.BlockSpec((None, rp, kq), lambda b, g, pt: (b, 0, 0)),
                pl.BlockSpec(tsel.shape, lambda b, g, pt: (0, 0)),
                pl.BlockSpec(wuv.shape, lambda b, g, pt: (0, 0, 0)),
                pl.BlockSpec(memory_space=pl.ANY), pl.BlockSpec(memory_space=pl.ANY)]
    grid_spec = pltpu.PrefetchScalarGridSpec(
        num_scalar_prefetch=1, grid=(nb, n_pages // gp), in_specs=in_specs,
        out_specs=pl.BlockSpec((None, rp, dv), lambda b, g, pt: (b, 0, 0)),
        scratch_shapes=[pltpu.VMEM((2, gp * PAGE_SIZE, kv_rank), F32), pltpu.VMEM((2, QK_ROPE, gp * PAGE_SIZE), F32),
                        pltpu.SemaphoreType.DMA((2,)),
                        pltpu.VMEM((n_heads * rp, QK_ROPE), BF16), pltpu.VMEM((n_heads * rp, 1), F32),
                        pltpu.VMEM((n_heads * rp, 1), F32), pltpu.VMEM((n_heads * rp, kv_rank), F32)])
    out = pl.pallas_call(
        kern, grid_spec=grid_spec, out_shape=jax.ShapeDtypeStruct((nb, rp, dv), F32),
        compiler_params=_params(("arbitrary", "arbitrary")), name="sample_attention",
    )(page_table, qb, kb, tsel, wuv, cache_ckv, cache_kpe_t)
    return out[:, :n_t]


def _odd_post_kernel(x_ref, yc_ref, yd_ref, wout_ref, g_ref, b_ref, o_ref, rl_sc, *, nb, tt):
    dc = yc_ref.shape[1]

    n_lane_tiles = rl_sc.shape[0]

    def body(b, carry):
        v = yd_ref[b]
        for c in range(n_lane_tiles):
            rl_sc[c, pl.ds(b, tt, stride=nb), :] = v[:, c * LANES:(c + 1) * LANES]
        return carry

    lax.fori_loop(0, nb, body, 0)
    yd = jnp.concatenate([rl_sc[c] for c in range(n_lane_tiles)], axis=1)
    m = _dot(yc_ref[...].astype(BF16), wout_ref[0:dc, :]) + _dot(yd.astype(BF16), wout_ref[dc:, :])
    o_ref[...] = _layer_norm(ALPHA * x_ref[...] + m, g_ref[...], b_ref[...])


def _odd_post(x, yc, yd, wout, g, b, *, nb, n_t):
    m_rows, d = x.shape
    dv = yd.shape[2]
    tt = _seq_tile(nb, n_t)
    tm = tt * nb
    row = lambda i: (i, 0)
    kern = functools.partial(_odd_post_kernel, nb=nb, tt=tt)
    return pl.pallas_call(
        kern, grid=(m_rows // tm,),
        in_specs=[pl.BlockSpec((tm, d), row), pl.BlockSpec((tm, yc.shape[1]), row),
                  pl.BlockSpec((nb, tt, dv), lambda i: (0, i, 0)),
                  _full(wout.shape), _full(g.shape), _full(b.shape)],
        out_specs=pl.BlockSpec((tm, d), row), out_shape=jax.ShapeDtypeStruct((m_rows, d), F32),
        scratch_shapes=[pltpu.VMEM((dv // LANES, tm, LANES), F32)],
        compiler_params=_params(("arbitrary",)), name="odd_post",
    )(x, yc, yd, wout, g, b)


def _token_tile(m_rows):
    tm = min(m_rows, TOKEN_TILE)
    assert m_rows % tm == 0
    return tm


def _router_kernel(x_ref, whi_ref, wlo_ref, br_ref, tri_ref, o_ref, cnt_ref, base_sc):
    tm = x_ref.shape[0]

    @pl.when(pl.program_id(0) == 0)
    def _():
        base_sc[...] = jnp.zeros(base_sc.shape, F32)

    x = x_ref[...]
    x_hi = x.astype(BF16)
    x_lo = (x - x_hi.astype(F32)).astype(BF16)
    logits = _dot(x_hi, whi_ref[...]) + (_dot(x_lo, whi_ref[...]) + _dot(x_hi, wlo_ref[...]))
    scores = _sigmoid(logits.T[0:N_EXPERTS, :])
    sel = scores + br_ref[...]

    def row(a, e):
        return a[e:e + 1, :]

    best_gs = None
    grp = None
    for gidx in range(N_GROUPS):
        v = [row(sel, gidx * EXPERTS_PER_GROUP + i) for i in range(EXPERTS_PER_GROUP)]
        gs = None
        for i in range(EXPERTS_PER_GROUP):
            for j in range(i + 1, EXPERTS_PER_GROUP):
                pair = v[i] + v[j]
                gs = pair if gs is None else jnp.maximum(gs, pair)
        if best_gs is None:
            best_gs, grp = gs, jnp.zeros(gs.shape, jnp.int32)
        else:
            better = gs > best_gs
            grp = jnp.where(better, gidx, grp)
            best_gs = jnp.where(better, gs, best_gs)

    def pick(a, i):
        out = row(a, i)
        for gidx in range(1, N_GROUPS):
            out = jnp.where(grp == gidx, row(a, gidx * EXPERTS_PER_GROUP + i), out)
        return out

    cand = [pick(sel, i) for i in range(EXPERTS_PER_GROUP)]
    cand_score = [pick(scores, i) for i in range(EXPERTS_PER_GROUP)]

    def argmax_first(vals, exclude=None):
        best, idx = None, None
        for i, v in enumerate(vals):
            if exclude is not None:
                v = jnp.where(exclude == i, NEG_INF, v)
            if best is None:
                best, idx = v, jnp.zeros(v.shape, jnp.int32)
            else:
                better = v > best
                idx = jnp.where(better, i, idx)
                best = jnp.where(better, v, best)
        return idx

    loc1 = argmax_first(cand)
    loc2 = argmax_first(cand, exclude=loc1)

    def take(vals, idx):
        out = vals[0]
        for i in range(1, len(vals)):
            out = jnp.where(idx == i, vals[i], out)
        return out

    g1 = take(cand_score, loc1)
    g2 = take(cand_score, loc2)
    gsum = g1 + g2
    e1 = grp * EXPERTS_PER_GROUP + loc1
    e2 = grp * EXPERTS_PER_GROUP + loc2

    e_iota = lax.broadcasted_iota(jnp.int32, (N_EXPERTS, tm), 0)
    hit1 = e_iota == e1
    hit2 = e_iota == e2
    both = jnp.where(hit1 | hit2, 1.0, 0.0)
    before = _dot(both.astype(BF16), tri_ref[...]) + base_sc[...]
    r1 = jnp.sum(jnp.where(hit1, before, 0.0), axis=0, keepdims=True)
    r2 = jnp.sum(jnp.where(hit2, before, 0.0), axis=0, keepdims=True)
    base_sc[...] = base_sc[...] + jnp.sum(both, axis=1, keepdims=True)
    cnt_ref[...] = jnp.broadcast_to(base_sc[...], cnt_ref.shape)

    zero = jnp.zeros((1, tm), F32)
    o_ref[...] = jnp.concatenate([e1.astype(F32), e2.astype(F32), g1 / gsum, g2 / gsum, r1, r2, zero, zero], axis=0)


def _router(x, whi, wlo, br):
    m_rows, d = x.shape
    tm = _token_tile(m_rows)
    tri = jnp.triu(jnp.ones((tm, tm), BF16), k=1)
    return pl.pallas_call(
        _router_kernel, grid=(m_rows // tm,),
        in_specs=[pl.BlockSpec((tm, d), lambda i: (i, 0)), _full(whi.shape), _full(wlo.shape), _full(br.shape),
                  _full(tri.shape)],
        out_specs=(pl.BlockSpec((SUBLANES, tm), lambda i: (0, i)), _full((N_EXPERTS, LANES))),
        out_shape=(jax.ShapeDtypeStruct((SUBLANES, m_rows), F32), jax.ShapeDtypeStruct((N_EXPERTS, LANES), F32)),
        scratch_shapes=[pltpu.VMEM((N_EXPERTS, 1), F32)],
        compiler_params=_params(("arbitrary",)), name="router",
    )(x, whi, wlo, br, tri)


def _row_copy(src_ref, src_row, dst_ref, dst_row, sem):
    return pltpu.make_async_copy(src_ref.at[pl.ds(src_row, 1), :], dst_ref.at[pl.ds(dst_row, 1), :], sem)


HI_HALF = 0xFFFF0000


def _pack_bf16_pairs(x):
    n = x.shape[1] // 2
    lo = pltpu.bitcast(x[:, :n].astype(BF16).astype(F32), jnp.uint32)
    hi = pltpu.bitcast(x[:, n:].astype(BF16).astype(F32), jnp.uint32)
    return (lo >> 16) | (hi & jnp.uint32(HI_HALF))


def _unpack_bf16_pairs(p):
    lo = pltpu.bitcast(p << 16, F32).astype(BF16)
    hi = pltpu.bitcast(p & jnp.uint32(HI_HALF), F32).astype(BF16)
    return jnp.concatenate([lo, hi], axis=1)


def _dispatch_kernel(dest_ref, x_ref, buf_in_ref, buf_ref, xp_sc, sem):
    del buf_in_ref
    tm = x_ref.shape[0]
    xp_sc[...] = _pack_bf16_pairs(x_ref[...])

    def issue(i, carry):
        for u in range(ROW_DMA_UNROLL):
            r = i * ROW_DMA_UNROLL + u
            _row_copy(xp_sc, r, buf_ref, dest_ref[0, 0, r], sem).start(priority=u % 2)
            _row_copy(xp_sc, r, buf_ref, dest_ref[0, 0, tm + r], sem).start(priority=(u + 1) % 2)
        return carry

    lax.fori_loop(0, tm // ROW_DMA_UNROLL, issue, 0)
    for _ in range(2):
        pltpu.make_async_copy(xp_sc, buf_ref.at[pl.ds(0, tm), :], sem).wait()


def _dispatch(x, dest_tiles, n_slots):
    m_rows, d = x.shape
    tm = dest_tiles.shape[2] // 2
    assert tm % ROW_DMA_UNROLL == 0
    buf0 = jnp.zeros((n_slots, d // 2), jnp.uint32)
    return pl.pallas_call(
        _dispatch_kernel, grid=(m_rows // tm,),
        in_specs=[pl.BlockSpec((1, 1, 2 * tm), lambda i: (i, 0, 0), memory_space=pltpu.SMEM),
                  pl.BlockSpec((tm, d), lambda i: (i, 0)),
                  pl.BlockSpec(memory_space=pl.ANY)],
        out_specs=pl.BlockSpec(memory_space=pl.ANY),
        out_shape=jax.ShapeDtypeStruct((n_slots, d // 2), jnp.uint32),
        scratch_shapes=[pltpu.VMEM((tm, d // 2), jnp.uint32), pltpu.SemaphoreType.DMA(())],
        input_output_aliases={2: 0},
        compiler_params=_params(("arbitrary",)), name="moe_dispatch",
    )(dest_tiles, x, buf0)


def _expert_kernel(te_ref, tv_ref, x_ref, win_ref, wout_ref, o_ref):
    i = pl.program_id(0)
    de = wout_ref.shape[0]

    @pl.when(tv_ref[i] != 0)
    def _():
        h = _dot(_unpack_bf16_pairs(x_ref[...]), win_ref[...])
        hg = h[:, 0:de]
        act = hg * _sigmoid(hg) * h[:, de:2 * de]
        o_ref[...] = _dot(act.astype(BF16), wout_ref[...])

    @pl.when(tv_ref[i] == 0)
    def _():
        o_ref[...] = jnp.zeros(o_ref.shape, F32)


def _experts(buf, tile_expert, tile_valid, w_in, w_out, layer, tile_rows):
    n_slots = buf.shape[0]
    d, de2 = w_in.shape[2], w_in.shape[3]
    grid_spec = pltpu.PrefetchScalarGridSpec(
        num_scalar_prefetch=2, grid=(n_slots // tile_rows,),
        in_specs=[pl.BlockSpec((tile_rows, d // 2), lambda i, te, tv: (i, 0)),
                  pl.BlockSpec((None, None, d, de2), lambda i, te, tv: (layer, te[i], 0, 0)),
                  pl.BlockSpec((None, None, de2 // 2, d), lambda i, te, tv: (layer, te[i], 0, 0))],
        out_specs=pl.BlockSpec((tile_rows, d), lambda i, te, tv: (i, 0)))
    return pl.pallas_call(
        _expert_kernel, grid_spec=grid_spec, out_shape=jax.ShapeDtypeStruct((n_slots, d), F32),
        compiler_params=_params(("arbitrary",)), name="moe_experts",
    )(tile_expert, tile_valid, buf, w_in, w_out)


def _combine_kernel(dest_ref, dest_next_ref, x_ref, gate_ref, g_ref, b_ref, y_ref, o_ref, y1_sc, y2_sc, sem):
    tm = x_ref.shape[0]
    i = pl.program_id(0)
    slot = i % 2

    def gather(idx_ref, sl):
        def issue(j, carry):
            for u in range(ROW_DMA_UNROLL):
                r = j * ROW_DMA_UNROLL + u
                _row_copy(y_ref, idx_ref[0, 0, r], y1_sc.at[sl], r, sem.at[sl]).start(priority=u % 2)
                _row_copy(y_ref, idx_ref[0, 0, tm + r], y2_sc.at[sl], r, sem.at[sl]).start(priority=(u + 1) % 2)
            return carry

        lax.fori_loop(0, tm // ROW_DMA_UNROLL, issue, 0)

    @pl.when(i == 0)
    def _():
        gather(dest_ref, 0)

    @pl.when(i + 1 < pl.num_programs(0))
    def _():
        gather(dest_next_ref, 1 - slot)

    pltpu.make_async_copy(y_ref.at[pl.ds(0, tm), :], y1_sc.at[slot], sem.at[slot]).wait()
    pltpu.make_async_copy(y_ref.at[pl.ds(0, tm), :], y2_sc.at[slot], sem.at[slot]).wait()
    gate = gate_ref[...]
    y = gate[:, 0:1] * y1_sc[slot] + gate[:, 1:2] * y2_sc[slot]
    o_ref[...] = _layer_norm(ALPHA * x_ref[...] + y, g_ref[...], b_ref[...])


def _combine(x, gates, dest_tiles, y, g, b):
    m_rows, d = x.shape
    tm = dest_tiles.shape[2] // 2
    assert tm % ROW_DMA_UNROLL == 0
    last = m_rows // tm - 1
    return pl.pallas_call(
        _combine_kernel, grid=(m_rows // tm,),
        in_specs=[pl.BlockSpec((1, 1, 2 * tm), lambda i: (i, 0, 0), memory_space=pltpu.SMEM),
                  pl.BlockSpec((1, 1, 2 * tm), lambda i: (jnp.minimum(i + 1, last), 0, 0), memory_space=pltpu.SMEM),
                  pl.BlockSpec((tm, d), lambda i: (i, 0)),
                  pl.BlockSpec((tm, 2), lambda i: (i, 0)),
                  _full(g.shape), _full(b.shape),
                  pl.BlockSpec(memory_space=pl.ANY)],
        out_specs=pl.BlockSpec((tm, d), lambda i: (i, 0)),
        out_shape=jax.ShapeDtypeStruct((m_rows, d), F32),
        scratch_shapes=[pltpu.VMEM((2, tm, d), F32), pltpu.VMEM((2, tm, d), F32), pltpu.SemaphoreType.DMA((2,))],
        compiler_params=_params(("arbitrary",)), name="moe_combine",
    )(dest_tiles, dest_tiles, x, gates, g, b, y)


def _moe_layer(x, rw, w_in, w_out, layer, g, b, expert_tile):
    m_rows, _ = x.shape
    tm = _token_tile(m_rows)
    route, counts = _router(x, rw["whi"], rw["wlo"], rw["br"])
    eids = route[0:2].astype(jnp.int32)
    gates = route[2:4].T
    rank = route[4:6].astype(jnp.int32)
    counts = counts[:, 0].astype(jnp.int32)
    padded = (counts + expert_tile - 1) // expert_tile * expert_tile
    pad_ends = jnp.cumsum(padded)
    pad_starts = pad_ends - padded
    e_range = jnp.arange(N_EXPERTS, dtype=jnp.int32)
    start_of = jnp.sum(jnp.where(eids[..., None] == e_range, pad_starts, 0), axis=-1)
    dest = start_of + rank
    n_tiles = -(-(2 * m_rows + N_EXPERTS * (expert_tile - 1)) // expert_tile)
    tile_start = jnp.arange(n_tiles, dtype=jnp.int32) * expert_tile
    tile_expert = jnp.minimum(jnp.sum((tile_start[:, None] >= pad_ends[None, :]).astype(jnp.int32), axis=1),
                              N_EXPERTS - 1)
    tile_valid = (tile_start < pad_ends[-1]).astype(jnp.int32)
    dest_tiles = dest.reshape(2, m_rows // tm, tm).transpose(1, 0, 2).reshape(m_rows // tm, 1, 2 * tm)
    buf = _dispatch(x, dest_tiles, n_tiles * expert_tile)
    y = _experts(buf, tile_expert, tile_valid, w_in, w_out, layer, expert_tile)
    return _combine(x, gates, dest_tiles, y, g, b)


def _block_diag(w):
    h, n, _ = w.shape
    eye = jnp.eye(h, dtype=w.dtype)
    return (eye[:, None, :, None] * w[:, :, None, :]).reshape(h * n, h * n)


def _pair_blocks(w):
    h, a, b = w.shape
    w = w.reshape(h // 2, 2, a, b)
    eye = jnp.eye(2, dtype=w.dtype)
    return (eye[None, :, None, :, None] * w[:, :, :, None, :]).reshape(h // 2, 2 * a, 2 * b)


def _time_major(a):
    b, t, c = a.shape
    return a.transpose(1, 0, 2).reshape(t * b, c)


def _batch_major(a, nb):
    return a.reshape(-1, nb, a.shape[-1]).transpose(1, 0, 2)


def _rope_tables(pos):
    half = QK_ROPE // 2
    inv = ROPE_THETA ** (-jnp.arange(half, dtype=F32) / half)
    ang = pos.astype(F32)[:, None] * inv
    cos, sin = jnp.cos(ang), jnp.sin(ang)
    cos_t = jnp.tile(jnp.concatenate([cos, cos], axis=1), (1, N_HEADS))
    sin_t = jnp.tile(jnp.concatenate([-sin, sin], axis=1), (1, N_HEADS))
    return cos_t, sin_t


def kernel(x_prompt, x_sample, state_conv_a, state_conv_b, state_rglru_h, state_conv_c, cache_ckv, cache_kpe, page_table, w_in_even, conv_a_w, conv_b_w, conv_b_b, lru_wa, lru_ba, lru_wx, lru_bx, lru_lambda, w_out_even, w_in_odd, conv_c_w, conv_c_b, ln_c_g, ln_c_b, q_norm_g, w_q_b, kv_norm_g, w_uk, w_uv, w_out_odd, ln_mix_g, ln_mix_b, ln_ffn_g, ln_ffn_b, w_router, b_router, w_exp_in, w_exp_out):
    bp, n_tp, d = x_prompt.shape
    bs, n_ts, _ = x_sample.shape
    past_len = page_table.shape[1] * PAGE_SIZE
    dc = conv_c_w.shape[2]
    q_rank = q_norm_g.shape[1]
    kv_rank = kv_norm_g.shape[1]

    xp = _time_major(x_prompt)
    xs = _time_major(x_sample)
    row2 = lambda v: v.reshape(1, -1)

    wr = jnp.pad(w_router, ((0, 0), (0, LANES - N_EXPERTS)))
    wr_hi = wr.astype(BF16)
    router_w = {"whi": wr_hi, "wlo": (wr - wr_hi.astype(F32)).astype(BF16), "br": b_router.reshape(N_EXPERTS, 1)}
    rope_p = _rope_tables(jnp.arange(n_tp))
    rope_s = _rope_tables(past_len + jnp.arange(n_ts))
    we_in, we_out = w_exp_in.astype(BF16), w_exp_out.astype(BF16)
    cache_kpe_t = jnp.swapaxes(cache_kpe, 2, 3)

    outs = {k: [] for k in ("ca_p", "ca_s", "cb_p", "cb_s", "h_p", "h_s", "cc_p", "cc_s", "ckv_p", "ckv_s", "kpe_p", "kpe_s")}
    for l in range(DEPTH):
        j = l // 2
        lg, lb = row2(ln_mix_g[l]), row2(ln_mix_b[l])
        if l % 2 == 0:
            wa_bd, wx_bd = _block_diag(lru_wa[j]), _block_diag(lru_wx[j])
            hc = wa_bd.shape[0] // 2
            wg = jnp.stack([jnp.concatenate([wa_bd[s * hc:(s + 1) * hc, s * hc:(s + 1) * hc],
                                             wx_bd[s * hc:(s + 1) * hc, s * hc:(s + 1) * hc]], axis=1)
                            for s in range(2)]).astype(BF16)
            w = {"win": w_in_even[j].astype(BF16), "caw": conv_a_w[j], "cbw": conv_b_w[j], "cbb": row2(conv_b_b[j]),
                 "wg": wg, "ba": row2(lru_ba[j]), "bx": row2(lru_bx[j]), "lam": row2(lru_lambda[j]),
                 "wout": w_out_even[j].astype(BF16), "g": lg, "b": lb}
            da = conv_a_w.shape[2]
            db = conv_b_w.shape[2]
            xp, a1, b1, h1 = _even_layer(xp, w, jnp.zeros((2 * bp, da), F32), jnp.zeros((3 * bp, db), F32),
                                         jnp.zeros((bp, db), F32), nb=bp, n_t=n_tp)
            xs, a2, b2, h2 = _even_layer(xs, w, _time_major(state_conv_a[j]), _time_major(state_conv_b[j]),
                                         state_rglru_h[j], nb=bs, n_t=n_ts)
            outs["ca_p"].append(_batch_major(a1, bp)); outs["ca_s"].append(_batch_major(a2, bs))
            outs["cb_p"].append(_batch_major(b1, bp)); outs["cb_s"].append(_batch_major(b2, bs))
            outs["h_p"].append(h1); outs["h_s"].append(h2)
        else:
            nope_all = N_HEADS * QK_NOPE
            wq = w_q_b[j].reshape(q_rank, N_HEADS, QK_NOPE + QK_ROPE)
            wqb = jnp.concatenate([wq[:, :, :QK_NOPE].reshape(q_rank, nope_all),
                                   wq[:, :, QK_NOPE:].reshape(q_rank, N_HEADS * QK_ROPE)], axis=1).astype(BF16)
            o_pe = 2 * dc + q_rank + kv_rank
            win = jnp.concatenate([w_in_odd[j][:, :o_pe], jnp.tile(w_in_odd[j][:, o_pe:], (1, N_HEADS))], axis=1).astype(BF16)
            wuk = _pair_blocks(w_uk[j].transpose(1, 2, 0)).astype(BF16)
            wuv = _pair_blocks(w_uv[j].transpose(1, 0, 2)).astype(BF16)
            w = {"win": win, "ccw": conv_c_w[j], "ccb": row2(conv_c_b[j]), "lcg": row2(ln_c_g[j]), "lcb": row2(ln_c_b[j]),
                 "qg": row2(q_norm_g[j]), "wqb": wqb, "kvg": row2(kv_norm_g[j]), "wuk": wuk}
            width = conv_c_w.shape[1]
            wout = w_out_odd[j].astype(BF16)
            ycp, qp, kp, ckv1, kpe1, c1 = _odd_pre(xp, w, jnp.zeros(((width - 1) * bp, dc), F32), *rope_p, nb=bp, n_t=n_tp)
            ycs, qs, ks, ckv2, kpe2, c2 = _odd_pre(xs, w, _time_major(state_conv_c[j]), *rope_s, nb=bs, n_t=n_ts)
            ydp = _prompt_attention(qp, kp, wuv)
            yds = _sample_attention(qs, ks, wuv, cache_ckv, cache_kpe_t, page_table, j)
            xp = _odd_post(xp, ycp, ydp, wout, lg, lb, nb=bp, n_t=n_tp)
            xs = _odd_post(xs, ycs, yds, wout, lg, lb, nb=bs, n_t=n_ts)
            outs["cc_p"].append(_batch_major(c1, bp)); outs["cc_s"].append(_batch_major(c2, bs))
            outs["ckv_p"].append(ckv1); outs["ckv_s"].append(ckv2)
            outs["kpe_p"].append(kpe1); outs["kpe_s"].append(kpe2)
        fg, fb = row2(ln_ffn_g[l]), row2(ln_ffn_b[l])
        xp = _moe_layer(xp, router_w, we_in, we_out, l, fg, fb, expert_tile=512)
        xs = _moe_layer(xs, router_w, we_in, we_out, l, fg, fb, expert_tile=64)

    y_prompt = _batch_major(xp, bp)
    y_sample = _batch_major(xs, bs)
    st = lambda k: jnp.stack(outs[k])
    st1 = lambda k: jnp.stack(outs[k], axis=1)
    return (y_prompt, y_sample, st("ca_p"), st("ca_s"), st("cb_p"), st("cb_s"), st("h_p"), st("h_s"),
            st("cc_p"), st("cc_s"), st1("ckv_p"), st1("ckv_s"), st1("kpe_p"), st1("kpe_s"))
```

```python
import functools

import jax
import jax.numpy as jnp
from jax import lax
from jax.experimental import pallas as pl
from jax.experimental.pallas import tpu as pltpu

F32 = jnp.float32
BF16 = jnp.bfloat16

DEPTH = 4
N_HEADS = 8
QK_NOPE = 64
QK_ROPE = 32
V_DIM = 64
N_EXPERTS = 16
N_GROUPS = 4
EXPERTS_PER_GROUP = 4
LRU_C = 8.0
ROPE_THETA = 10000.0
ATTN_SCALE = (QK_NOPE + QK_ROPE) ** -0.5
ALPHA = (2 * DEPTH) ** 0.25
PAGE_SIZE = 128

SUBLANES = 8
BF16_SUBLANES = 16
LANES = 128
VMEM_LIMIT_BYTES = 56 * 1024 * 1024

TIME_STEPS_PER_TILE = 32
TOKEN_TILE = 512
ATTN_TILE = 512
PAGES_PER_STEP = 64
NEG_INF = float("-inf")


def _params(semantics):
    return pltpu.CompilerParams(dimension_semantics=semantics, vmem_limit_bytes=VMEM_LIMIT_BYTES)


def _full(shape):
    nd = len(shape)
    return pl.BlockSpec(shape, lambda *_: (0,) * nd)


def _layer_norm(x, g, b, eps=1e-5):
    mu = jnp.mean(x, axis=-1, keepdims=True)
    xc = x - mu
    var = jnp.mean(xc * xc, axis=-1, keepdims=True)
    return xc * lax.rsqrt(var + eps) * g + b


def _rms_norm(x, g, eps=1e-6):
    return x * lax.rsqrt(jnp.mean(x * x, axis=-1, keepdims=True) + eps) * g


def _sigmoid(x):
    return 1.0 / (1.0 + jnp.exp(-x))


def _dot(a, b):
    return jnp.dot(a, b, preferred_element_type=F32)


def _dot_nt(a, b):
    return lax.dot_general(a, b, (((1,), (1,)), ((), ())), preferred_element_type=F32)


def _seq_tile(nb, n_t):
    tt = min(n_t, TIME_STEPS_PER_TILE)
    assert n_t % tt == 0 and nb % SUBLANES == 0
    return tt


def _even_kernel(x_ref, win_ref, caw_ref, cbw_ref, cbb_ref, wg_ref, ba_ref, bx_ref, lam_ref, wout_ref,
                 g_ref, b_ref, sa_ref, sb_ref, h0_ref,
                 o_ref, na_ref, nb_ref, hl_ref,
                 ua_ext, vb_ext, h_sc, hs_sc, *, nb, tt):
    tm = tt * nb
    dh = caw_ref.shape[1]
    half = dh // 2

    @pl.when(pl.program_id(0) == 0)
    def _():
        ua_ext[0:2 * nb, :] = sa_ref[...]
        vb_ext[0:3 * nb, :] = sb_ref[...]
        h_sc[...] = h0_ref[...]

    x = x_ref[...]
    xb = x.astype(BF16)

    def proj(j):
        return _dot(xb, win_ref[:, j * dh:(j + 1) * dh])

    ua_ext[2 * nb:2 * nb + tm, :] = proj(1) * proj(2)
    caw = caw_ref[...]
    conv_a = (caw[0:1] * ua_ext[0:tm, :] + caw[1:2] * ua_ext[nb:nb + tm, :]
              + caw[2:3] * ua_ext[2 * nb:2 * nb + tm, :])
    y_a = proj(0) * conv_a
    tail_a = ua_ext[tm:tm + 2 * nb, :]
    na_ref[...] = tail_a
    ua_ext[0:2 * nb, :] = tail_a

    vb_ext[3 * nb:3 * nb + tm, :] = proj(4)
    cbw = cbw_ref[...]
    u_b = (cbb_ref[...] + cbw[0:1] * vb_ext[0:tm, :] + cbw[1:2] * vb_ext[nb:nb + tm, :]
           + cbw[2:3] * vb_ext[2 * nb:2 * nb + tm, :] + cbw[3:4] * vb_ext[3 * nb:3 * nb + tm, :])
    tail_b = vb_ext[tm:tm + 3 * nb, :]
    nb_ref[...] = tail_b
    vb_ext[0:3 * nb, :] = tail_b

    ub16 = u_b.astype(BF16)
    gk0 = _dot(ub16[:, :half], wg_ref[0])
    gk1 = _dot(ub16[:, half:], wg_ref[1])
    r = _sigmoid(jnp.concatenate([gk0[:, :half], gk1[:, :half]], axis=1) + ba_ref[...])
    ig = _sigmoid(jnp.concatenate([gk0[:, half:], gk1[:, half:]], axis=1) + bx_ref[...])
    nlam = -lam_ref[...]
    softplus = jnp.maximum(nlam, 0.0) + jnp.log(1.0 + jnp.exp(-jnp.abs(nlam)))
    log_a = (-LRU_C * r) * softplus
    a = jnp.exp(log_a)
    bterm = jnp.sqrt(1.0 - a * a) * (ig * u_b)

    h = h_sc[...]
    for t in range(tt):
        h = a[t * nb:(t + 1) * nb, :] * h + bterm[t * nb:(t + 1) * nb, :]
        hs_sc[t * nb:(t + 1) * nb, :] = h
    h_sc[...] = h
    hl_ref[...] = h
    y_b = jax.nn.gelu(proj(3), approximate=True) * hs_sc[...]

    m = _dot(y_a.astype(BF16), wout_ref[0:dh, :]) + _dot(y_b.astype(BF16), wout_ref[dh:2 * dh, :])
    o_ref[...] = _layer_norm(ALPHA * x + m, g_ref[...], b_ref[...])


def _even_layer(x, w, sa, sb, h0, *, nb, n_t):
    m_rows, d = x.shape
    dh = w["caw"].shape[1]
    tt = _seq_tile(nb, n_t)
    tm = tt * nb
    kern = functools.partial(_even_kernel, nb=nb, tt=tt)
    row = lambda i: (i, 0)
    in_specs = [pl.BlockSpec((tm, d), row), _full(w["win"].shape), _full(w["caw"].shape), _full(w["cbw"].shape),
                _full(w["cbb"].shape), _full(w["wg"].shape), _full(w["ba"].shape), _full(w["bx"].shape),
                _full(w["lam"].shape), _full(w["wout"].shape), _full(w["g"].shape), _full(w["b"].shape),
                _full(sa.shape), _full(sb.shape), _full(h0.shape)]
    out_shape = (jax.ShapeDtypeStruct((m_rows, d), F32), jax.ShapeDtypeStruct(sa.shape, F32),
                 jax.ShapeDtypeStruct(sb.shape, F32), jax.ShapeDtypeStruct(h0.shape, F32))
    out_specs = (pl.BlockSpec((tm, d), row), _full(sa.shape), _full(sb.shape), _full(h0.shape))
    scratch = [pltpu.VMEM((tm + 2 * nb, dh), F32), pltpu.VMEM((tm + 3 * nb, dh), F32),
               pltpu.VMEM((nb, dh), F32), pltpu.VMEM((tm, dh), F32)]
    return pl.pallas_call(
        kern, grid=(m_rows // tm,), in_specs=in_specs, out_specs=out_specs, out_shape=out_shape,
        scratch_shapes=scratch, compiler_params=_params(("arbitrary",)), name="even_mixer",
    )(x, w["win"], w["caw"], w["cbw"], w["cbb"], w["wg"], w["ba"], w["bx"], w["lam"], w["wout"], w["g"], w["b"],
      sa, sb, h0)


def _odd_pre_kernel(x_ref, win_ref, ccw_ref, ccb_ref, lcg_ref, lcb_ref, qg_ref, wqb_ref, kvg_ref, wuk_ref,
                    cos_ref, sin_ref, sc_ref, perm_ref,
                    yc_ref, q_ref, k_ref, ckv_ref, kpe_ref, nc_ref,
                    c_ext, rl_sc, *, nb, tt, width):
    tm = tt * nb
    dc = ccw_ref.shape[1]
    q_rank = qg_ref.shape[1]
    kv_rank = kvg_ref.shape[1]
    pe_all = N_HEADS * QK_ROPE
    hist = (width - 1) * nb

    @pl.when(pl.program_id(0) == 0)
    def _():
        c_ext[0:hist, :] = sc_ref[...]

    xb = x_ref[...].astype(BF16)
    o_q = 2 * dc
    o_kv = o_q + q_rank
    o_pe = o_kv + kv_rank

    glu = _dot(xb, win_ref[:, 0:dc]) * _sigmoid(_dot(xb, win_ref[:, dc:2 * dc]))
    c_ext[hist:hist + tm, :] = glu
    ccw = ccw_ref[...]
    u_c = ccb_ref[...] + ccw[0:1] * c_ext[0:tm, :]
    for k in range(1, width):
        u_c = u_c + ccw[k:k + 1] * c_ext[k * nb:k * nb + tm, :]
    tail = c_ext[tm:tm + hist, :]
    nc_ref[...] = tail
    c_ext[0:hist, :] = tail
    ln = _layer_norm(u_c, lcg_ref[...], lcb_ref[...])
    yc_ref[...] = ln * _sigmoid(ln)

    def to_sequences(val, store):
        n_lane_tiles = val.shape[1] // LANES
        for c in range(n_lane_tiles):
            rl_sc[c] = val[:, c * LANES:(c + 1) * LANES]

        def body(b, carry):
            store(b, jnp.concatenate([rl_sc[c, pl.ds(b, tt, stride=nb), :] for c in range(n_lane_tiles)], axis=1))
            return carry

        lax.fori_loop(0, nb, body, 0)

    cos = jnp.broadcast_to(cos_ref[...][:, None, :], (tt, nb, pe_all)).reshape(tm, pe_all)
    sin = jnp.broadcast_to(sin_ref[...][:, None, :], (tt, nb, pe_all)).reshape(tm, pe_all)
    lane = lax.broadcasted_iota(jnp.int32, (1, pe_all), 1)
    first_half = (lane % QK_ROPE) < (QK_ROPE // 2)
    head_of_lane = lane // QK_ROPE

    def rope(v):
        swapped = jnp.where(first_half, pltpu.roll(v, pe_all - QK_ROPE // 2, 1), pltpu.roll(v, QK_ROPE // 2, 1))
        return v * cos + swapped * sin

    permute_on_mxu = tt % BF16_SUBLANES == 0

    def to_sequences_bf16(val, store):
        if not permute_on_mxu:
            to_sequences(val, lambda b, v: store(b, v.astype(BF16)))
            return
        pv = _dot(perm_ref[...], val.astype(BF16)).astype(BF16)
        for b in range(nb):
            store(b, pv[b * tt:(b + 1) * tt])

    ckv = _rms_norm(_dot(xb, win_ref[:, o_kv:o_kv + kv_rank]), kvg_ref[...])
    kpe = rope(_dot(xb, win_ref[:, o_pe:o_pe + pe_all]))
    k_all = jnp.concatenate([ckv, kpe], axis=1)

    def store_k32(b, v):
        ckv_ref[b] = v[:, 0:kv_rank]
        kpe_ref[b] = v[:, kv_rank:kv_rank + QK_ROPE]

    def store_k16(b, v):
        k_ref[b] = v

    def store_k(b, v):
        store_k32(b, v)
        store_k16(b, v.astype(BF16))

    if permute_on_mxu:
        to_sequences(k_all, store_k32)
        to_sequences_bf16(k_all, store_k16)
    else:
        to_sequences(k_all, store_k)

    qn = _rms_norm(_dot(xb, win_ref[:, o_q:o_q + q_rank]), qg_ref[...]).astype(BF16)
    nope_all = N_HEADS * QK_NOPE
    q_nope = (_dot(qn, wqb_ref[:, 0:nope_all]) * ATTN_SCALE).astype(BF16)
    q_pe = rope(_dot(qn, wqb_ref[:, nope_all:nope_all + pe_all]) * ATTN_SCALE)
    for p in range(N_HEADS // 2):
        lat2 = _dot(q_nope[:, p * 2 * QK_NOPE:(p + 1) * 2 * QK_NOPE], wuk_ref[p])
        for s in range(2):
            h = 2 * p + s

            def store_q(b, v, h=h):
                q_ref[b, h] = v

            to_sequences_bf16(jnp.concatenate([lat2[:, s * kv_rank:(s + 1) * kv_rank],
                                               jnp.where(head_of_lane == h, q_pe, 0.0)], axis=1), store_q)


def _odd_pre(x, w, sc, cos, sin, *, nb, n_t):
    m_rows, d = x.shape
    width, dc = w["ccw"].shape
    kv_rank = w["kvg"].shape[1]
    pe_all = N_HEADS * QK_ROPE
    kq = kv_rank + pe_all
    tt = _seq_tile(nb, n_t)
    tm = tt * nb
    kern = functools.partial(_odd_pre_kernel, nb=nb, tt=tt, width=width)
    row = lambda i: (i, 0)
    seq = lambda i: (0, i, 0)
    in_specs = [pl.BlockSpec((tm, d), row)] + [_full(w[k].shape) for k in
                                                ("win", "ccw", "ccb", "lcg", "lcb", "qg", "wqb", "kvg", "wuk")]
    out_row = jnp.arange(tm, dtype=jnp.int32)
    src_row = (out_row % tt) * nb + out_row // tt
    perm = (src_row[:, None] == out_row[None, :]).astype(BF16)
    in_specs += [pl.BlockSpec((tt, pe_all), row), pl.BlockSpec((tt, pe_all), row), _full(sc.shape), _full(perm.shape)]
    out_shape = (jax.ShapeDtypeStruct((m_rows, dc), F32),
                 jax.ShapeDtypeStruct((nb, N_HEADS, n_t, kq), BF16),
                 jax.ShapeDtypeStruct((nb, n_t, kq), BF16),
                 jax.ShapeDtypeStruct((nb, n_t, kv_rank), F32),
                 jax.ShapeDtypeStruct((nb, n_t, QK_ROPE), F32),
                 jax.ShapeDtypeStruct(sc.shape, F32))
    out_specs = (pl.BlockSpec((tm, dc), row), pl.BlockSpec((nb, N_HEADS, tt, kq), lambda i: (0, 0, i, 0)),
                 pl.BlockSpec((nb, tt, kq), seq), pl.BlockSpec((nb, tt, kv_rank), seq),
                 pl.BlockSpec((nb, tt, QK_ROPE), seq), _full(sc.shape))
    scratch = [pltpu.VMEM((tm + (width - 1) * nb, dc), F32), pltpu.VMEM((kq // LANES, tm, LANES), F32)]
    return pl.pallas_call(
        kern, grid=(m_rows // tm,), in_specs=in_specs, out_specs=out_specs, out_shape=out_shape,
        scratch_shapes=scratch, compiler_params=_params(("arbitrary",)), name="odd_pre",
    )(x, w["win"], w["ccw"], w["ccb"], w["lcg"], w["lcb"], w["qg"], w["wqb"], w["kvg"], w["wuk"], cos, sin, sc, perm)


def _softmax_step(s, v16, m_sc, l_sc, acc_sc):
    m_prev = m_sc[...]
    m_new = jnp.maximum(m_prev, jnp.max(s, axis=-1, keepdims=True))
    alpha = jnp.exp(m_prev - m_new)
    p = jnp.exp(s - m_new)
    l_sc[...] = alpha * l_sc[...] + jnp.sum(p, axis=-1, keepdims=True)
    acc_sc[...] = alpha * acc_sc[...] + _dot(p.astype(BF16), v16)
    m_sc[...] = m_new


def _softmax_init(m_sc, l_sc, acc_sc):
    m_sc[...] = jnp.full(m_sc.shape, NEG_INF, F32)
    l_sc[...] = jnp.zeros(l_sc.shape, F32)
    acc_sc[...] = jnp.zeros(acc_sc.shape, F32)


def _prompt_attn_kernel(qi_ref, ki_ref, q_ref, k_ref, bias_ref, wuv_ref, o_ref, *state, kv_rank):
    m_sc, l_sc, acc_sc = state[0:N_HEADS], state[N_HEADS:2 * N_HEADS], state[2 * N_HEADS:3 * N_HEADS]
    j = pl.program_id(1)
    qi = qi_ref[j]
    ki = ki_ref[j]

    @pl.when(ki == 0)
    def _():
        for h in range(N_HEADS):
            _softmax_init(m_sc[h], l_sc[h], acc_sc[h])

    k = k_ref[...]
    v_t = k[:, 0:kv_rank].T
    diag = (ki == qi).astype(jnp.int32)

    def scores_t(h):
        return _dot_nt(k, q_ref[h]) + bias_ref[diag]

    s_next = scores_t(0)
    for h in range(N_HEADS):
        s = s_next
        if h + 1 < N_HEADS:
            s_next = scores_t(h + 1)
        m_prev = m_sc[h][...]
        m_new = jnp.maximum(m_prev, jnp.max(s, axis=0, keepdims=True))
        alpha = jnp.exp(m_prev - m_new)
        p = jnp.exp(s - m_new)
        l_sc[h][...] = alpha * l_sc[h][...] + jnp.sum(p, axis=0, keepdims=True)
        acc_sc[h][...] = alpha * acc_sc[h][...] + _dot(v_t, p.astype(BF16))
        m_sc[h][...] = m_new

    @pl.when(ki == qi)
    def _():
        for p in range(N_HEADS // 2):
            o2_t = jnp.concatenate([acc_sc[2 * p][...] / l_sc[2 * p][...],
                                    acc_sc[2 * p + 1][...] / l_sc[2 * p + 1][...]], axis=0)
            o_ref[:, p * 2 * V_DIM:(p + 1) * 2 * V_DIM] = _dot(o2_t.T.astype(BF16), wuv_ref[p])


def _prompt_attention(q, k, wuv):
    nb, n_heads, n_t, kq = q.shape
    kv_rank = wuv.shape[1] // 2
    dv = N_HEADS * V_DIM
    tile = min(n_t, ATTN_TILE)
    assert n_t % tile == 0
    nq = n_t // tile
    pairs = [(a, b) for a in range(nq) for b in range(a + 1)]
    qi_tab = jnp.asarray([p[0] for p in pairs], jnp.int32)
    ki_tab = jnp.asarray([p[1] for p in pairs], jnp.int32)
    causal = jnp.where(jnp.arange(tile)[:, None] <= jnp.arange(tile)[None, :], 0.0, NEG_INF).astype(F32)
    bias = jnp.stack([jnp.zeros((tile, tile), F32), causal])
    kern = functools.partial(_prompt_attn_kernel, kv_rank=kv_rank)
    grid_spec = pltpu.PrefetchScalarGridSpec(
        num_scalar_prefetch=2, grid=(nb, len(pairs)),
        in_specs=[pl.BlockSpec((None, n_heads, tile, kq), lambda b, j, qi, ki: (b, 0, qi[j], 0)),
                  pl.BlockSpec((None, tile, kq), lambda b, j, qi, ki: (b, ki[j], 0)),
                  pl.BlockSpec(bias.shape, lambda b, j, qi, ki: (0, 0, 0)),
                  pl.BlockSpec(wuv.shape, lambda b, j, qi, ki: (0, 0, 0))],
        out_specs=pl.BlockSpec((None, tile, dv), lambda b, j, qi, ki: (b, qi[j], 0)),
        scratch_shapes=([pltpu.VMEM((1, tile), F32)] * (2 * n_heads) + [pltpu.VMEM((kv_rank, tile), F32)] * n_heads))
    return pl.pallas_call(
        kern, grid_spec=grid_spec, out_shape=jax.ShapeDtypeStruct((nb, n_t, dv), F32),
        compiler_params=_params(("arbitrary", "arbitrary")), name="prompt_attention",
    )(qi_tab, ki_tab, q, k, bias, wuv)


def _sample_attn_kernel(pt_ref, q_ref, kn_ref, tsel_ref, wuv_ref, ckv_hbm, kpe_hbm, o_ref,
                        ck_buf, kp_buf, sem, qc_sc, m_sc, l_sc, acc_sc, *,
                        n_pages_step, layer, kv_rank, n_new, rows_per_head):
    b = pl.program_id(0)
    g = pl.program_id(1)
    n_g = pl.num_programs(1)
    step = b * n_g + g
    n_steps = pl.num_programs(0) * n_g
    slot = step % 2

    def page_copies(seq, grp, sl):
        copies = []
        for i in range(n_pages_step):
            page = pt_ref[seq, grp * n_pages_step + i]
            copies.append(pltpu.make_async_copy(
                ckv_hbm.at[page, layer], ck_buf.at[sl, pl.ds(i * PAGE_SIZE, PAGE_SIZE), :], sem.at[sl]))
            copies.append(pltpu.make_async_copy(
                kpe_hbm.at[page, layer], kp_buf.at[sl, :, pl.ds(i * PAGE_SIZE, PAGE_SIZE)], sem.at[sl]))
        return copies

    @pl.when(step == 0)
    def _():
        for c in page_copies(0, 0, 0):
            c.start()

    @pl.when(step + 1 < n_steps)
    def _():
        nxt = step + 1
        for c in page_copies(nxt // n_g, nxt % n_g, 1 - slot):
            c.start()

    @pl.when(g == 0)
    def _():
        _softmax_init(m_sc, l_sc, acc_sc)
        qc_sc[...] = _dot(q_ref[:, kv_rank:], tsel_ref[...]).astype(BF16)

    for c in page_copies(b, g, slot):
        c.wait()

    q = q_ref[...]
    ck = ck_buf[slot].astype(BF16)
    kp_t = kp_buf[slot].astype(BF16)
    s = _dot_nt(q[:, 0:kv_rank], ck) + _dot(qc_sc[...], kp_t)
    _softmax_step(s, ck, m_sc, l_sc, acc_sc)

    @pl.when(g == pl.num_programs(1) - 1)
    def _():
        kn = kn_ref[...]
        s_new = _dot_nt(q, kn)
        t_row = lax.broadcasted_iota(jnp.int32, s_new.shape, 0) % rows_per_head
        t_col = lax.broadcasted_iota(jnp.int32, s_new.shape, 1)
        s_new = jnp.where((t_col <= t_row) & (t_col < n_new), s_new, NEG_INF)
        _softmax_step(s_new, kn[:, 0:kv_rank], m_sc, l_sc, acc_sc)
        o = acc_sc[...] / l_sc[...]
        rp = rows_per_head
        for p in range(N_HEADS // 2):
            o2 = jnp.concatenate([o[2 * p * rp:(2 * p + 1) * rp], o[(2 * p + 1) * rp:(2 * p + 2) * rp]], axis=1)
            o_ref[:, p * 2 * V_DIM:(p + 1) * 2 * V_DIM] = _dot(o2.astype(BF16), wuv_ref[p])


def _sample_attention(q, k, wuv, cache_ckv, cache_kpe_t, page_table, layer):
    nb, n_heads, n_t, kq = q.shape
    kv_rank = wuv.shape[1] // 2
    dv = N_HEADS * V_DIM
    n_pages = page_table.shape[1]
    gp = min(PAGES_PER_STEP, n_pages)
    assert n_pages % gp == 0 and n_t <= SUBLANES
    rp = SUBLANES
    qb = jnp.pad(q, ((0, 0), (0, 0), (0, rp - n_t), (0, 0))).reshape(nb, n_heads * rp, kq)
    kb = jnp.pad(k, ((0, 0), (0, rp - n_t), (0, 0)))
    tsel = jnp.tile(jnp.eye(QK_ROPE, dtype=BF16), (N_HEADS, 1))
    kern = functools.partial(_sample_attn_kernel, n_pages_step=gp, layer=layer, kv_rank=kv_rank, n_new=n_t,
                             rows_per_head=rp)
    in_specs = [pl.BlockSpec((None, n_heads * rp, kq), lambda b, g, pt: (b, 0, 0)),
                pl.BlockSpec((None, rp, kq), lambda b, g, pt: (b, 0, 0)),
                pl.BlockSpec(tsel.shape, lambda b, g, pt: (0, 0)),
                pl.BlockSpec(wuv.shape, lambda b, g, pt: (0, 0, 0)),
                pl.BlockSpec(memory_space=pl.ANY), pl.BlockSpec(memory_space=pl.ANY)]
    grid_spec = pltpu.PrefetchScalarGridSpec(
        num_scalar_prefetch=1, grid=(nb, n_pages // gp), in_specs=in_specs,
        out_specs=pl.BlockSpec((None, rp, dv), lambda b, g, pt: (b, 0, 0)),
        scratch_shapes=[pltpu.VMEM((2, gp * PAGE_SIZE, kv_rank), F32), pltpu.VMEM((2, QK_ROPE, gp * PAGE_SIZE), F32),
                        pltpu.SemaphoreType.DMA((2,)),
                        pltpu.VMEM((n_heads * rp, QK_ROPE), BF16), pltpu.VMEM((n_heads * rp, 1), F32),
                        pltpu.VMEM((n_heads * rp, 1), F32), pltpu.VMEM((n_heads * rp, kv_rank), F32)])
    out = pl.pallas_call(
        kern, grid_spec=grid_spec, out_shape=jax.ShapeDtypeStruct((nb, rp, dv), F32),
        compiler_params=_params(("arbitrary", "arbitrary")), name="sample_attention",
    )(page_table, qb, kb, tsel, wuv, cache_ckv, cache_kpe_t)
    return out[:, :n_t]


def _odd_post_kernel(x_ref, yc_ref, yd_ref, wout_ref, g_ref, b_ref, o_ref, rl_sc, *, nb, tt):
    dc = yc_ref.shape[1]

    n_lane_tiles = rl_sc.shape[0]

    def body(b, carry):
        v = yd_ref[b]
        for c in range(n_lane_tiles):
            rl_sc[c, pl.ds(b, tt, stride=nb), :] = v[:, c * LANES:(c + 1) * LANES]
        return carry

    lax.fori_loop(0, nb, body, 0)
    yd = jnp.concatenate([rl_sc[c] for c in range(n_lane_tiles)], axis=1)
    m = _dot(yc_ref[...].astype(BF16), wout_ref[0:dc, :]) + _dot(yd.astype(BF16), wout_ref[dc:, :])
    o_ref[...] = _layer_norm(ALPHA * x_ref[...] + m, g_ref[...], b_ref[...])


def _odd_post(x, yc, yd, wout, g, b, *, nb, n_t):
    m_rows, d = x.shape
    dv = yd.shape[2]
    tt = _seq_tile(nb, n_t)
    tm = tt * nb
    row = lambda i: (i, 0)
    kern = functools.partial(_odd_post_kernel, nb=nb, tt=tt)
    return pl.pallas_call(
        kern, grid=(m_rows // tm,),
        in_specs=[pl.BlockSpec((tm, d), row), pl.BlockSpec((tm, yc.shape[1]), row),
                  pl.BlockSpec((nb, tt, dv), lambda i: (0, i, 0)),
                  _full(wout.shape), _full(g.shape), _full(b.shape)],
        out_specs=pl.BlockSpec((tm, d), row), out_shape=jax.ShapeDtypeStruct((m_rows, d), F32),
        scratch_shapes=[pltpu.VMEM((dv // LANES, tm, LANES), F32)],
        compiler_params=_params(("arbitrary",)), name="odd_post",
    )(x, yc, yd, wout, g, b)


def _token_tile(m_rows):
    tm = min(m_rows, TOKEN_TILE)
    assert m_rows % tm == 0
    return tm


def _router_kernel(x_ref, whi_ref, wlo_ref, br_ref, tri_ref, o_ref, cnt_ref, base_sc):
    tm = x_ref.shape[0]

    @pl.when(pl.program_id(0) == 0)
    def _():
        base_sc[...] = jnp.zeros(base_sc.shape, F32)

    x = x_ref[...]
    x_hi = x.astype(BF16)
    x_lo = (x - x_hi.astype(F32)).astype(BF16)
    logits = _dot(x_hi, whi_ref[...]) + (_dot(x_lo, whi_ref[...]) + _dot(x_hi, wlo_ref[...]))
    scores = _sigmoid(logits.T[0:N_EXPERTS, :])
    sel = scores + br_ref[...]

    def row(a, e):
        return a[e:e + 1, :]

    best_gs = None
    grp = None
    for gidx in range(N_GROUPS):
        v = [row(sel, gidx * EXPERTS_PER_GROUP + i) for i in range(EXPERTS_PER_GROUP)]
        gs = None
        for i in range(EXPERTS_PER_GROUP):
            for j in range(i + 1, EXPERTS_PER_GROUP):
                pair = v[i] + v[j]
                gs = pair if gs is None else jnp.maximum(gs, pair)
        if best_gs is None:
            best_gs, grp = gs, jnp.zeros(gs.shape, jnp.int32)
        else:
            better = gs > best_gs
            grp = jnp.where(better, gidx, grp)
            best_gs = jnp.where(better, gs, best_gs)

    def pick(a, i):
        out = row(a, i)
        for gidx in range(1, N_GROUPS):
            out = jnp.where(grp == gidx, row(a, gidx * EXPERTS_PER_GROUP + i), out)
        return out

    cand = [pick(sel, i) for i in range(EXPERTS_PER_GROUP)]
    cand_score = [pick(scores, i) for i in range(EXPERTS_PER_GROUP)]

    def argmax_first(vals, exclude=None):
        best, idx = None, None
        for i, v in enumerate(vals):
            if exclude is not None:
                v = jnp.where(exclude == i, NEG_INF, v)
            if best is None:
                best, idx = v, jnp.zeros(v.shape, jnp.int32)
            else:
                better = v > best
                idx = jnp.where(better, i, idx)
                best = jnp.where(better, v, best)
        return idx

    loc1 = argmax_first(cand)
    loc2 = argmax_first(cand, exclude=loc1)

    def take(vals, idx):
        out = vals[0]
        for i in range(1, len(vals)):
            out = jnp.where(idx == i, vals[i], out)
        return out

    g1 = take(cand_score, loc1)
    g2 = take(cand_score, loc2)
    gsum = g1 + g2
    e1 = grp * EXPERTS_PER_GROUP + loc1
    e2 = grp * EXPERTS_PER_GROUP + loc2

    e_iota = lax.broadcasted_iota(jnp.int32, (N_EXPERTS, tm), 0)
    hit1 = e_iota == e1
    hit2 = e_iota == e2
    both = jnp.where(hit1 | hit2, 1.0, 0.0)
    before = _dot(both.astype(BF16), tri_ref[...]) + base_sc[...]
    r1 = jnp.sum(jnp.where(hit1, before, 0.0), axis=0, keepdims=True)
    r2 = jnp.sum(jnp.where(hit2, before, 0.0), axis=0, keepdims=True)
    base_sc[...] = base_sc[...] + jnp.sum(both, axis=1, keepdims=True)
    cnt_ref[...] = jnp.broadcast_to(base_sc[...], cnt_ref.shape)

    zero = jnp.zeros((1, tm), F32)
    o_ref[...] = jnp.concatenate([e1.astype(F32), e2.astype(F32), g1 / gsum, g2 / gsum, r1, r2, zero, zero], axis=0)


def _router(x, whi, wlo, br):
    m_rows, d = x.shape
    tm = _token_tile(m_rows)
    tri = jnp.triu(jnp.ones((tm, tm), BF16), k=1)
    return pl.pallas_call(
        _router_kernel, grid=(m_rows // tm,),
        in_specs=[pl.BlockSpec((tm, d), lambda i: (i, 0)), _full(whi.shape), _full(wlo.shape), _full(br.shape),
                  _full(tri.shape)],
        out_specs=(pl.BlockSpec((SUBLANES, tm), lambda i: (0, i)), _full((N_EXPERTS, LANES))),
        out_shape=(jax.ShapeDtypeStruct((SUBLANES, m_rows), F32), jax.ShapeDtypeStruct((N_EXPERTS, LANES), F32)),
        scratch_shapes=[pltpu.VMEM((N_EXPERTS, 1), F32)],
        compiler_params=_params(("arbitrary",)), name="router",
    )(x, whi, wlo, br, tri)


def _row_copy(src_ref, src_row, dst_ref, dst_row, sem):
    return pltpu.make_async_copy(src_ref.at[pl.ds(src_row, 1), :], dst_ref.at[pl.ds(dst_row, 1), :], sem)


def _expert_kernel(te_ref, src_ref, src_next_ref, dst_prev_ref, dst_ref, x_hbm, win_ref, wout_ref, y_hbm,
                   xg, yo, gsem, ssem):
    del te_ref
    tile = xg.shape[1]
    de = wout_ref.shape[0]
    i = pl.program_id(0)
    last = pl.num_programs(0) - 1
    slot = i % 2

    def gather(idx_ref, sl):
        for r in range(tile):
            _row_copy(x_hbm, idx_ref[0, 0, r], xg.at[sl], r, gsem.at[sl]).start(priority=r % 2)

    def scatter(idx_ref, sl):
        for r in range(tile):
            _row_copy(yo.at[sl], r, y_hbm, idx_ref[0, 0, r], ssem.at[sl]).start(priority=(r + 1) % 2)

    def wait_gather(sl):
        pltpu.make_async_copy(x_hbm.at[pl.ds(0, tile), :], xg.at[sl], gsem.at[sl]).wait()

    def wait_scatter(sl):
        pltpu.make_async_copy(yo.at[sl], y_hbm.at[pl.ds(0, tile), :], ssem.at[sl]).wait()

    @pl.when(i == 0)
    def _():
        yo[...] = jnp.zeros(yo.shape, F32)
        gather(src_ref, 0)

    wait_gather(slot)

    @pl.when(i > 0)
    def _():
        wait_scatter(slot)

    gather(src_next_ref, 1 - slot)
    scatter(dst_prev_ref, 1 - slot)
    h = _dot(xg[slot].astype(BF16), win_ref[...])
    hg = h[:, 0:de]
    act = hg * _sigmoid(hg) * h[:, de:2 * de]
    yo[slot] = _dot(act.astype(BF16), wout_ref[...])

    @pl.when(i == last)
    def _():
        scatter(dst_ref, slot)
        wait_scatter(1 - slot)
        wait_scatter(slot)
        wait_gather(1 - slot)


def _experts(x, tile_expert, src_tiles, dst_tiles, w_in, w_out, layer):
    m_rows, d = x.shape
    n_tiles, _, tile = src_tiles.shape
    de2 = w_in.shape[3]
    kern = _expert_kernel
    smem = lambda fn: pl.BlockSpec((1, 1, tile), fn, memory_space=pltpu.SMEM)
    grid_spec = pltpu.PrefetchScalarGridSpec(
        num_scalar_prefetch=1, grid=(n_tiles,),
        in_specs=[smem(lambda i, te: (i, 0, 0)),
                  smem(lambda i, te: (jnp.minimum(i + 1, n_tiles - 1), 0, 0)),
                  smem(lambda i, te: (i, 0, 0)),
                  smem(lambda i, te: (i + 1, 0, 0)),
                  pl.BlockSpec(memory_space=pl.ANY),
                  pl.BlockSpec((None, None, d, de2), lambda i, te: (layer, te[i], 0, 0)),
                  pl.BlockSpec((None, None, de2 // 2, d), lambda i, te: (layer, te[i], 0, 0))],
        out_specs=pl.BlockSpec(memory_space=pl.ANY),
        scratch_shapes=[pltpu.VMEM((2, tile, d), F32), pltpu.VMEM((2, tile, d), F32),
                        pltpu.SemaphoreType.DMA((2,)), pltpu.SemaphoreType.DMA((2,))])
    return pl.pallas_call(
        kern, grid_spec=grid_spec, out_shape=jax.ShapeDtypeStruct(((n_tiles + 1) * tile, d), F32),
        compiler_params=_params(("arbitrary",)), name="moe_experts",
    )(tile_expert, src_tiles, src_tiles, dst_tiles, dst_tiles, x, w_in, w_out)


def _combine_kernel(x_ref, gate_ref, y1_ref, y2_ref, g_ref, b_ref, o_ref):
    gate = gate_ref[...]
    y = gate[:, 0:1] * y1_ref[...] + gate[:, 1:2] * y2_ref[...]
    o_ref[...] = _layer_norm(ALPHA * x_ref[...] + y, g_ref[...], b_ref[...])


def _combine(x, gates, y, g, b):
    m_rows, d = x.shape
    tm = _token_tile(m_rows)
    n = m_rows // tm
    row = lambda i: (i, 0)
    return pl.pallas_call(
        _combine_kernel, grid=(n,),
        in_specs=[pl.BlockSpec((tm, d), row), pl.BlockSpec((tm, 2), row),
                  pl.BlockSpec((tm, d), row), pl.BlockSpec((tm, d), lambda i: (n + i, 0)),
                  _full(g.shape), _full(b.shape)],
        out_specs=pl.BlockSpec((tm, d), row), out_shape=jax.ShapeDtypeStruct((m_rows, d), F32),
        compiler_params=_params(("parallel",)), name="moe_combine",
    )(x, gates, y, y, g, b)


def _moe_layer(x, rw, w_in, w_out, layer, g, b, expert_tile):
    m_rows, _ = x.shape
    route, counts = _router(x, rw["whi"], rw["wlo"], rw["br"])
    eids = route[0:2].astype(jnp.int32)
    gates = route[2:4].T
    rank = route[4:6].astype(jnp.int32)
    counts = counts[:, 0].astype(jnp.int32)
    padded = (counts + expert_tile - 1) // expert_tile * expert_tile
    pad_ends = jnp.cumsum(padded)
    pad_starts = pad_ends - padded
    e_range = jnp.arange(N_EXPERTS, dtype=jnp.int32)
    start_of = jnp.sum(jnp.where(eids[..., None] == e_range, pad_starts, 0), axis=-1)
    dest = start_of + rank
    n_tiles = -(-(2 * m_rows + N_EXPERTS * (expert_tile - 1)) // expert_tile)
    n_slots = n_tiles * expert_tile
    tile_start = jnp.arange(n_tiles, dtype=jnp.int32) * expert_tile
    tile_expert = jnp.minimum(jnp.sum((tile_start[:, None] >= pad_ends[None, :]).astype(jnp.int32), axis=1),
                              N_EXPERTS - 1)
    n_unused = n_slots - 2 * m_rows
    pad_cnt = padded - counts
    pad_cum = jnp.cumsum(pad_cnt)
    j = jnp.arange(n_unused, dtype=jnp.int32)
    in_or_after = j[:, None] >= pad_cum[None, :]
    before = jnp.sum(jnp.where(in_or_after, pad_cnt, 0), axis=1)
    owner = jnp.sum(in_or_after.astype(jnp.int32), axis=1)
    first_unused = pad_ends - pad_cnt
    base = (jnp.sum(jnp.where(owner[:, None] == e_range, first_unused, 0), axis=1)
            + jnp.where(owner == N_EXPERTS, pad_ends[-1], 0))
    unused_slot = base + (j - before)
    keys = jnp.concatenate([dest.reshape(-1), unused_slot])
    _, code = lax.sort((keys, jnp.arange(n_slots, dtype=jnp.int32)), num_keys=1)
    src_tiles = jnp.where(code >= 2 * m_rows, 0, code % m_rows).reshape(n_tiles, 1, expert_tile)
    dst_tiles = jnp.concatenate([n_slots + jnp.arange(expert_tile, dtype=jnp.int32), code])
    dst_tiles = dst_tiles.reshape(n_tiles + 1, 1, expert_tile)
    y = _experts(x, tile_expert, src_tiles, dst_tiles, w_in, w_out, layer)
    return _combine(x, gates, y, g, b)


def _block_diag(w):
    h, n, _ = w.shape
    eye = jnp.eye(h, dtype=w.dtype)
    return (eye[:, None, :, None] * w[:, :, None, :]).reshape(h * n, h * n)


def _pair_blocks(w):
    h, a, b = w.shape
    w = w.reshape(h // 2, 2, a, b)
    eye = jnp.eye(2, dtype=w.dtype)
    return (eye[None, :, None, :, None] * w[:, :, :, None, :]).reshape(h // 2, 2 * a, 2 * b)


def _time_major(a):
    b, t, c = a.shape
    return a.transpose(1, 0, 2).reshape(t * b, c)


def _batch_major(a, nb):
    return a.reshape(-1, nb, a.shape[-1]).transpose(1, 0, 2)


def _rope_tables(pos):
    half = QK_ROPE // 2
    inv = ROPE_THETA ** (-jnp.arange(half, dtype=F32) / half)
    ang = pos.astype(F32)[:, None] * inv
    cos, sin = jnp.cos(ang), jnp.sin(ang)
    cos_t = jnp.tile(jnp.concatenate([cos, cos], axis=1), (1, N_HEADS))
    sin_t = jnp.tile(jnp.concatenate([-sin, sin], axis=1), (1, N_HEADS))
    return cos_t, sin_t


def kernel(x_prompt, x_sample, state_conv_a, state_conv_b, state_rglru_h, state_conv_c, cache_ckv, cache_kpe, page_table, w_in_even, conv_a_w, conv_b_w, conv_b_b, lru_wa, lru_ba, lru_wx, lru_bx, lru_lambda, w_out_even, w_in_odd, conv_c_w, conv_c_b, ln_c_g, ln_c_b, q_norm_g, w_q_b, kv_norm_g, w_uk, w_uv, w_out_odd, ln_mix_g, ln_mix_b, ln_ffn_g, ln_ffn_b, w_router, b_router, w_exp_in, w_exp_out):
    bp, n_tp, d = x_prompt.shape
    bs, n_ts, _ = x_sample.shape
    past_len = page_table.shape[1] * PAGE_SIZE
    dc = conv_c_w.shape[2]
    q_rank = q_norm_g.shape[1]
    kv_rank = kv_norm_g.shape[1]

    xp = _time_major(x_prompt)
    xs = _time_major(x_sample)
    row2 = lambda v: v.reshape(1, -1)

    wr = jnp.pad(w_router, ((0, 0), (0, LANES - N_EXPERTS)))
    wr_hi = wr.astype(BF16)
    router_w = {"whi": wr_hi, "wlo": (wr - wr_hi.astype(F32)).astype(BF16), "br": b_router.reshape(N_EXPERTS, 1)}
    rope_p = _rope_tables(jnp.arange(n_tp))
    rope_s = _rope_tables(past_len + jnp.arange(n_ts))
    we_in, we_out = w_exp_in.astype(BF16), w_exp_out.astype(BF16)
    cache_kpe_t = jnp.swapaxes(cache_kpe, 2, 3)

    outs = {k: [] for k in ("ca_p", "ca_s", "cb_p", "cb_s", "h_p", "h_s", "cc_p", "cc_s", "ckv_p", "ckv_s", "kpe_p", "kpe_s")}
    for l in range(DEPTH):
        j = l // 2
        lg, lb = row2(ln_mix_g[l]), row2(ln_mix_b[l])
        if l % 2 == 0:
            wa_bd, wx_bd = _block_diag(lru_wa[j]), _block_diag(lru_wx[j])
            hc = wa_bd.shape[0] // 2
            wg = jnp.stack([jnp.concatenate([wa_bd[s * hc:(s + 1) * hc, s * hc:(s + 1) * hc],
                                             wx_bd[s * hc:(s + 1) * hc, s * hc:(s + 1) * hc]], axis=1)
                            for s in range(2)]).astype(BF16)
            w = {"win": w_in_even[j].astype(BF16), "caw": conv_a_w[j], "cbw": conv_b_w[j], "cbb": row2(conv_b_b[j]),
                 "wg": wg, "ba": row2(lru_ba[j]), "bx": row2(lru_bx[j]), "lam": row2(lru_lambda[j]),
                 "wout": w_out_even[j].astype(BF16), "g": lg, "b": lb}
            da = conv_a_w.shape[2]
            db = conv_b_w.shape[2]
            xp, a1, b1, h1 = _even_layer(xp, w, jnp.zeros((2 * bp, da), F32), jnp.zeros((3 * bp, db), F32),
                                         jnp.zeros((bp, db), F32), nb=bp, n_t=n_tp)
            xs, a2, b2, h2 = _even_layer(xs, w, _time_major(state_conv_a[j]), _time_major(state_conv_b[j]),
                                         state_rglru_h[j], nb=bs, n_t=n_ts)
            outs["ca_p"].append(_batch_major(a1, bp)); outs["ca_s"].append(_batch_major(a2, bs))
            outs["cb_p"].append(_batch_major(b1, bp)); outs["cb_s"].append(_batch_major(b2, bs))
            outs["h_p"].append(h1); outs["h_s"].append(h2)
        else:
            nope_all = N_HEADS * QK_NOPE
            wq = w_q_b[j].reshape(q_rank, N_HEADS, QK_NOPE + QK_ROPE)
            wqb = jnp.concatenate([wq[:, :, :QK_NOPE].reshape(q_rank, nope_all),
                                   wq[:, :, QK_NOPE:].reshape(q_rank, N_HEADS * QK_ROPE)], axis=1).astype(BF16)
            o_pe = 2 * dc + q_rank + kv_rank
            win = jnp.concatenate([w_in_odd[j][:, :o_pe], jnp.tile(w_in_odd[j][:, o_pe:], (1, N_HEADS))], axis=1).astype(BF16)
            wuk = _pair_blocks(w_uk[j].transpose(1, 2, 0)).astype(BF16)
            wuv = _pair_blocks(w_uv[j].transpose(1, 0, 2)).astype(BF16)
            w = {"win": win, "ccw": conv_c_w[j], "ccb": row2(conv_c_b[j]), "lcg": row2(ln_c_g[j]), "lcb": row2(ln_c_b[j]),
                 "qg": row2(q_norm_g[j]), "wqb": wqb, "kvg": row2(kv_norm_g[j]), "wuk": wuk}
            width = conv_c_w.shape[1]
            wout = w_out_odd[j].astype(BF16)
            ycp, qp, kp, ckv1, kpe1, c1 = _odd_pre(xp, w, jnp.zeros(((width - 1) * bp, dc), F32), *rope_p, nb=bp, n_t=n_tp)
            ycs, qs, ks, ckv2, kpe2, c2 = _odd_pre(xs, w, _time_major(state_conv_c[j]), *rope_s, nb=bs, n_t=n_ts)
            ydp = _prompt_attention(qp, kp, wuv)
            yds = _sample_attention(qs, ks, wuv, cache_ckv, cache_kpe_t, page_table, j)
            xp = _odd_post(xp, ycp, ydp, wout, lg, lb, nb=bp, n_t=n_tp)
            xs = _odd_post(xs, ycs, yds, wout, lg, lb, nb=bs, n_t=n_ts)
            outs["cc_p"].append(_batch_major(c1, bp)); outs["cc_s"].append(_batch_major(c2, bs))
            outs["ckv_p"].append(ckv1); outs["ckv_s"].append(ckv2)
            outs["kpe_p"].append(kpe1); outs["kpe_s"].append(kpe2)
        fg, fb = row2(ln_ffn_g[l]), row2(ln_ffn_b[l])
        xp = _moe_layer(xp, router_w, we_in, we_out, l, fg, fb, expert_tile=512)
        xs = _moe_layer(xs, router_w, we_in, we_out, l, fg, fb, expert_tile=64)

    y_prompt = _batch_major(xp, bp)
    y_sample = _batch_major(xs, bs)
    st = lambda k: jnp.stack(outs[k])
    st1 = lambda k: jnp.stack(outs[k], axis=1)
    return (y_prompt, y_sample, st("ca_p"), st("ca_s"), st("cb_p"), st("cb_s"), st("h_p"), st("h_s"),
            st("cc_p"), st("cc_s"), st1("ckv_p"), st1("ckv_s"), st1("kpe_p"), st1("kpe_s"))
```

```python
import functools

import jax
import jax.numpy as jnp
from jax import lax
from jax.experimental import pallas as pl
from jax.experimental.pallas import tpu as pltpu

F32 = jnp.float32
BF16 = jnp.bfloat16

DEPTH = 4
N_HEADS = 8
QK_NOPE = 64
QK_ROPE = 32
V_DIM = 64
N_EXPERTS = 16
N_GROUPS = 4
EXPERTS_PER_GROUP = 4
PAIRS_PER_GROUP = 6
N_BUCKETS = N_GROUPS * PAIRS_PER_GROUP
PAIR_FIRST = (0, 0, 0, 1, 1, 2)
PAIR_SECOND = (1, 2, 3, 2, 3, 3)
LRU_C = 8.0
ROPE_THETA = 10000.0
ATTN_SCALE = (QK_NOPE + QK_ROPE) ** -0.5
ALPHA = (2 * DEPTH) ** 0.25
PAGE_SIZE = 128

SUBLANES = 8
BF16_SUBLANES = 16
LANES = 128
VMEM_LIMIT_BYTES = 56 * 1024 * 1024

TIME_STEPS_PER_TILE = 32
TOKEN_TILE = 512
ATTN_TILE = 512
PAGES_PER_STEP = 64
NEG_INF = float("-inf")


def _params(semantics):
    return pltpu.CompilerParams(dimension_semantics=semantics, vmem_limit_bytes=VMEM_LIMIT_BYTES)


def _full(shape):
    nd = len(shape)
    return pl.BlockSpec(shape, lambda *_: (0,) * nd)


def _layer_norm(x, g, b, eps=1e-5):
    mu = jnp.mean(x, axis=-1, keepdims=True)
    xc = x - mu
    var = jnp.mean(xc * xc, axis=-1, keepdims=True)
    return xc * lax.rsqrt(var + eps) * g + b


def _rms_norm(x, g, eps=1e-6):
    return x * lax.rsqrt(jnp.mean(x * x, axis=-1, keepdims=True) + eps) * g


def _sigmoid(x):
    return 1.0 / (1.0 + jnp.exp(-x))


def _dot(a, b):
    return jnp.dot(a, b, preferred_element_type=F32)


def _dot_nt(a, b):
    return lax.dot_general(a, b, (((1,), (1,)), ((), ())), preferred_element_type=F32)


def _seq_tile(nb, n_t):
    tt = min(n_t, TIME_STEPS_PER_TILE)
    assert n_t % tt == 0 and nb % SUBLANES == 0
    return tt


def _even_kernel(x_ref, win_ref, caw_ref, cbw_ref, cbb_ref, wg_ref, ba_ref, bx_ref, lam_ref, wout_ref,
                 g_ref, b_ref, sa_ref, sb_ref, h0_ref,
                 o_ref, na_ref, nb_ref, hl_ref,
                 ua_ext, vb_ext, h_sc, hs_sc, *, nb, tt):
    tm = tt * nb
    dh = caw_ref.shape[1]
    half = dh // 2

    @pl.when(pl.program_id(0) == 0)
    def _():
        ua_ext[0:2 * nb, :] = sa_ref[...]
        vb_ext[0:3 * nb, :] = sb_ref[...]
        h_sc[...] = h0_ref[...]

    x = x_ref[...]
    xb = x.astype(BF16)

    def proj(j):
        return _dot(xb, win_ref[:, j * dh:(j + 1) * dh])

    ua_ext[2 * nb:2 * nb + tm, :] = proj(1) * proj(2)
    caw = caw_ref[...]
    conv_a = (caw[0:1] * ua_ext[0:tm, :] + caw[1:2] * ua_ext[nb:nb + tm, :]
              + caw[2:3] * ua_ext[2 * nb:2 * nb + tm, :])
    y_a = proj(0) * conv_a
    tail_a = ua_ext[tm:tm + 2 * nb, :]
    na_ref[...] = tail_a
    ua_ext[0:2 * nb, :] = tail_a

    vb_ext[3 * nb:3 * nb + tm, :] = proj(4)
    cbw = cbw_ref[...]
    u_b = (cbb_ref[...] + cbw[0:1] * vb_ext[0:tm, :] + cbw[1:2] * vb_ext[nb:nb + tm, :]
           + cbw[2:3] * vb_ext[2 * nb:2 * nb + tm, :] + cbw[3:4] * vb_ext[3 * nb:3 * nb + tm, :])
    tail_b = vb_ext[tm:tm + 3 * nb, :]
    nb_ref[...] = tail_b
    vb_ext[0:3 * nb, :] = tail_b

    ub16 = u_b.astype(BF16)
    gk0 = _dot(ub16[:, :half], wg_ref[0])
    gk1 = _dot(ub16[:, half:], wg_ref[1])
    r = _sigmoid(jnp.concatenate([gk0[:, :half], gk1[:, :half]], axis=1) + ba_ref[...])
    ig = _sigmoid(jnp.concatenate([gk0[:, half:], gk1[:, half:]], axis=1) + bx_ref[...])
    nlam = -lam_ref[...]
    softplus = jnp.maximum(nlam, 0.0) + jnp.log(1.0 + jnp.exp(-jnp.abs(nlam)))
    log_a = (-LRU_C * r) * softplus
    a = jnp.exp(log_a)
    bterm = jnp.sqrt(1.0 - a * a) * (ig * u_b)

    h = h_sc[...]
    for t in range(tt):
        h = a[t * nb:(t + 1) * nb, :] * h + bterm[t * nb:(t + 1) * nb, :]
        hs_sc[t * nb:(t + 1) * nb, :] = h
    h_sc[...] = h
    hl_ref[...] = h
    y_b = jax.nn.gelu(proj(3), approximate=True) * hs_sc[...]

    m = _dot(y_a.astype(BF16), wout_ref[0:dh, :]) + _dot(y_b.astype(BF16), wout_ref[dh:2 * dh, :])
    o_ref[...] = _layer_norm(ALPHA * x + m, g_ref[...], b_ref[...])


def _even_layer(x, w, sa, sb, h0, *, nb, n_t):
    m_rows, d = x.shape
    dh = w["caw"].shape[1]
    tt = _seq_tile(nb, n_t)
    tm = tt * nb
    kern = functools.partial(_even_kernel, nb=nb, tt=tt)
    row = lambda i: (i, 0)
    in_specs = [pl.BlockSpec((tm, d), row), _full(w["win"].shape), _full(w["caw"].shape), _full(w["cbw"].shape),
                _full(w["cbb"].shape), _full(w["wg"].shape), _full(w["ba"].shape), _full(w["bx"].shape),
                _full(w["lam"].shape), _full(w["wout"].shape), _full(w["g"].shape), _full(w["b"].shape),
                _full(sa.shape), _full(sb.shape), _full(h0.shape)]
    out_shape = (jax.ShapeDtypeStruct((m_rows, d), F32), jax.ShapeDtypeStruct(sa.shape, F32),
                 jax.ShapeDtypeStruct(sb.shape, F32), jax.ShapeDtypeStruct(h0.shape, F32))
    out_specs = (pl.BlockSpec((tm, d), row), _full(sa.shape), _full(sb.shape), _full(h0.shape))
    scratch = [pltpu.VMEM((tm + 2 * nb, dh), F32), pltpu.VMEM((tm + 3 * nb, dh), F32),
               pltpu.VMEM((nb, dh), F32), pltpu.VMEM((tm, dh), F32)]
    return pl.pallas_call(
        kern, grid=(m_rows // tm,), in_specs=in_specs, out_specs=out_specs, out_shape=out_shape,
        scratch_shapes=scratch, compiler_params=_params(("arbitrary",)), name="even_mixer",
    )(x, w["win"], w["caw"], w["cbw"], w["cbb"], w["wg"], w["ba"], w["bx"], w["lam"], w["wout"], w["g"], w["b"],
      sa, sb, h0)


def _odd_pre_kernel(x_ref, win_ref, ccw_ref, ccb_ref, lcg_ref, lcb_ref, qg_ref, wqb_ref, kvg_ref, wuk_ref,
                    cos_ref, sin_ref, sc_ref, perm_ref,
                    yc_ref, q_ref, k_ref, ckv_ref, kpe_ref, nc_ref,
                    c_ext, rl_sc, *, nb, tt, width):
    tm = tt * nb
    dc = ccw_ref.shape[1]
    q_rank = qg_ref.shape[1]
    kv_rank = kvg_ref.shape[1]
    pe_all = N_HEADS * QK_ROPE
    hist = (width - 1) * nb

    @pl.when(pl.program_id(0) == 0)
    def _():
        c_ext[0:hist, :] = sc_ref[...]

    xb = x_ref[...].astype(BF16)
    o_q = 2 * dc
    o_kv = o_q + q_rank
    o_pe = o_kv + kv_rank

    glu = _dot(xb, win_ref[:, 0:dc]) * _sigmoid(_dot(xb, win_ref[:, dc:2 * dc]))
    c_ext[hist:hist + tm, :] = glu
    ccw = ccw_ref[...]
    u_c = ccb_ref[...] + ccw[0:1] * c_ext[0:tm, :]
    for k in range(1, width):
        u_c = u_c + ccw[k:k + 1] * c_ext[k * nb:k * nb + tm, :]
    tail = c_ext[tm:tm + hist, :]
    nc_ref[...] = tail
    c_ext[0:hist, :] = tail
    ln = _layer_norm(u_c, lcg_ref[...], lcb_ref[...])
    yc_ref[...] = ln * _sigmoid(ln)

    def to_sequences(val, store):
        n_lane_tiles = val.shape[1] // LANES
        for c in range(n_lane_tiles):
            rl_sc[c] = val[:, c * LANES:(c + 1) * LANES]

        def body(b, carry):
            store(b, jnp.concatenate([rl_sc[c, pl.ds(b, tt, stride=nb), :] for c in range(n_lane_tiles)], axis=1))
            return carry

        lax.fori_loop(0, nb, body, 0)

    cos = jnp.broadcast_to(cos_ref[...][:, None, :], (tt, nb, pe_all)).reshape(tm, pe_all)
    sin = jnp.broadcast_to(sin_ref[...][:, None, :], (tt, nb, pe_all)).reshape(tm, pe_all)
    lane = lax.broadcasted_iota(jnp.int32, (1, pe_all), 1)
    first_half = (lane % QK_ROPE) < (QK_ROPE // 2)
    head_of_lane = lane // QK_ROPE

    def rope(v):
        swapped = jnp.where(first_half, pltpu.roll(v, pe_all - QK_ROPE // 2, 1), pltpu.roll(v, QK_ROPE // 2, 1))
        return v * cos + swapped * sin

    permute_on_mxu = tt % BF16_SUBLANES == 0

    def to_sequences_bf16(val, store):
        if not permute_on_mxu:
            to_sequences(val, lambda b, v: store(b, v.astype(BF16)))
            return
        pv = _dot(perm_ref[...], val.astype(BF16)).astype(BF16)
        for b in range(nb):
            store(b, pv[b * tt:(b + 1) * tt])

    ckv = _rms_norm(_dot(xb, win_ref[:, o_kv:o_kv + kv_rank]), kvg_ref[...])
    kpe = rope(_dot(xb, win_ref[:, o_pe:o_pe + pe_all]))
    k_all = jnp.concatenate([ckv, kpe], axis=1)

    def store_k32(b, v):
        ckv_ref[b] = v[:, 0:kv_rank]
        kpe_ref[b] = v[:, kv_rank:kv_rank + QK_ROPE]

    def store_k16(b, v):
        k_ref[b] = v

    def store_k(b, v):
        store_k32(b, v)
        store_k16(b, v.astype(BF16))

    if permute_on_mxu:
        to_sequences(k_all, store_k32)
        to_sequences_bf16(k_all, store_k16)
    else:
        to_sequences(k_all, store_k)

    qn = _rms_norm(_dot(xb, win_ref[:, o_q:o_q + q_rank]), qg_ref[...]).astype(BF16)
    nope_all = N_HEADS * QK_NOPE
    q_nope = (_dot(qn, wqb_ref[:, 0:nope_all]) * ATTN_SCALE).astype(BF16)
    q_pe = rope(_dot(qn, wqb_ref[:, nope_all:nope_all + pe_all]) * ATTN_SCALE)
    for p in range(N_HEADS // 2):
        lat2 = _dot(q_nope[:, p * 2 * QK_NOPE:(p + 1) * 2 * QK_NOPE], wuk_ref[p])
        for s in range(2):
            h = 2 * p + s

            def store_q(b, v, h=h):
                q_ref[b, h] = v

            to_sequences_bf16(jnp.concatenate([lat2[:, s * kv_rank:(s + 1) * kv_rank],
                                               jnp.where(head_of_lane == h, q_pe, 0.0)], axis=1), store_q)


def _odd_pre(x, w, sc, cos, sin, *, nb, n_t):
    m_rows, d = x.shape
    width, dc = w["ccw"].shape
    kv_rank = w["kvg"].shape[1]
    pe_all = N_HEADS * QK_ROPE
    kq = kv_rank + pe_all
    tt = _seq_tile(nb, n_t)
    tm = tt * nb
    kern = functools.partial(_odd_pre_kernel, nb=nb, tt=tt, width=width)
    row = lambda i: (i, 0)
    seq = lambda i: (0, i, 0)
    in_specs = [pl.BlockSpec((tm, d), row)] + [_full(w[k].shape) for k in
                                                ("win", "ccw", "ccb", "lcg", "lcb", "qg", "wqb", "kvg", "wuk")]
    out_row = jnp.arange(tm, dtype=jnp.int32)
    src_row = (out_row % tt) * nb + out_row // tt
    perm = (src_row[:, None] == out_row[None, :]).astype(BF16)
    in_specs += [pl.BlockSpec((tt, pe_all), row), pl.BlockSpec((tt, pe_all), row), _full(sc.shape), _full(perm.shape)]
    out_shape = (jax.ShapeDtypeStruct((m_rows, dc), F32),
                 jax.ShapeDtypeStruct((nb, N_HEADS, n_t, kq), BF16),
                 jax.ShapeDtypeStruct((nb, n_t, kq), BF16),
                 jax.ShapeDtypeStruct((nb, n_t, kv_rank), F32),
                 jax.ShapeDtypeStruct((nb, n_t, QK_ROPE), F32),
                 jax.ShapeDtypeStruct(sc.shape, F32))
    out_specs = (pl.BlockSpec((tm, dc), row), pl.BlockSpec((nb, N_HEADS, tt, kq), lambda i: (0, 0, i, 0)),
                 pl.BlockSpec((nb, tt, kq), seq), pl.BlockSpec((nb, tt, kv_rank), seq),
                 pl.BlockSpec((nb, tt, QK_ROPE), seq), _full(sc.shape))
    scratch = [pltpu.VMEM((tm + (width - 1) * nb, dc), F32), pltpu.VMEM((kq // LANES, tm, LANES), F32)]
    return pl.pallas_call(
        kern, grid=(m_rows // tm,), in_specs=in_specs, out_specs=out_specs, out_shape=out_shape,
        scratch_shapes=scratch, compiler_params=_params(("arbitrary",)), name="odd_pre",
    )(x, w["win"], w["ccw"], w["ccb"], w["lcg"], w["lcb"], w["qg"], w["wqb"], w["kvg"], w["wuk"], cos, sin, sc, perm)


def _softmax_step(s, v16, m_sc, l_sc, acc_sc):
    m_prev = m_sc[...]
    m_new = jnp.maximum(m_prev, jnp.max(s, axis=-1, keepdims=True))
    alpha = jnp.exp(m_prev - m_new)
    p = jnp.exp(s - m_new)
    l_sc[...] = alpha * l_sc[...] + jnp.sum(p, axis=-1, keepdims=True)
    acc_sc[...] = alpha * acc_sc[...] + _dot(p.astype(BF16), v16)
    m_sc[...] = m_new


def _softmax_init(m_sc, l_sc, acc_sc):
    m_sc[...] = jnp.full(m_sc.shape, NEG_INF, F32)
    l_sc[...] = jnp.zeros(l_sc.shape, F32)
    acc_sc[...] = jnp.zeros(acc_sc.shape, F32)


def _prompt_attn_kernel(qi_ref, ki_ref, q_ref, k_ref, bias_ref, wuv_ref, o_ref, *state, kv_rank):
    m_sc, l_sc, acc_sc = state[0:N_HEADS], state[N_HEADS:2 * N_HEADS], state[2 * N_HEADS:3 * N_HEADS]
    j = pl.program_id(1)
    qi = qi_ref[j]
    ki = ki_ref[j]

    @pl.when(ki == 0)
    def _():
        for h in range(N_HEADS):
            _softmax_init(m_sc[h], l_sc[h], acc_sc[h])

    k = k_ref[...]
    v_t = k[:, 0:kv_rank].T
    diag = (ki == qi).astype(jnp.int32)

    def scores_t(h):
        return _dot_nt(k, q_ref[h]) + bias_ref[diag]

    s_next = scores_t(0)
    for h in range(N_HEADS):
        s = s_next
        if h + 1 < N_HEADS:
            s_next = scores_t(h + 1)
        m_prev = m_sc[h][...]
        m_new = jnp.maximum(m_prev, jnp.max(s, axis=0, keepdims=True))
        alpha = jnp.exp(m_prev - m_new)
        p = jnp.exp(s - m_new)
        l_sc[h][...] = alpha * l_sc[h][...] + jnp.sum(p, axis=0, keepdims=True)
        acc_sc[h][...] = alpha * acc_sc[h][...] + _dot(v_t, p.astype(BF16))
        m_sc[h][...] = m_new

    @pl.when(ki == qi)
    def _():
        for p in range(N_HEADS // 2):
            o2_t = jnp.concatenate([acc_sc[2 * p][...] / l_sc[2 * p][...],
                                    acc_sc[2 * p + 1][...] / l_sc[2 * p + 1][...]], axis=0)
            o_ref[:, p * 2 * V_DIM:(p + 1) * 2 * V_DIM] = _dot(o2_t.T.astype(BF16), wuv_ref[p])


def _prompt_attention(q, k, wuv):
    nb, n_heads, n_t, kq = q.shape
    kv_rank = wuv.shape[1] // 2
    dv = N_HEADS * V_DIM
    tile = min(n_t, ATTN_TILE)
    assert n_t % tile == 0
    nq = n_t // tile
    pairs = [(a, b) for a in range(nq) for b in range(a + 1)]
    qi_tab = jnp.asarray([p[0] for p in pairs], jnp.int32)
    ki_tab = jnp.asarray([p[1] for p in pairs], jnp.int32)
    causal = jnp.where(jnp.arange(tile)[:, None] <= jnp.arange(tile)[None, :], 0.0, NEG_INF).astype(F32)
    bias = jnp.stack([jnp.zeros((tile, tile), F32), causal])
    kern = functools.partial(_prompt_attn_kernel, kv_rank=kv_rank)
    grid_spec = pltpu.PrefetchScalarGridSpec(
        num_scalar_prefetch=2, grid=(nb, len(pairs)),
        in_specs=[pl.BlockSpec((None, n_heads, tile, kq), lambda b, j, qi, ki: (b, 0, qi[j], 0)),
                  pl.BlockSpec((None, tile, kq), lambda b, j, qi, ki: (b, ki[j], 0)),
                  pl.BlockSpec(bias.shape, lambda b, j, qi, ki: (0, 0, 0)),
                  pl.BlockSpec(wuv.shape, lambda b, j, qi, ki: (0, 0, 0))],
        out_specs=pl.BlockSpec((None, tile, dv), lambda b, j, qi, ki: (b, qi[j], 0)),
        scratch_shapes=([pltpu.VMEM((1, tile), F32)] * (2 * n_heads) + [pltpu.VMEM((kv_rank, tile), F32)] * n_heads))
    return pl.pallas_call(
        kern, grid_spec=grid_spec, out_shape=jax.ShapeDtypeStruct((nb, n_t, dv), F32),
        compiler_params=_params(("arbitrary", "arbitrary")), name="prompt_attention",
    )(qi_tab, ki_tab, q, k, bias, wuv)


def _sample_attn_kernel(pt_ref, q_ref, kn_ref, tsel_ref, wuv_ref, ckv_hbm, kpe_hbm, o_ref,
                        ck_buf, kp_buf, sem, qc_sc, m_sc, l_sc, acc_sc, *,
                        n_pages_step, layer, kv_rank, n_new, rows_per_head):
    b = pl.program_id(0)
    g = pl.program_id(1)
    n_g = pl.num_programs(1)
    step = b * n_g + g
    n_steps = pl.num_programs(0) * n_g
    slot = step % 2

    def page_copies(seq, grp, sl):
        copies = []
        for i in range(n_pages_step):
            page = pt_ref[seq, grp * n_pages_step + i]
            copies.append(pltpu.make_async_copy(
                ckv_hbm.at[page, layer], ck_buf.at[sl, pl.ds(i * PAGE_SIZE, PAGE_SIZE), :], sem.at[sl]))
            copies.append(pltpu.make_async_copy(
                kpe_hbm.at[page, layer], kp_buf.at[sl, :, pl.ds(i * PAGE_SIZE, PAGE_SIZE)], sem.at[sl]))
        return copies

    @pl.when(step == 0)
    def _():
        for c in page_copies(0, 0, 0):
            c.start()

    @pl.when(step + 1 < n_steps)
    def _():
        nxt = step + 1
        for c in page_copies(nxt // n_g, nxt % n_g, 1 - slot):
            c.start()

    @pl.when(g == 0)
    def _():
        _softmax_init(m_sc, l_sc, acc_sc)
        qc_sc[...] = _dot(q_ref[:, kv_rank:], tsel_ref[...]).astype(BF16)

    for c in page_copies(b, g, slot):
        c.wait()

    q = q_ref[...]
    ck = ck_buf[slot].astype(BF16)
    kp_t = kp_buf[slot].astype(BF16)
    s = _dot_nt(q[:, 0:kv_rank], ck) + _dot(qc_sc[...], kp_t)
    _softmax_step(s, ck, m_sc, l_sc, acc_sc)

    @pl.when(g == pl.num_programs(1) - 1)
    def _():
        kn = kn_ref[...]
        s_new = _dot_nt(q, kn)
        t_row = lax.broadcasted_iota(jnp.int32, s_new.shape, 0) % rows_per_head
        t_col = lax.broadcasted_iota(jnp.int32, s_new.shape, 1)
        s_new = jnp.where((t_col <= t_row) & (t_col < n_new), s_new, NEG_INF)
        _softmax_step(s_new, kn[:, 0:kv_rank], m_sc, l_sc, acc_sc)
        o = acc_sc[...] / l_sc[...]
        rp = rows_per_head
        for p in range(N_HEADS // 2):
            o2 = jnp.concatenate([o[2 * p * rp:(2 * p + 1) * rp], o[(2 * p + 1) * rp:(2 * p + 2) * rp]], axis=1)
            o_ref[:, p * 2 * V_DIM:(p + 1) * 2 * V_DIM] = _dot(o2.astype(BF16), wuv_ref[p])


def _sample_attention(q, k, wuv, cache_ckv, cache_kpe_t, page_table, layer):
    nb, n_heads, n_t, kq = q.shape
    kv_rank = wuv.shape[1] // 2
    dv = N_HEADS * V_DIM
    n_pages = page_table.shape[1]
    gp = min(PAGES_PER_STEP, n_pages)
    assert n_pages % gp == 0 and n_t <= SUBLANES
    rp = SUBLANES
    qb = jnp.pad(q, ((0, 0), (0, 0), (0, rp - n_t), (0, 0))).reshape(nb, n_heads * rp, kq)
    kb = jnp.pad(k, ((0, 0), (0, rp - n_t), (0, 0)))
    tsel = jnp.tile(jnp.eye(QK_ROPE, dtype=BF16), (N_HEADS, 1))
    kern = functools.partial(_sample_attn_kernel, n_pages_step=gp, layer=layer, kv_rank=kv_rank, n_new=n_t,
                             rows_per_head=rp)
    in_specs = [pl.BlockSpec((None, n_heads * rp, kq), lambda b, g, pt: (b, 0, 0)),
                pl.BlockSpec((None, rp, kq), lambda b, g, pt: (b, 0, 0)),
                pl.BlockSpec(tsel.shape, lambda b, g, pt: (0, 0)),
                pl.BlockSpec(wuv.shape, lambda b, g, pt: (0, 0, 0)),
                pl.BlockSpec(memory_space=pl.ANY), pl.BlockSpec(memory_space=pl.ANY)]
    grid_spec = pltpu.PrefetchScalarGridSpec(
        num_scalar_prefetch=1, grid=(nb, n_pages // gp), in_specs=in_specs,
        out_specs=pl.BlockSpec((None, rp, dv), lambda b, g, pt: (b, 0, 0)),
        scratch_shapes=[pltpu.VMEM((2, gp * PAGE_SIZE, kv_rank), F32), pltpu.VMEM((2, QK_ROPE, gp * PAGE_SIZE), F32),
                        pltpu.SemaphoreType.DMA((2,)),
                        pltpu.VMEM((n_heads * rp, QK_ROPE), BF16), pltpu.VMEM((n_heads * rp, 1), F32),
                        pltpu.VMEM((n_heads * rp, 1), F32), pltpu.VMEM((n_heads * rp, kv_rank), F32)])
    out = pl.pallas_call(
        kern, grid_spec=grid_spec, out_shape=jax.ShapeDtypeStruct((nb, rp, dv), F32),
        compiler_params=_params(("arbitrary", "arbitrary")), name="sample_attention",
    )(page_table, qb, kb, tsel, wuv, cache_ckv, cache_kpe_t)
    return out[:, :n_t]


def _odd_post_kernel(x_ref, yc_ref, yd_ref, wout_ref, g_ref, b_ref, o_ref, rl_sc, *, nb, tt):
    dc = yc_ref.shape[1]

    n_lane_tiles = rl_sc.shape[0]

    def body(b, carry):
        v = yd_ref[b]
        for c in range(n_lane_tiles):
            rl_sc[c, pl.ds(b, tt, stride=nb), :] = v[:, c * LANES:(c + 1) * LANES]
        return carry

    lax.fori_loop(0, nb, body, 0)
    yd = jnp.concatenate([rl_sc[c] for c in range(n_lane_tiles)], axis=1)
    m = _dot(yc_ref[...].astype(BF16), wout_ref[0:dc, :]) + _dot(yd.astype(BF16), wout_ref[dc:, :])
    o_ref[...] = _layer_norm(ALPHA * x_ref[...] + m, g_ref[...], b_ref[...])


def _odd_post(x, yc, yd, wout, g, b, *, nb, n_t):
    m_rows, d = x.shape
    dv = yd.shape[2]
    tt = _seq_tile(nb, n_t)
    tm = tt * nb
    row = lambda i: (i, 0)
    kern = functools.partial(_odd_post_kernel, nb=nb, tt=tt)
    return pl.pallas_call(
        kern, grid=(m_rows // tm,),
        in_specs=[pl.BlockSpec((tm, d), row), pl.BlockSpec((tm, yc.shape[1]), row),
                  pl.BlockSpec((nb, tt, dv), lambda i: (0, i, 0)),
                  _full(wout.shape), _full(g.shape), _full(b.shape)],
        out_specs=pl.BlockSpec((tm, d), row), out_shape=jax.ShapeDtypeStruct((m_rows, d), F32),
        scratch_shapes=[pltpu.VMEM((dv // LANES, tm, LANES), F32)],
        compiler_params=_params(("arbitrary",)), name="odd_post",
    )(x, yc, yd, wout, g, b)


def _token_tile(m_rows):
    tm = min(m_rows, TOKEN_TILE)
    assert m_rows % tm == 0
    return tm


def _router_kernel(x_ref, whi_ref, wlo_ref, br_ref, tri_ref, o_ref, cnt_ref, base_sc):
    tm = x_ref.shape[0]

    @pl.when(pl.program_id(0) == 0)
    def _():
        base_sc[...] = jnp.zeros(base_sc.shape, F32)

    x = x_ref[...]
    x_hi = x.astype(BF16)
    x_lo = (x - x_hi.astype(F32)).astype(BF16)
    logits = _dot(x_hi, whi_ref[...]) + (_dot(x_lo, whi_ref[...]) + _dot(x_hi, wlo_ref[...]))
    scores = _sigmoid(logits.T[0:N_EXPERTS, :])
    sel = scores + br_ref[...]

    def row(a, e):
        return a[e:e + 1, :]

    best_gs = None
    grp = None
    for gidx in range(N_GROUPS):
        v = [row(sel, gidx * EXPERTS_PER_GROUP + i) for i in range(EXPERTS_PER_GROUP)]
        gs = None
        for i in range(EXPERTS_PER_GROUP):
            for j in range(i + 1, EXPERTS_PER_GROUP):
                pair = v[i] + v[j]
                gs = pair if gs is None else jnp.maximum(gs, pair)
        if best_gs is None:
            best_gs, grp = gs, jnp.zeros(gs.shape, jnp.int32)
        else:
            better = gs > best_gs
            grp = jnp.where(better, gidx, grp)
            best_gs = jnp.where(better, gs, best_gs)

    def pick(a, i):
        out = row(a, i)
        for gidx in range(1, N_GROUPS):
            out = jnp.where(grp == gidx, row(a, gidx * EXPERTS_PER_GROUP + i), out)
        return out

    cand = [pick(sel, i) for i in range(EXPERTS_PER_GROUP)]
    cand_score = [pick(scores, i) for i in range(EXPERTS_PER_GROUP)]

    def argmax_first(vals, exclude=None):
        best, idx = None, None
        for i, v in enumerate(vals):
            if exclude is not None:
                v = jnp.where(exclude == i, NEG_INF, v)
            if best is None:
                best, idx = v, jnp.zeros(v.shape, jnp.int32)
            else:
                better = v > best
                idx = jnp.where(better, i, idx)
                best = jnp.where(better, v, best)
        return idx

    loc1 = argmax_first(cand)
    loc2 = argmax_first(cand, exclude=loc1)

    def take(vals, idx):
        out = vals[0]
        for i in range(1, len(vals)):
            out = jnp.where(idx == i, vals[i], out)
        return out

    g1 = take(cand_score, loc1)
    g2 = take(cand_score, loc2)
    gsum = g1 + g2
    first_is_a = loc1 < loc2
    a = jnp.minimum(loc1, loc2)
    b = jnp.maximum(loc1, loc2)
    pair = jnp.where(a == 0, b - 1, jnp.where(a == 1, b + 1, PAIRS_PER_GROUP - 1))
    bucket = grp * PAIRS_PER_GROUP + pair
    gate_a = jnp.where(first_is_a, g1, g2) / gsum
    gate_b = jnp.where(first_is_a, g2, g1) / gsum

    n_rows = base_sc.shape[0]
    hit = lax.broadcasted_iota(jnp.int32, (n_rows, tm), 0) == bucket
    ones = jnp.where(hit, 1.0, 0.0)
    before = _dot(ones.astype(BF16), tri_ref[...]) + base_sc[...]
    rank = jnp.sum(jnp.where(hit, before, 0.0), axis=0, keepdims=True)
    base_sc[...] = base_sc[...] + jnp.sum(ones, axis=1, keepdims=True)
    cnt_ref[...] = jnp.broadcast_to(base_sc[...], cnt_ref.shape)

    zero = jnp.zeros((1, tm), F32)
    o_ref[...] = jnp.concatenate([bucket.astype(F32), rank, gate_a, gate_b, zero, zero, zero, zero], axis=0)


def _router(x, whi, wlo, br):
    m_rows, d = x.shape
    tm = _token_tile(m_rows)
    tri = jnp.triu(jnp.ones((tm, tm), BF16), k=1)
    return pl.pallas_call(
        _router_kernel, grid=(m_rows // tm,),
        in_specs=[pl.BlockSpec((tm, d), lambda i: (i, 0)), _full(whi.shape), _full(wlo.shape), _full(br.shape),
                  _full(tri.shape)],
        out_specs=(pl.BlockSpec((SUBLANES, tm), lambda i: (0, i)), _full((N_BUCKETS, LANES))),
        out_shape=(jax.ShapeDtypeStruct((SUBLANES, m_rows), F32), jax.ShapeDtypeStruct((N_BUCKETS, LANES), F32)),
        scratch_shapes=[pltpu.VMEM((N_BUCKETS, 1), F32)],
        compiler_params=_params(("arbitrary",)), name="router",
    )(x, whi, wlo, br, tri)


def _row_copy(src_ref, src_row, dst_ref, dst_row, sem):
    return pltpu.make_async_copy(src_ref.at[pl.ds(src_row, 1), :], dst_ref.at[pl.ds(dst_row, 1), :], sem)


def _expert_kernel(tea_ref, teb_ref, src_ref, src_next_ref, dst_prev_ref, dst_ref, ga_ref, gb_ref, x_hbm,
                   win_a_ref, wout_a_ref, win_b_ref, wout_b_ref, y_hbm, xg, yo, gsem, ssem):
    del tea_ref, teb_ref
    tile = xg.shape[1]
    de = wout_a_ref.shape[0]
    i = pl.program_id(0)
    last = pl.num_programs(0) - 1
    slot = i % 2

    def gather(idx_ref, sl):
        for r in range(tile):
            _row_copy(x_hbm, idx_ref[0, 0, r], xg.at[sl], r, gsem.at[sl]).start(priority=r % 2)

    def scatter(idx_ref, sl):
        for r in range(tile):
            _row_copy(yo.at[sl], r, y_hbm, idx_ref[0, 0, r], ssem.at[sl]).start(priority=(r + 1) % 2)

    def wait_gather(sl):
        pltpu.make_async_copy(x_hbm.at[pl.ds(0, tile), :], xg.at[sl], gsem.at[sl]).wait()

    def wait_scatter(sl):
        pltpu.make_async_copy(yo.at[sl], y_hbm.at[pl.ds(0, tile), :], ssem.at[sl]).wait()

    @pl.when(i == 0)
    def _():
        yo[...] = jnp.zeros(yo.shape, F32)
        gather(src_ref, 0)

    wait_gather(slot)

    @pl.when(i > 0)
    def _():
        wait_scatter(slot)

    gather(src_next_ref, 1 - slot)
    scatter(dst_prev_ref, 1 - slot)
    xb = xg[slot].astype(BF16)

    def expert(win_ref, wout_ref):
        h = _dot(xb, win_ref[...])
        hg = h[:, 0:de]
        act = hg * _sigmoid(hg) * h[:, de:2 * de]
        return _dot(act.astype(BF16), wout_ref[...])

    yo[slot] = ga_ref[...] * expert(win_a_ref, wout_a_ref) + gb_ref[...] * expert(win_b_ref, wout_b_ref)

    @pl.when(i == last)
    def _():
        scatter(dst_ref, slot)
        wait_scatter(1 - slot)
        wait_scatter(slot)
        wait_gather(1 - slot)


def _experts(x, tile_ea, tile_eb, src_tiles, dst_tiles, gate_a, gate_b, w_in, w_out, layer):
    m_rows, d = x.shape
    n_tiles, _, tile = src_tiles.shape
    de2 = w_in.shape[3]
    smem = lambda fn: pl.BlockSpec((1, 1, tile), fn, memory_space=pltpu.SMEM)
    slot_col = pl.BlockSpec((tile, 1), lambda i, ea, eb: (i, 0))
    grid_spec = pltpu.PrefetchScalarGridSpec(
        num_scalar_prefetch=2, grid=(n_tiles,),
        in_specs=[smem(lambda i, ea, eb: (i, 0, 0)),
                  smem(lambda i, ea, eb: (jnp.minimum(i + 1, n_tiles - 1), 0, 0)),
                  smem(lambda i, ea, eb: (i, 0, 0)),
                  smem(lambda i, ea, eb: (i + 1, 0, 0)),
                  slot_col, slot_col,
                  pl.BlockSpec(memory_space=pl.ANY),
                  pl.BlockSpec((None, None, d, de2), lambda i, ea, eb: (layer, ea[i], 0, 0)),
                  pl.BlockSpec((None, None, de2 // 2, d), lambda i, ea, eb: (layer, ea[i], 0, 0)),
                  pl.BlockSpec((None, None, d, de2), lambda i, ea, eb: (layer, eb[i], 0, 0)),
                  pl.BlockSpec((None, None, de2 // 2, d), lambda i, ea, eb: (layer, eb[i], 0, 0))],
        out_specs=pl.BlockSpec(memory_space=pl.ANY),
        scratch_shapes=[pltpu.VMEM((2, tile, d), F32), pltpu.VMEM((2, tile, d), F32),
                        pltpu.SemaphoreType.DMA((2,)), pltpu.SemaphoreType.DMA((2,))])
    return pl.pallas_call(
        _expert_kernel, grid_spec=grid_spec, out_shape=jax.ShapeDtypeStruct(((n_tiles + 1) * tile, d), F32),
        compiler_params=_params(("arbitrary",)), name="moe_experts",
    )(tile_ea, tile_eb, src_tiles, src_tiles, dst_tiles, dst_tiles, gate_a, gate_b, x, w_in, w_out, w_in, w_out)


def _combine_kernel(x_ref, y_ref, g_ref, b_ref, o_ref):
    o_ref[...] = _layer_norm(ALPHA * x_ref[...] + y_ref[...], g_ref[...], b_ref[...])


def _combine(x, y, g, b):
    m_rows, d = x.shape
    tm = _token_tile(m_rows)
    row = lambda i: (i, 0)
    return pl.pallas_call(
        _combine_kernel, grid=(m_rows // tm,),
        in_specs=[pl.BlockSpec((tm, d), row), pl.BlockSpec((tm, d), row), _full(g.shape), _full(b.shape)],
        out_specs=pl.BlockSpec((tm, d), row), out_shape=jax.ShapeDtypeStruct((m_rows, d), F32),
        compiler_params=_params(("parallel",)), name="moe_combine",
    )(x, y, g, b)


def _moe_layer(x, rw, w_in, w_out, layer, g, b, expert_tile):
    m_rows, _ = x.shape
    route, counts = _router(x, rw["whi"], rw["wlo"], rw["br"])
    bucket = route[0].astype(jnp.int32)
    rank = route[1].astype(jnp.int32)
    counts = counts[:, 0].astype(jnp.int32)
    padded = (counts + expert_tile - 1) // expert_tile * expert_tile
    pad_ends = jnp.cumsum(padded)
    pad_starts = pad_ends - padded
    b_range = jnp.arange(N_BUCKETS, dtype=jnp.int32)
    dest = jnp.sum(jnp.where(bucket[:, None] == b_range, pad_starts, 0), axis=-1) + rank
    n_tiles = -(-(m_rows + N_BUCKETS * (expert_tile - 1)) // expert_tile)
    n_slots = n_tiles * expert_tile
    tile_start = jnp.arange(n_tiles, dtype=jnp.int32) * expert_tile
    tile_bucket = jnp.minimum(jnp.sum((tile_start[:, None] >= pad_ends[None, :]).astype(jnp.int32), axis=1),
                              N_BUCKETS - 1)
    pair = tile_bucket % PAIRS_PER_GROUP
    group_base = tile_bucket // PAIRS_PER_GROUP * EXPERTS_PER_GROUP
    tile_ea = group_base + jnp.sum(jnp.where(pair[:, None] == jnp.arange(PAIRS_PER_GROUP), jnp.asarray(PAIR_FIRST), 0), axis=1)
    tile_eb = group_base + jnp.sum(jnp.where(pair[:, None] == jnp.arange(PAIRS_PER_GROUP), jnp.asarray(PAIR_SECOND), 0), axis=1)
    n_unused = n_slots - m_rows
    pad_cnt = padded - counts
    pad_cum = jnp.cumsum(pad_cnt)
    j = jnp.arange(n_unused, dtype=jnp.int32)
    in_or_after = j[:, None] >= pad_cum[None, :]
    before = jnp.sum(jnp.where(in_or_after, pad_cnt, 0), axis=1)
    owner = jnp.sum(in_or_after.astype(jnp.int32), axis=1)
    first_unused = pad_ends - pad_cnt
    base = (jnp.sum(jnp.where(owner[:, None] == b_range, first_unused, 0), axis=1)
            + jnp.where(owner == N_BUCKETS, pad_ends[-1], 0))
    unused_slot = base + (j - before)
    keys = jnp.concatenate([dest, unused_slot])
    no_gate = jnp.zeros((n_unused,), F32)
    _, code, gate_a, gate_b = lax.sort(
        (keys, jnp.arange(n_slots, dtype=jnp.int32), jnp.concatenate([route[2], no_gate]),
         jnp.concatenate([route[3], no_gate])), num_keys=1)
    src_tiles = jnp.where(code >= m_rows, 0, code).reshape(n_tiles, 1, expert_tile)
    dst_tiles = jnp.concatenate([n_slots + jnp.arange(expert_tile, dtype=jnp.int32), code])
    dst_tiles = dst_tiles.reshape(n_tiles + 1, 1, expert_tile)
    y = _experts(x, tile_ea.astype(jnp.int32), tile_eb.astype(jnp.int32), src_tiles, dst_tiles,
                 gate_a.reshape(n_slots, 1), gate_b.reshape(n_slots, 1), w_in, w_out, layer)
    return _combine(x, y, g, b)


def _block_diag(w):
    h, n, _ = w.shape
    eye = jnp.eye(h, dtype=w.dtype)
    return (eye[:, None, :, None] * w[:, :, None, :]).reshape(h * n, h * n)


def _pair_blocks(w):
    h, a, b = w.shape
    w = w.reshape(h // 2, 2, a, b)
    eye = jnp.eye(2, dtype=w.dtype)
    return (eye[None, :, None, :, None] * w[:, :, :, None, :]).reshape(h // 2, 2 * a, 2 * b)


def _time_major(a):
    b, t, c = a.shape
    return a.transpose(1, 0, 2).reshape(t * b, c)


def _batch_major(a, nb):
    return a.reshape(-1, nb, a.shape[-1]).transpose(1, 0, 2)


def _rope_tables(pos):
    half = QK_ROPE // 2
    inv = ROPE_THETA ** (-jnp.arange(half, dtype=F32) / half)
    ang = pos.astype(F32)[:, None] * inv
    cos, sin = jnp.cos(ang), jnp.sin(ang)
    cos_t = jnp.tile(jnp.concatenate([cos, cos], axis=1), (1, N_HEADS))
    sin_t = jnp.tile(jnp.concatenate([-sin, sin], axis=1), (1, N_HEADS))
    return cos_t, sin_t


def kernel(x_prompt, x_sample, state_conv_a, state_conv_b, state_rglru_h, state_conv_c, cache_ckv, cache_kpe, page_table, w_in_even, conv_a_w, conv_b_w, conv_b_b, lru_wa, lru_ba, lru_wx, lru_bx, lru_lambda, w_out_even, w_in_odd, conv_c_w, conv_c_b, ln_c_g, ln_c_b, q_norm_g, w_q_b, kv_norm_g, w_uk, w_uv, w_out_odd, ln_mix_g, ln_mix_b, ln_ffn_g, ln_ffn_b, w_router, b_router, w_exp_in, w_exp_out):
    bp, n_tp, d = x_prompt.shape
    bs, n_ts, _ = x_sample.shape
    past_len = page_table.shape[1] * PAGE_SIZE
    dc = conv_c_w.shape[2]
    q_rank = q_norm_g.shape[1]
    kv_rank = kv_norm_g.shape[1]

    xp = _time_major(x_prompt)
    xs = _time_major(x_sample)
    row2 = lambda v: v.reshape(1, -1)

    wr = jnp.pad(w_router, ((0, 0), (0, LANES - N_EXPERTS)))
    wr_hi = wr.astype(BF16)
    router_w = {"whi": wr_hi, "wlo": (wr - wr_hi.astype(F32)).astype(BF16), "br": b_router.reshape(N_EXPERTS, 1)}
    rope_p = _rope_tables(jnp.arange(n_tp))
    rope_s = _rope_tables(past_len + jnp.arange(n_ts))
    we_in, we_out = w_exp_in.astype(BF16), w_exp_out.astype(BF16)
    cache_kpe_t = jnp.swapaxes(cache_kpe, 2, 3)

    outs = {k: [] for k in ("ca_p", "ca_s", "cb_p", "cb_s", "h_p", "h_s", "cc_p", "cc_s", "ckv_p", "ckv_s", "kpe_p", "kpe_s")}
    for l in range(DEPTH):
        j = l // 2
        lg, lb = row2(ln_mix_g[l]), row2(ln_mix_b[l])
        if l % 2 == 0:
            wa_bd, wx_bd = _block_diag(lru_wa[j]), _block_diag(lru_wx[j])
            hc = wa_bd.shape[0] // 2
            wg = jnp.stack([jnp.concatenate([wa_bd[s * hc:(s + 1) * hc, s * hc:(s + 1) * hc],
                                             wx_bd[s * hc:(s + 1) * hc, s * hc:(s + 1) * hc]], axis=1)
                            for s in range(2)]).astype(BF16)
            w = {"win": w_in_even[j].astype(BF16), "caw": conv_a_w[j], "cbw": conv_b_w[j], "cbb": row2(conv_b_b[j]),
                 "wg": wg, "ba": row2(lru_ba[j]), "bx": row2(lru_bx[j]), "lam": row2(lru_lambda[j]),
                 "wout": w_out_even[j].astype(BF16), "g": lg, "b": lb}
            da = conv_a_w.shape[2]
            db = conv_b_w.shape[2]
            xp, a1, b1, h1 = _even_layer(xp, w, jnp.zeros((2 * bp, da), F32), jnp.zeros((3 * bp, db), F32),
                                         jnp.zeros((bp, db), F32), nb=bp, n_t=n_tp)
            xs, a2, b2, h2 = _even_layer(xs, w, _time_major(state_conv_a[j]), _time_major(state_conv_b[j]),
                                         state_rglru_h[j], nb=bs, n_t=n_ts)
            outs["ca_p"].append(_batch_major(a1, bp)); outs["ca_s"].append(_batch_major(a2, bs))
            outs["cb_p"].append(_batch_major(b1, bp)); outs["cb_s"].append(_batch_major(b2, bs))
            outs["h_p"].append(h1); outs["h_s"].append(h2)
        else:
            nope_all = N_HEADS * QK_NOPE
            wq = w_q_b[j].reshape(q_rank, N_HEADS, QK_NOPE + QK_ROPE)
            wqb = jnp.concatenate([wq[:, :, :QK_NOPE].reshape(q_rank, nope_all),
                                   wq[:, :, QK_NOPE:].reshape(q_rank, N_HEADS * QK_ROPE)], axis=1).astype(BF16)
            o_pe = 2 * dc + q_rank + kv_rank
            win = jnp.concatenate([w_in_odd[j][:, :o_pe], jnp.tile(w_in_odd[j][:, o_pe:], (1, N_HEADS))], axis=1).astype(BF16)
            wuk = _pair_blocks(w_uk[j].transpose(1, 2, 0)).astype(BF16)
            wuv = _pair_blocks(w_uv[j].transpose(1, 0, 2)).astype(BF16)
            w = {"win": win, "ccw": conv_c_w[j], "ccb": row2(conv_c_b[j]), "lcg": row2(ln_c_g[j]), "lcb": row2(ln_c_b[j]),
                 "qg": row2(q_norm_g[j]), "wqb": wqb, "kvg": row2(kv_norm_g[j]), "wuk": wuk}
            width = conv_c_w.shape[1]
            wout = w_out_odd[j].astype(BF16)
            ycp, qp, kp, ckv1, kpe1, c1 = _odd_pre(xp, w, jnp.zeros(((width - 1) * bp, dc), F32), *rope_p, nb=bp, n_t=n_tp)
            ycs, qs, ks, ckv2, kpe2, c2 = _odd_pre(xs, w, _time_major(state_conv_c[j]), *rope_s, nb=bs, n_t=n_ts)
            ydp = _prompt_attention(qp, kp, wuv)
            yds = _sample_attention(qs, ks, wuv, cache_ckv, cache_kpe_t, page_table, j)
            xp = _odd_post(xp, ycp, ydp, wout, lg, lb, nb=bp, n_t=n_tp)
            xs = _odd_post(xs, ycs, yds, wout, lg, lb, nb=bs, n_t=n_ts)
            outs["cc_p"].append(_batch_major(c1, bp)); outs["cc_s"].append(_batch_major(c2, bs))
            outs["ckv_p"].append(ckv1); outs["ckv_s"].append(ckv2)
            outs["kpe_p"].append(kpe1); outs["kpe_s"].append(kpe2)
        fg, fb = row2(ln_ffn_g[l]), row2(ln_ffn_b[l])
        xp = _moe_layer(xp, router_w, we_in, we_out, l, fg, fb, expert_tile=512)
        xs = _moe_layer(xs, router_w, we_in, we_out, l, fg, fb, expert_tile=64)

    y_prompt = _batch_major(xp, bp)
    y_sample = _batch_major(xs, bs)
    st = lambda k: jnp.stack(outs[k])
    st1 = lambda k: jnp.stack(outs[k], axis=1)
    return (y_prompt, y_sample, st("ca_p"), st("ca_s"), st("cb_p"), st("cb_s"), st("h_p"), st("h_s"),
            st("cc_p"), st("cc_s"), st1("ckv_p"), st1("ckv_s"), st1("kpe_p"), st1("kpe_s"))
```

```python
import functools

import jax
import jax.numpy as jnp
from jax import lax
from jax.experimental import pallas as pl
from jax.experimental.pallas import tpu as pltpu

F32 = jnp.float32
BF16 = jnp.bfloat16

DEPTH = 4
N_HEADS = 8
QK_NOPE = 64
QK_ROPE = 32
V_DIM = 64
N_EXPERTS = 16
N_GROUPS = 4
EXPERTS_PER_GROUP = 4
PAIRS_PER_GROUP = 6
N_BUCKETS = N_GROUPS * PAIRS_PER_GROUP
PAIR_FIRST = (0, 0, 0, 1, 1, 2)
PAIR_SECOND = (1, 2, 3, 2, 3, 3)
LRU_C = 8.0
ROPE_THETA = 10000.0
ATTN_SCALE = (QK_NOPE + QK_ROPE) ** -0.5
ALPHA = (2 * DEPTH) ** 0.25
PAGE_SIZE = 128

SUBLANES = 8
BF16_SUBLANES = 16
LANES = 128
VMEM_LIMIT_BYTES = 56 * 1024 * 1024

TIME_STEPS_PER_TILE = 32
TOKEN_TILE = 512
ATTN_TILE = 512
PAGES_PER_STEP = 64
NEG_INF = float("-inf")


def _params(semantics):
    return pltpu.CompilerParams(dimension_semantics=semantics, vmem_limit_bytes=VMEM_LIMIT_BYTES)


def _full(shape):
    nd = len(shape)
    return pl.BlockSpec(shape, lambda *_: (0,) * nd)


def _layer_norm(x, g, b, eps=1e-5):
    mu = jnp.mean(x, axis=-1, keepdims=True)
    xc = x - mu
    var = jnp.mean(xc * xc, axis=-1, keepdims=True)
    return xc * lax.rsqrt(var + eps) * g + b


def _rms_norm(x, g, eps=1e-6):
    return x * lax.rsqrt(jnp.mean(x * x, axis=-1, keepdims=True) + eps) * g


def _sigmoid(x):
    return 1.0 / (1.0 + jnp.exp(-x))


def _dot(a, b):
    return jnp.dot(a, b, preferred_element_type=F32)


def _dot_nt(a, b):
    return lax.dot_general(a, b, (((1,), (1,)), ((), ())), preferred_element_type=F32)


def _seq_tile(nb, n_t):
    tt = min(n_t, TIME_STEPS_PER_TILE)
    assert n_t % tt == 0 and nb % SUBLANES == 0
    return tt


def _even_kernel(x_ref, win_ref, caw_ref, cbw_ref, cbb_ref, wg_ref, ba_ref, bx_ref, lam_ref, wout_ref,
                 g_ref, b_ref, sa_ref, sb_ref, h0_ref,
                 o_ref, na_ref, nb_ref, hl_ref,
                 ua_ext, vb_ext, h_sc, hs_sc, *, nb, tt):
    tm = tt * nb
    dh = caw_ref.shape[1]
    half = dh // 2

    @pl.when(pl.program_id(0) == 0)
    def _():
        ua_ext[0:2 * nb, :] = sa_ref[...]
        vb_ext[0:3 * nb, :] = sb_ref[...]
        h_sc[...] = h0_ref[...]

    x = x_ref[...]
    xb = x.astype(BF16)

    def proj(j):
        return _dot(xb, win_ref[:, j * dh:(j + 1) * dh])

    ua_ext[2 * nb:2 * nb + tm, :] = proj(1) * proj(2)
    caw = caw_ref[...]
    conv_a = (caw[0:1] * ua_ext[0:tm, :] + caw[1:2] * ua_ext[nb:nb + tm, :]
              + caw[2:3] * ua_ext[2 * nb:2 * nb + tm, :])
    y_a = proj(0) * conv_a
    tail_a = ua_ext[tm:tm + 2 * nb, :]
    na_ref[...] = tail_a
    ua_ext[0:2 * nb, :] = tail_a

    vb_ext[3 * nb:3 * nb + tm, :] = proj(4)
    cbw = cbw_ref[...]
    u_b = (cbb_ref[...] + cbw[0:1] * vb_ext[0:tm, :] + cbw[1:2] * vb_ext[nb:nb + tm, :]
           + cbw[2:3] * vb_ext[2 * nb:2 * nb + tm, :] + cbw[3:4] * vb_ext[3 * nb:3 * nb + tm, :])
    tail_b = vb_ext[tm:tm + 3 * nb, :]
    nb_ref[...] = tail_b
    vb_ext[0:3 * nb, :] = tail_b

    ub16 = u_b.astype(BF16)
    gk0 = _dot(ub16[:, :half], wg_ref[0])
    gk1 = _dot(ub16[:, half:], wg_ref[1])
    r = _sigmoid(jnp.concatenate([gk0[:, :half], gk1[:, :half]], axis=1) + ba_ref[...])
    ig = _sigmoid(jnp.concatenate([gk0[:, half:], gk1[:, half:]], axis=1) + bx_ref[...])
    nlam = -lam_ref[...]
    softplus = jnp.maximum(nlam, 0.0) + jnp.log(1.0 + jnp.exp(-jnp.abs(nlam)))
    log_a = (-LRU_C * r) * softplus
    a = jnp.exp(log_a)
    bterm = jnp.sqrt(1.0 - a * a) * (ig * u_b)

    h = h_sc[...]
    for t in range(tt):
        h = a[t * nb:(t + 1) * nb, :] * h + bterm[t * nb:(t + 1) * nb, :]
        hs_sc[t * nb:(t + 1) * nb, :] = h
    h_sc[...] = h
    hl_ref[...] = h
    y_b = jax.nn.gelu(proj(3), approximate=True) * hs_sc[...]

    m = _dot(y_a.astype(BF16), wout_ref[0:dh, :]) + _dot(y_b.astype(BF16), wout_ref[dh:2 * dh, :])
    o_ref[...] = _layer_norm(ALPHA * x + m, g_ref[...], b_ref[...])


def _even_layer(x, w, sa, sb, h0, *, nb, n_t):
    m_rows, d = x.shape
    dh = w["caw"].shape[1]
    tt = _seq_tile(nb, n_t)
    tm = tt * nb
    kern = functools.partial(_even_kernel, nb=nb, tt=tt)
    row = lambda i: (i, 0)
    in_specs = [pl.BlockSpec((tm, d), row), _full(w["win"].shape), _full(w["caw"].shape), _full(w["cbw"].shape),
                _full(w["cbb"].shape), _full(w["wg"].shape), _full(w["ba"].shape), _full(w["bx"].shape),
                _full(w["lam"].shape), _full(w["wout"].shape), _full(w["g"].shape), _full(w["b"].shape),
                _full(sa.shape), _full(sb.shape), _full(h0.shape)]
    out_shape = (jax.ShapeDtypeStruct((m_rows, d), F32), jax.ShapeDtypeStruct(sa.shape, F32),
                 jax.ShapeDtypeStruct(sb.shape, F32), jax.ShapeDtypeStruct(h0.shape, F32))
    out_specs = (pl.BlockSpec((tm, d), row), _full(sa.shape), _full(sb.shape), _full(h0.shape))
    scratch = [pltpu.VMEM((tm + 2 * nb, dh), F32), pltpu.VMEM((tm + 3 * nb, dh), F32),
               pltpu.VMEM((nb, dh), F32), pltpu.VMEM((tm, dh), F32)]
    return pl.pallas_call(
        kern, grid=(m_rows // tm,), in_specs=in_specs, out_specs=out_specs, out_shape=out_shape,
        scratch_shapes=scratch, compiler_params=_params(("arbitrary",)), name="even_mixer",
    )(x, w["win"], w["caw"], w["cbw"], w["cbb"], w["wg"], w["ba"], w["bx"], w["lam"], w["wout"], w["g"], w["b"],
      sa, sb, h0)


def _odd_pre_kernel(x_ref, win_ref, ccw_ref, ccb_ref, lcg_ref, lcb_ref, qg_ref, wqb_ref, kvg_ref, wuk_ref,
                    cos_ref, sin_ref, sc_ref, perm_ref,
                    yc_ref, q_ref, k_ref, ckv_ref, kpe_ref, nc_ref,
                    c_ext, rl_sc, *, nb, tt, width):
    tm = tt * nb
    dc = ccw_ref.shape[1]
    q_rank = qg_ref.shape[1]
    kv_rank = kvg_ref.shape[1]
    pe_all = N_HEADS * QK_ROPE
    hist = (width - 1) * nb

    @pl.when(pl.program_id(0) == 0)
    def _():
        c_ext[0:hist, :] = sc_ref[...]

    xb = x_ref[...].astype(BF16)
    o_q = 2 * dc
    o_kv = o_q + q_rank
    o_pe = o_kv + kv_rank

    glu = _dot(xb, win_ref[:, 0:dc]) * _sigmoid(_dot(xb, win_ref[:, dc:2 * dc]))
    c_ext[hist:hist + tm, :] = glu
    ccw = ccw_ref[...]
    u_c = ccb_ref[...] + ccw[0:1] * c_ext[0:tm, :]
    for k in range(1, width):
        u_c = u_c + ccw[k:k + 1] * c_ext[k * nb:k * nb + tm, :]
    tail = c_ext[tm:tm + hist, :]
    nc_ref[...] = tail
    c_ext[0:hist, :] = tail
    ln = _layer_norm(u_c, lcg_ref[...], lcb_ref[...])
    yc_ref[...] = ln * _sigmoid(ln)

    def to_sequences(val, store):
        n_lane_tiles = val.shape[1] // LANES
        for c in range(n_lane_tiles):
            rl_sc[c] = val[:, c * LANES:(c + 1) * LANES]

        def body(b, carry):
            store(b, jnp.concatenate([rl_sc[c, pl.ds(b, tt, stride=nb), :] for c in range(n_lane_tiles)], axis=1))
            return carry

        lax.fori_loop(0, nb, body, 0)

    cos = jnp.broadcast_to(cos_ref[...][:, None, :], (tt, nb, pe_all)).reshape(tm, pe_all)
    sin = jnp.broadcast_to(sin_ref[...][:, None, :], (tt, nb, pe_all)).reshape(tm, pe_all)
    lane = lax.broadcasted_iota(jnp.int32, (1, pe_all), 1)
    first_half = (lane % QK_ROPE) < (QK_ROPE // 2)
    head_of_lane = lane // QK_ROPE

    def rope(v):
        swapped = jnp.where(first_half, pltpu.roll(v, pe_all - QK_ROPE // 2, 1), pltpu.roll(v, QK_ROPE // 2, 1))
        return v * cos + swapped * sin

    permute_on_mxu = tt % BF16_SUBLANES == 0

    def to_sequences_bf16(val, store):
        if not permute_on_mxu:
            to_sequences(val, lambda b, v: store(b, v.astype(BF16)))
            return
        pv = _dot(perm_ref[...], val.astype(BF16)).astype(BF16)
        for b in range(nb):
            store(b, pv[b * tt:(b + 1) * tt])

    ckv = _rms_norm(_dot(xb, win_ref[:, o_kv:o_kv + kv_rank]), kvg_ref[...])
    kpe = rope(_dot(xb, win_ref[:, o_pe:o_pe + pe_all]))
    k_all = jnp.concatenate([ckv, kpe], axis=1)

    def store_k32(b, v):
        ckv_ref[b] = v[:, 0:kv_rank]
        kpe_ref[b] = v[:, kv_rank:kv_rank + QK_ROPE]

    def store_k16(b, v):
        k_ref[b] = v

    def store_k(b, v):
        store_k32(b, v)
        store_k16(b, v.astype(BF16))

    if permute_on_mxu:
        to_sequences(k_all, store_k32)
        to_sequences_bf16(k_all, store_k16)
    else:
        to_sequences(k_all, store_k)

    qn = _rms_norm(_dot(xb, win_ref[:, o_q:o_q + q_rank]), qg_ref[...]).astype(BF16)
    nope_all = N_HEADS * QK_NOPE
    q_nope = (_dot(qn, wqb_ref[:, 0:nope_all]) * ATTN_SCALE).astype(BF16)
    q_pe = rope(_dot(qn, wqb_ref[:, nope_all:nope_all + pe_all]) * ATTN_SCALE)
    for p in range(N_HEADS // 2):
        lat2 = _dot(q_nope[:, p * 2 * QK_NOPE:(p + 1) * 2 * QK_NOPE], wuk_ref[p])
        for s in range(2):
            h = 2 * p + s

            def store_q(b, v, h=h):
                q_ref[b, h] = v

            to_sequences_bf16(jnp.concatenate([lat2[:, s * kv_rank:(s + 1) * kv_rank],
                                               jnp.where(head_of_lane == h, q_pe, 0.0)], axis=1), store_q)


def _odd_pre(x, w, sc, cos, sin, *, nb, n_t):
    m_rows, d = x.shape
    width, dc = w["ccw"].shape
    kv_rank = w["kvg"].shape[1]
    pe_all = N_HEADS * QK_ROPE
    kq = kv_rank + pe_all
    tt = _seq_tile(nb, n_t)
    tm = tt * nb
    kern = functools.partial(_odd_pre_kernel, nb=nb, tt=tt, width=width)
    row = lambda i: (i, 0)
    seq = lambda i: (0, i, 0)
    in_specs = [pl.BlockSpec((tm, d), row)] + [_full(w[k].shape) for k in
                                                ("win", "ccw", "ccb", "lcg", "lcb", "qg", "wqb", "kvg", "wuk")]
    out_row = jnp.arange(tm, dtype=jnp.int32)
    src_row = (out_row % tt) * nb + out_row // tt
    perm = (src_row[:, None] == out_row[None, :]).astype(BF16)
    in_specs += [pl.BlockSpec((tt, pe_all), row), pl.BlockSpec((tt, pe_all), row), _full(sc.shape), _full(perm.shape)]
    out_shape = (jax.ShapeDtypeStruct((m_rows, dc), F32),
                 jax.ShapeDtypeStruct((nb, N_HEADS, n_t, kq), BF16),
                 jax.ShapeDtypeStruct((nb, n_t, kq), BF16),
                 jax.ShapeDtypeStruct((nb, n_t, kv_rank), F32),
                 jax.ShapeDtypeStruct((nb, n_t, QK_ROPE), F32),
                 jax.ShapeDtypeStruct(sc.shape, F32))
    out_specs = (pl.BlockSpec((tm, dc), row), pl.BlockSpec((nb, N_HEADS, tt, kq), lambda i: (0, 0, i, 0)),
                 pl.BlockSpec((nb, tt, kq), seq), pl.BlockSpec((nb, tt, kv_rank), seq),
                 pl.BlockSpec((nb, tt, QK_ROPE), seq), _full(sc.shape))
    scratch = [pltpu.VMEM((tm + (width - 1) * nb, dc), F32), pltpu.VMEM((kq // LANES, tm, LANES), F32)]
    return pl.pallas_call(
        kern, grid=(m_rows // tm,), in_specs=in_specs, out_specs=out_specs, out_shape=out_shape,
        scratch_shapes=scratch, compiler_params=_params(("arbitrary",)), name="odd_pre",
    )(x, w["win"], w["ccw"], w["ccb"], w["lcg"], w["lcb"], w["qg"], w["wqb"], w["kvg"], w["wuk"], cos, sin, sc, perm)


def _softmax_step(s, v16, m_sc, l_sc, acc_sc):
    m_prev = m_sc[...]
    m_new = jnp.maximum(m_prev, jnp.max(s, axis=-1, keepdims=True))
    alpha = jnp.exp(m_prev - m_new)
    p = jnp.exp(s - m_new)
    l_sc[...] = alpha * l_sc[...] + jnp.sum(p, axis=-1, keepdims=True)
    acc_sc[...] = alpha * acc_sc[...] + _dot(p.astype(BF16), v16)
    m_sc[...] = m_new


def _softmax_init(m_sc, l_sc, acc_sc):
    m_sc[...] = jnp.full(m_sc.shape, NEG_INF, F32)
    l_sc[...] = jnp.zeros(l_sc.shape, F32)
    acc_sc[...] = jnp.zeros(acc_sc.shape, F32)


def _prompt_attn_kernel(qi_ref, ki_ref, q_ref, k_ref, bias_ref, wuv_ref, o_ref, *state, kv_rank):
    m_sc, l_sc, acc_sc = state[0:N_HEADS], state[N_HEADS:2 * N_HEADS], state[2 * N_HEADS:3 * N_HEADS]
    j = pl.program_id(1)
    qi = qi_ref[j]
    ki = ki_ref[j]

    @pl.when(ki == 0)
    def _():
        for h in range(N_HEADS):
            _softmax_init(m_sc[h], l_sc[h], acc_sc[h])

    k = k_ref[...]
    v_t = k[:, 0:kv_rank].T
    diag = (ki == qi).astype(jnp.int32)

    def scores_t(h):
        return _dot_nt(k, q_ref[h]) + bias_ref[diag]

    s_next = scores_t(0)
    for h in range(N_HEADS):
        s = s_next
        if h + 1 < N_HEADS:
            s_next = scores_t(h + 1)
        m_prev = m_sc[h][...]
        m_new = jnp.maximum(m_prev, jnp.max(s, axis=0, keepdims=True))
        alpha = jnp.exp(m_prev - m_new)
        p = jnp.exp(s - m_new)
        l_sc[h][...] = alpha * l_sc[h][...] + jnp.sum(p, axis=0, keepdims=True)
        acc_sc[h][...] = alpha * acc_sc[h][...] + _dot(v_t, p.astype(BF16))
        m_sc[h][...] = m_new

    @pl.when(ki == qi)
    def _():
        for p in range(N_HEADS // 2):
            o2_t = jnp.concatenate([acc_sc[2 * p][...] / l_sc[2 * p][...],
                                    acc_sc[2 * p + 1][...] / l_sc[2 * p + 1][...]], axis=0)
            o_ref[:, p * 2 * V_DIM:(p + 1) * 2 * V_DIM] = _dot(o2_t.T.astype(BF16), wuv_ref[p])


def _prompt_attention(q, k, wuv):
    nb, n_heads, n_t, kq = q.shape
    kv_rank = wuv.shape[1] // 2
    dv = N_HEADS * V_DIM
    tile = min(n_t, ATTN_TILE)
    assert n_t % tile == 0
    nq = n_t // tile
    pairs = [(a, b) for a in range(nq) for b in range(a + 1)]
    qi_tab = jnp.asarray([p[0] for p in pairs], jnp.int32)
    ki_tab = jnp.asarray([p[1] for p in pairs], jnp.int32)
    causal = jnp.where(jnp.arange(tile)[:, None] <= jnp.arange(tile)[None, :], 0.0, NEG_INF).astype(F32)
    bias = jnp.stack([jnp.zeros((tile, tile), F32), causal])
    kern = functools.partial(_prompt_attn_kernel, kv_rank=kv_rank)
    grid_spec = pltpu.PrefetchScalarGridSpec(
        num_scalar_prefetch=2, grid=(nb, len(pairs)),
        in_specs=[pl.BlockSpec((None, n_heads, tile, kq), lambda b, j, qi, ki: (b, 0, qi[j], 0)),
                  pl.BlockSpec((None, tile, kq), lambda b, j, qi, ki: (b, ki[j], 0)),
                  pl.BlockSpec(bias.shape, lambda b, j, qi, ki: (0, 0, 0)),
                  pl.BlockSpec(wuv.shape, lambda b, j, qi, ki: (0, 0, 0))],
        out_specs=pl.BlockSpec((None, tile, dv), lambda b, j, qi, ki: (b, qi[j], 0)),
        scratch_shapes=([pltpu.VMEM((1, tile), F32)] * (2 * n_heads) + [pltpu.VMEM((kv_rank, tile), F32)] * n_heads))
    return pl.pallas_call(
        kern, grid_spec=grid_spec, out_shape=jax.ShapeDtypeStruct((nb, n_t, dv), F32),
        compiler_params=_params(("arbitrary", "arbitrary")), name="prompt_attention",
    )(qi_tab, ki_tab, q, k, bias, wuv)


def _sample_attn_kernel(pt_ref, q_ref, kn_ref, tsel_ref, wuv_ref, ckv_hbm, kpe_hbm, o_ref,
                        ck_buf, kp_buf, sem, qc_sc, m_sc, l_sc, acc_sc, *,
                        n_pages_step, layer, kv_rank, n_new, rows_per_head):
    b = pl.program_id(0)
    g = pl.program_id(1)
    n_g = pl.num_programs(1)
    step = b * n_g + g
    n_steps = pl.num_programs(0) * n_g
    slot = step % 2

    def page_copies(seq, grp, sl):
        copies = []
        for i in range(n_pages_step):
            page = pt_ref[seq, grp * n_pages_step + i]
            copies.append(pltpu.make_async_copy(
                ckv_hbm.at[page, layer], ck_buf.at[sl, pl.ds(i * PAGE_SIZE, PAGE_SIZE), :], sem.at[sl]))
            copies.append(pltpu.make_async_copy(
                kpe_hbm.at[page, layer], kp_buf.at[sl, :, pl.ds(i * PAGE_SIZE, PAGE_SIZE)], sem.at[sl]))
        return copies

    @pl.when(step == 0)
    def _():
        for c in page_copies(0, 0, 0):
            c.start()

    @pl.when(step + 1 < n_steps)
    def _():
        nxt = step + 1
        for c in page_copies(nxt // n_g, nxt % n_g, 1 - slot):
            c.start()

    @pl.when(g == 0)
    def _():
        _softmax_init(m_sc, l_sc, acc_sc)
        qc_sc[...] = _dot(q_ref[:, kv_rank:], tsel_ref[...]).astype(BF16)

    for c in page_copies(b, g, slot):
        c.wait()

    q = q_ref[...]
    ck = ck_buf[slot].astype(BF16)
    kp_t = kp_buf[slot].astype(BF16)
    s = _dot_nt(q[:, 0:kv_rank], ck) + _dot(qc_sc[...], kp_t)
    _softmax_step(s, ck, m_sc, l_sc, acc_sc)

    @pl.when(g == pl.num_programs(1) - 1)
    def _():
        kn = kn_ref[...]
        s_new = _dot_nt(q, kn)
        t_row = lax.broadcasted_iota(jnp.int32, s_new.shape, 0) % rows_per_head
        t_col = lax.broadcasted_iota(jnp.int32, s_new.shape, 1)
        s_new = jnp.where((t_col <= t_row) & (t_col < n_new), s_new, NEG_INF)
        _softmax_step(s_new, kn[:, 0:kv_rank], m_sc, l_sc, acc_sc)
        o = acc_sc[...] / l_sc[...]
        rp = rows_per_head
        for p in range(N_HEADS // 2):
            o2 = jnp.concatenate([o[2 * p * rp:(2 * p + 1) * rp], o[(2 * p + 1) * rp:(2 * p + 2) * rp]], axis=1)
            o_ref[:, p * 2 * V_DIM:(p + 1) * 2 * V_DIM] = _dot(o2.astype(BF16), wuv_ref[p])


def _sample_attention(q, k, wuv, cache_ckv, cache_kpe_t, page_table, layer):
    nb, n_heads, n_t, kq = q.shape
    kv_rank = wuv.shape[1] // 2
    dv = N_HEADS * V_DIM
    n_pages = page_table.shape[1]
    gp = min(PAGES_PER_STEP, n_pages)
    assert n_pages % gp == 0 and n_t <= SUBLANES
    rp = SUBLANES
    qb = jnp.pad(q, ((0, 0), (0, 0), (0, rp - n_t), (0, 0))).reshape(nb, n_heads * rp, kq)
    kb = jnp.pad(k, ((0, 0), (0, rp - n_t), (0, 0)))
    tsel = jnp.tile(jnp.eye(QK_ROPE, dtype=BF16), (N_HEADS, 1))
    kern = functools.partial(_sample_attn_kernel, n_pages_step=gp, layer=layer, kv_rank=kv_rank, n_new=n_t,
                             rows_per_head=rp)
    in_specs = [pl.BlockSpec((None, n_heads * rp, kq), lambda b, g, pt: (b, 0, 0)),
                pl.BlockSpec((None, rp, kq), lambda b, g, pt: (b, 0, 0)),
                pl.BlockSpec(tsel.shape, lambda b, g, pt: (0, 0)),
                pl.BlockSpec(wuv.shape, lambda b, g, pt: (0, 0, 0)),
                pl.BlockSpec(memory_space=pl.ANY), pl.BlockSpec(memory_space=pl.ANY)]
    grid_spec = pltpu.PrefetchScalarGridSpec(
        num_scalar_prefetch=1, grid=(nb, n_pages // gp), in_specs=in_specs,
        out_specs=pl.BlockSpec((None, rp, dv), lambda b, g, pt: (b, 0, 0)),
        scratch_shapes=[pltpu.VMEM((2, gp * PAGE_SIZE, kv_rank), F32), pltpu.VMEM((2, QK_ROPE, gp * PAGE_SIZE), F32),
                        pltpu.SemaphoreType.DMA((2,)),
                        pltpu.VMEM((n_heads * rp, QK_ROPE), BF16), pltpu.VMEM((n_heads * rp, 1), F32),
                        pltpu.VMEM((n_heads * rp, 1), F32), pltpu.VMEM((n_heads * rp, kv_rank), F32)])
    out = pl.pallas_call(
        kern, grid_spec=grid_spec, out_shape=jax.ShapeDtypeStruct((nb, rp, dv), F32),
        compiler_params=_params(("arbitrary", "arbitrary")), name="sample_attention",
    )(page_table, qb, kb, tsel, wuv, cache_ckv, cache_kpe_t)
    return out[:, :n_t]


def _odd_post_kernel(x_ref, yc_ref, yd_ref, wout_ref, g_ref, b_ref, o_ref, rl_sc, *, nb, tt):
    dc = yc_ref.shape[1]

    n_lane_tiles = rl_sc.shape[0]

    def body(b, carry):
        v = yd_ref[b]
        for c in range(n_lane_tiles):
            rl_sc[c, pl.ds(b, tt, stride=nb), :] = v[:, c * LANES:(c + 1) * LANES]
        return carry

    lax.fori_loop(0, nb, body, 0)
    yd = jnp.concatenate([rl_sc[c] for c in range(n_lane_tiles)], axis=1)
    m = _dot(yc_ref[...].astype(BF16), wout_ref[0:dc, :]) + _dot(yd.astype(BF16), wout_ref[dc:, :])
    o_ref[...] = _layer_norm(ALPHA * x_ref[...] + m, g_ref[...], b_ref[...])


def _odd_post(x, yc, yd, wout, g, b, *, nb, n_t):
    m_rows, d = x.shape
    dv = yd.shape[2]
    tt = _seq_tile(nb, n_t)
    tm = tt * nb
    row = lambda i: (i, 0)
    kern = functools.partial(_odd_post_kernel, nb=nb, tt=tt)
    return pl.pallas_call(
        kern, grid=(m_rows // tm,),
        in_specs=[pl.BlockSpec((tm, d), row), pl.BlockSpec((tm, yc.shape[1]), row),
                  pl.BlockSpec((nb, tt, dv), lambda i: (0, i, 0)),
                  _full(wout.shape), _full(g.shape), _full(b.shape)],
        out_specs=pl.BlockSpec((tm, d), row), out_shape=jax.ShapeDtypeStruct((m_rows, d), F32),
        scratch_shapes=[pltpu.VMEM((dv // LANES, tm, LANES), F32)],
        compiler_params=_params(("arbitrary",)), name="odd_post",
    )(x, yc, yd, wout, g, b)


def _token_tile(m_rows):
    tm = min(m_rows, TOKEN_TILE)
    assert m_rows % tm == 0
    return tm


def _router_kernel(x_ref, whi_ref, wlo_ref, br_ref, tri_ref, o_ref, cnt_ref, base_sc):
    tm = x_ref.shape[0]

    @pl.when(pl.program_id(0) == 0)
    def _():
        base_sc[...] = jnp.zeros(base_sc.shape, F32)

    x = x_ref[...]
    x_hi = x.astype(BF16)
    x_lo = (x - x_hi.astype(F32)).astype(BF16)
    logits = _dot(x_hi, whi_ref[...]) + (_dot(x_lo, whi_ref[...]) + _dot(x_hi, wlo_ref[...]))
    scores = _sigmoid(logits.T[0:N_EXPERTS, :])
    sel = scores + br_ref[...]

    def row(a, e):
        return a[e:e + 1, :]

    best_gs = None
    grp = None
    for gidx in range(N_GROUPS):
        v = [row(sel, gidx * EXPERTS_PER_GROUP + i) for i in range(EXPERTS_PER_GROUP)]
        gs = None
        for i in range(EXPERTS_PER_GROUP):
            for j in range(i + 1, EXPERTS_PER_GROUP):
                pair = v[i] + v[j]
                gs = pair if gs is None else jnp.maximum(gs, pair)
        if best_gs is None:
            best_gs, grp = gs, jnp.zeros(gs.shape, jnp.int32)
        else:
            better = gs > best_gs
            grp = jnp.where(better, gidx, grp)
            best_gs = jnp.where(better, gs, best_gs)

    def pick(a, i):
        out = row(a, i)
        for gidx in range(1, N_GROUPS):
            out = jnp.where(grp == gidx, row(a, gidx * EXPERTS_PER_GROUP + i), out)
        return out

    cand = [pick(sel, i) for i in range(EXPERTS_PER_GROUP)]
    cand_score = [pick(scores, i) for i in range(EXPERTS_PER_GROUP)]

    def argmax_first(vals, exclude=None):
        best, idx = None, None
        for i, v in enumerate(vals):
            if exclude is not None:
                v = jnp.where(exclude == i, NEG_INF, v)
            if best is None:
                best, idx = v, jnp.zeros(v.shape, jnp.int32)
            else:
                better = v > best
                idx = jnp.where(better, i, idx)
                best = jnp.where(better, v, best)
        return idx

    loc1 = argmax_first(cand)
    loc2 = argmax_first(cand, exclude=loc1)

    def take(vals, idx):
        out = vals[0]
        for i in range(1, len(vals)):
            out = jnp.where(idx == i, vals[i], out)
        return out

    g1 = take(cand_score, loc1)
    g2 = take(cand_score, loc2)
    gsum = g1 + g2
    first_is_a = loc1 < loc2
    a = jnp.minimum(loc1, loc2)
    b = jnp.maximum(loc1, loc2)
    pair = jnp.where(a == 0, b - 1, jnp.where(a == 1, b + 1, PAIRS_PER_GROUP - 1))
    bucket = grp * PAIRS_PER_GROUP + pair
    gate_a = jnp.where(first_is_a, g1, g2) / gsum
    gate_b = jnp.where(first_is_a, g2, g1) / gsum

    n_rows = base_sc.shape[0]
    hit = lax.broadcasted_iota(jnp.int32, (n_rows, tm), 0) == bucket
    ones = jnp.where(hit, 1.0, 0.0)
    before = _dot(ones.astype(BF16), tri_ref[...]) + base_sc[...]
    rank = jnp.sum(jnp.where(hit, before, 0.0), axis=0, keepdims=True)
    base_sc[...] = base_sc[...] + jnp.sum(ones, axis=1, keepdims=True)
    cnt_ref[...] = jnp.broadcast_to(base_sc[...], cnt_ref.shape)

    zero = jnp.zeros((1, tm), F32)
    o_ref[...] = jnp.concatenate([bucket.astype(F32), rank, gate_a, gate_b, zero, zero, zero, zero], axis=0)


def _router(x, whi, wlo, br):
    m_rows, d = x.shape
    tm = _token_tile(m_rows)
    tri = jnp.triu(jnp.ones((tm, tm), BF16), k=1)
    return pl.pallas_call(
        _router_kernel, grid=(m_rows // tm,),
        in_specs=[pl.BlockSpec((tm, d), lambda i: (i, 0)), _full(whi.shape), _full(wlo.shape), _full(br.shape),
                  _full(tri.shape)],
        out_specs=(pl.BlockSpec((SUBLANES, tm), lambda i: (0, i)), _full((N_BUCKETS, LANES))),
        out_shape=(jax.ShapeDtypeStruct((SUBLANES, m_rows), F32), jax.ShapeDtypeStruct((N_BUCKETS, LANES), F32)),
        scratch_shapes=[pltpu.VMEM((N_BUCKETS, 1), F32)],
        compiler_params=_params(("arbitrary",)), name="router",
    )(x, whi, wlo, br, tri)


def _row_copy(src_ref, src_row, dst_ref, dst_row, sem):
    return pltpu.make_async_copy(src_ref.at[pl.ds(src_row, 1), :], dst_ref.at[pl.ds(dst_row, 1), :], sem)


HI_HALF = 0xFFFF0000
ROW_DMA_UNROLL = 8


def _pack_bf16_pairs(x):
    n = x.shape[1] // 2
    lo = pltpu.bitcast(x[:, :n].astype(BF16).astype(F32), jnp.uint32)
    hi = pltpu.bitcast(x[:, n:].astype(BF16).astype(F32), jnp.uint32)
    return (lo >> 16) | (hi & jnp.uint32(HI_HALF))


def _unpack_bf16_pairs(p):
    lo = pltpu.bitcast(p << 16, F32).astype(BF16)
    hi = pltpu.bitcast(p & jnp.uint32(HI_HALF), F32).astype(BF16)
    return jnp.concatenate([lo, hi], axis=1)


def _dispatch_kernel(dest_ref, x_ref, gate_ref, buf_in_ref, buf_ref, xp_sc, sem):
    del buf_in_ref
    tm, d = x_ref.shape
    xp_sc[:, 0:d // 2] = _pack_bf16_pairs(x_ref[...])
    gate = gate_ref[...]
    lane = lax.broadcasted_iota(jnp.int32, (tm, LANES), 1)
    xp_sc[:, d // 2:] = pltpu.bitcast(jnp.where(lane == 0, gate[:, 0:1], jnp.where(lane == 1, gate[:, 1:2], 0.0)),
                                      jnp.uint32)

    def issue(i, carry):
        for u in range(ROW_DMA_UNROLL):
            r = i * ROW_DMA_UNROLL + u
            _row_copy(xp_sc, r, buf_ref, dest_ref[0, 0, r], sem).start(priority=u % 2)
        return carry

    lax.fori_loop(0, tm // ROW_DMA_UNROLL, issue, 0)
    pltpu.make_async_copy(xp_sc, buf_ref.at[pl.ds(0, tm), :], sem).wait()


def _dispatch(x, gates, dest_tiles, n_slots):
    m_rows, d = x.shape
    tm = dest_tiles.shape[2]
    assert tm % ROW_DMA_UNROLL == 0
    width = d // 2 + LANES
    buf0 = jnp.zeros((n_slots, width), jnp.uint32)
    return pl.pallas_call(
        _dispatch_kernel, grid=(m_rows // tm,),
        in_specs=[pl.BlockSpec((1, 1, tm), lambda i: (i, 0, 0), memory_space=pltpu.SMEM),
                  pl.BlockSpec((tm, d), lambda i: (i, 0)),
                  pl.BlockSpec((tm, 2), lambda i: (i, 0)),
                  pl.BlockSpec(memory_space=pl.ANY)],
        out_specs=pl.BlockSpec(memory_space=pl.ANY),
        out_shape=jax.ShapeDtypeStruct((n_slots, width), jnp.uint32),
        scratch_shapes=[pltpu.VMEM((tm, width), jnp.uint32), pltpu.SemaphoreType.DMA(())],
        input_output_aliases={3: 0},
        compiler_params=_params(("arbitrary",)), name="moe_dispatch",
    )(dest_tiles, x, gates, buf0)


def _expert_kernel(tea_ref, teb_ref, tv_ref, x_ref, win_a_ref, wout_a_ref, win_b_ref, wout_b_ref, o_ref):
    del tea_ref, teb_ref
    i = pl.program_id(0)
    de = wout_a_ref.shape[0]
    half = win_a_ref.shape[0] // 2

    @pl.when(tv_ref[i] != 0)
    def _():
        xb = _unpack_bf16_pairs(x_ref[:, 0:half])
        gate = pltpu.bitcast(x_ref[:, half:], F32)

        def expert(win_ref, wout_ref):
            h = _dot(xb, win_ref[...])
            hg = h[:, 0:de]
            act = hg * _sigmoid(hg) * h[:, de:2 * de]
            return _dot(act.astype(BF16), wout_ref[...])

        o_ref[...] = gate[:, 0:1] * expert(win_a_ref, wout_a_ref) + gate[:, 1:2] * expert(win_b_ref, wout_b_ref)

    @pl.when(tv_ref[i] == 0)
    def _():
        o_ref[...] = jnp.zeros(o_ref.shape, F32)


def _experts(buf, tile_ea, tile_eb, tile_valid, w_in, w_out, layer, tile_rows):
    n_slots, width = buf.shape
    d, de2 = w_in.shape[2], w_in.shape[3]
    grid_spec = pltpu.PrefetchScalarGridSpec(
        num_scalar_prefetch=3, grid=(n_slots // tile_rows,),
        in_specs=[pl.BlockSpec((tile_rows, width), lambda i, ea, eb, tv: (i, 0)),
                  pl.BlockSpec((None, None, d, de2), lambda i, ea, eb, tv: (layer, ea[i], 0, 0)),
                  pl.BlockSpec((None, None, de2 // 2, d), lambda i, ea, eb, tv: (layer, ea[i], 0, 0)),
                  pl.BlockSpec((None, None, d, de2), lambda i, ea, eb, tv: (layer, eb[i], 0, 0)),
                  pl.BlockSpec((None, None, de2 // 2, d), lambda i, ea, eb, tv: (layer, eb[i], 0, 0))],
        out_specs=pl.BlockSpec((tile_rows, d), lambda i, ea, eb, tv: (i, 0)))
    return pl.pallas_call(
        _expert_kernel, grid_spec=grid_spec, out_shape=jax.ShapeDtypeStruct((n_slots, d), F32),
        compiler_params=_params(("arbitrary",)), name="moe_experts",
    )(tile_ea, tile_eb, tile_valid, buf, w_in, w_out, w_in, w_out)


def _combine_kernel(dest_ref, dest_next_ref, x_ref, g_ref, b_ref, y_ref, o_ref, y_sc, sem):
    tm = x_ref.shape[0]
    i = pl.program_id(0)
    slot = i % 2

    def gather(idx_ref, sl):
        def issue(j, carry):
            for u in range(ROW_DMA_UNROLL):
                r = j * ROW_DMA_UNROLL + u
                _row_copy(y_ref, idx_ref[0, 0, r], y_sc.at[sl], r, sem.at[sl]).start(priority=u % 2)
            return carry

        lax.fori_loop(0, tm // ROW_DMA_UNROLL, issue, 0)

    @pl.when(i == 0)
    def _():
        gather(dest_ref, 0)

    @pl.when(i + 1 < pl.num_programs(0))
    def _():
        gather(dest_next_ref, 1 - slot)

    pltpu.make_async_copy(y_ref.at[pl.ds(0, tm), :], y_sc.at[slot], sem.at[slot]).wait()
    o_ref[...] = _layer_norm(ALPHA * x_ref[...] + y_sc[slot], g_ref[...], b_ref[...])


def _combine(x, dest_tiles, y, g, b):
    m_rows, d = x.shape
    tm = dest_tiles.shape[2]
    assert tm % ROW_DMA_UNROLL == 0
    last = m_rows // tm - 1
    return pl.pallas_call(
        _combine_kernel, grid=(m_rows // tm,),
        in_specs=[pl.BlockSpec((1, 1, tm), lambda i: (i, 0, 0), memory_space=pltpu.SMEM),
                  pl.BlockSpec((1, 1, tm), lambda i: (jnp.minimum(i + 1, last), 0, 0), memory_space=pltpu.SMEM),
                  pl.BlockSpec((tm, d), lambda i: (i, 0)),
                  _full(g.shape), _full(b.shape),
                  pl.BlockSpec(memory_space=pl.ANY)],
        out_specs=pl.BlockSpec((tm, d), lambda i: (i, 0)),
        out_shape=jax.ShapeDtypeStruct((m_rows, d), F32),
        scratch_shapes=[pltpu.VMEM((2, tm, d), F32), pltpu.SemaphoreType.DMA((2,))],
        compiler_params=_params(("arbitrary",)), name="moe_combine",
    )(dest_tiles, dest_tiles, x, g, b, y)


def _moe_layer(x, rw, w_in, w_out, layer, g, b, expert_tile):
    m_rows, _ = x.shape
    tm = _token_tile(m_rows)
    route, counts = _router(x, rw["whi"], rw["wlo"], rw["br"])
    bucket = route[0].astype(jnp.int32)
    rank = route[1].astype(jnp.int32)
    gates = route[2:4].T
    counts = counts[:, 0].astype(jnp.int32)
    padded = (counts + expert_tile - 1) // expert_tile * expert_tile
    pad_ends = jnp.cumsum(padded)
    pad_starts = pad_ends - padded
    b_range = jnp.arange(N_BUCKETS, dtype=jnp.int32)
    dest = jnp.sum(jnp.where(bucket[:, None] == b_range, pad_starts, 0), axis=-1) + rank
    n_tiles = -(-(m_rows + N_BUCKETS * (expert_tile - 1)) // expert_tile)
    tile_start = jnp.arange(n_tiles, dtype=jnp.int32) * expert_tile
    tile_bucket = jnp.minimum(jnp.sum((tile_start[:, None] >= pad_ends[None, :]).astype(jnp.int32), axis=1),
                              N_BUCKETS - 1)
    tile_valid = (tile_start < pad_ends[-1]).astype(jnp.int32)
    pair = tile_bucket % PAIRS_PER_GROUP
    group_base = tile_bucket // PAIRS_PER_GROUP * EXPERTS_PER_GROUP
    pair_range = jnp.arange(PAIRS_PER_GROUP, dtype=jnp.int32)
    tile_ea = group_base + jnp.sum(jnp.where(pair[:, None] == pair_range, jnp.asarray(PAIR_FIRST, jnp.int32), 0), axis=1)
    tile_eb = group_base + jnp.sum(jnp.where(pair[:, None] == pair_range, jnp.asarray(PAIR_SECOND, jnp.int32), 0), axis=1)
    dest_tiles = dest.reshape(m_rows // tm, 1, tm)
    buf = _dispatch(x, gates, dest_tiles, n_tiles * expert_tile)
    y = _experts(buf, tile_ea, tile_eb, tile_valid, w_in, w_out, layer, expert_tile)
    return _combine(x, dest_tiles, y, g, b)


def _block_diag(w):
    h, n, _ = w.shape
    eye = jnp.eye(h, dtype=w.dtype)
    return (eye[:, None, :, None] * w[:, :, None, :]).reshape(h * n, h * n)


def _pair_blocks(w):
    h, a, b = w.shape
    w = w.reshape(h // 2, 2, a, b)
    eye = jnp.eye(2, dtype=w.dtype)
    return (eye[None, :, None, :, None] * w[:, :, :, None, :]).reshape(h // 2, 2 * a, 2 * b)


def _time_major(a):
    b, t, c = a.shape
    return a.transpose(1, 0, 2).reshape(t * b, c)


def _batch_major(a, nb):
    return a.reshape(-1, nb, a.shape[-1]).transpose(1, 0, 2)


def _rope_tables(pos):
    half = QK_ROPE // 2
    inv = ROPE_THETA ** (-jnp.arange(half, dtype=F32) / half)
    ang = pos.astype(F32)[:, None] * inv
    cos, sin = jnp.cos(ang), jnp.sin(ang)
    cos_t = jnp.tile(jnp.concatenate([cos, cos], axis=1), (1, N_HEADS))
    sin_t = jnp.tile(jnp.concatenate([-sin, sin], axis=1), (1, N_HEADS))
    return cos_t, sin_t


def kernel(x_prompt, x_sample, state_conv_a, state_conv_b, state_rglru_h, state_conv_c, cache_ckv, cache_kpe, page_table, w_in_even, conv_a_w, conv_b_w, conv_b_b, lru_wa, lru_ba, lru_wx, lru_bx, lru_lambda, w_out_even, w_in_odd, conv_c_w, conv_c_b, ln_c_g, ln_c_b, q_norm_g, w_q_b, kv_norm_g, w_uk, w_uv, w_out_odd, ln_mix_g, ln_mix_b, ln_ffn_g, ln_ffn_b, w_router, b_router, w_exp_in, w_exp_out):
    bp, n_tp, d = x_prompt.shape
    bs, n_ts, _ = x_sample.shape
    past_len = page_table.shape[1] * PAGE_SIZE
    dc = conv_c_w.shape[2]
    q_rank = q_norm_g.shape[1]
    kv_rank = kv_norm_g.shape[1]

    xp = _time_major(x_prompt)
    xs = _time_major(x_sample)
    row2 = lambda v: v.reshape(1, -1)

    wr = jnp.pad(w_router, ((0, 0), (0, LANES - N_EXPERTS)))
    wr_hi = wr.astype(BF16)
    router_w = {"whi": wr_hi, "wlo": (wr - wr_hi.astype(F32)).astype(BF16), "br": b_router.reshape(N_EXPERTS, 1)}
    rope_p = _rope_tables(jnp.arange(n_tp))
    rope_s = _rope_tables(past_len + jnp.arange(n_ts))
    we_in, we_out = w_exp_in.astype(BF16), w_exp_out.astype(BF16)
    cache_kpe_t = jnp.swapaxes(cache_kpe, 2, 3)

    outs = {k: [] for k in ("ca_p", "ca_s", "cb_p", "cb_s", "h_p", "h_s", "cc_p", "cc_s", "ckv_p", "ckv_s", "kpe_p", "kpe_s")}
    for l in range(DEPTH):
        j = l // 2
        lg, lb = row2(ln_mix_g[l]), row2(ln_mix_b[l])
        if l % 2 == 0:
            wa_bd, wx_bd = _block_diag(lru_wa[j]), _block_diag(lru_wx[j])
            hc = wa_bd.shape[0] // 2
            wg = jnp.stack([jnp.concatenate([wa_bd[s * hc:(s + 1) * hc, s * hc:(s + 1) * hc],
                                             wx_bd[s * hc:(s + 1) * hc, s * hc:(s + 1) * hc]], axis=1)
                            for s in range(2)]).astype(BF16)
            w = {"win": w_in_even[j].astype(BF16), "caw": conv_a_w[j], "cbw": conv_b_w[j], "cbb": row2(conv_b_b[j]),
                 "wg": wg, "ba": row2(lru_ba[j]), "bx": row2(lru_bx[j]), "lam": row2(lru_lambda[j]),
                 "wout": w_out_even[j].astype(BF16), "g": lg, "b": lb}
            da = conv_a_w.shape[2]
            db = conv_b_w.shape[2]
            xp, a1, b1, h1 = _even_layer(xp, w, jnp.zeros((2 * bp, da), F32), jnp.zeros((3 * bp, db), F32),
                                         jnp.zeros((bp, db), F32), nb=bp, n_t=n_tp)
            xs, a2, b2, h2 = _even_layer(xs, w, _time_major(state_conv_a[j]), _time_major(state_conv_b[j]),
                                         state_rglru_h[j], nb=bs, n_t=n_ts)
            outs["ca_p"].append(_batch_major(a1, bp)); outs["ca_s"].append(_batch_major(a2, bs))
            outs["cb_p"].append(_batch_major(b1, bp)); outs["cb_s"].append(_batch_major(b2, bs))
            outs["h_p"].append(h1); outs["h_s"].append(h2)
        else:
            nope_all = N_HEADS * QK_NOPE
            wq = w_q_b[j].reshape(q_rank, N_HEADS, QK_NOPE + QK_ROPE)
            wqb = jnp.concatenate([wq[:, :, :QK_NOPE].reshape(q_rank, nope_all),
                                   wq[:, :, QK_NOPE:].reshape(q_rank, N_HEADS * QK_ROPE)], axis=1).astype(BF16)
            o_pe = 2 * dc + q_rank + kv_rank
            win = jnp.concatenate([w_in_odd[j][:, :o_pe], jnp.tile(w_in_odd[j][:, o_pe:], (1, N_HEADS))], axis=1).astype(BF16)
            wuk = _pair_blocks(w_uk[j].transpose(1, 2, 0)).astype(BF16)
            wuv = _pair_blocks(w_uv[j].transpose(1, 0, 2)).astype(BF16)
            w = {"win": win, "ccw": conv_c_w[j], "ccb": row2(conv_c_b[j]), "lcg": row2(ln_c_g[j]), "lcb": row2(ln_c_b[j]),
                 "qg": row2(q_norm_g[j]), "wqb": wqb, "kvg": row2(kv_norm_g[j]), "wuk": wuk}
            width = conv_c_w.shape[1]
            wout = w_out_odd[j].astype(BF16)
            ycp, qp, kp, ckv1, kpe1, c1 = _odd_pre(xp, w, jnp.zeros(((width - 1) * bp, dc), F32), *rope_p, nb=bp, n_t=n_tp)
            ycs, qs, ks, ckv2, kpe2, c2 = _odd_pre(xs, w, _time_major(state_conv_c[j]), *rope_s, nb=bs, n_t=n_ts)
            ydp = _prompt_attention(qp, kp, wuv)
            yds = _sample_attention(qs, ks, wuv, cache_ckv, cache_kpe_t, page_table, j)
            xp = _odd_post(xp, ycp, ydp, wout, lg, lb, nb=bp, n_t=n_tp)
            xs = _odd_post(xs, ycs, yds, wout, lg, lb, nb=bs, n_t=n_ts)
            outs["cc_p"].append(_batch_major(c1, bp)); outs["cc_s"].append(_batch_major(c2, bs))
            outs["ckv_p"].append(ckv1); outs["ckv_s"].append(ckv2)
            outs["kpe_p"].append(kpe1); outs["kpe_s"].append(kpe2)
        fg, fb = row2(ln_ffn_g[l]), row2(ln_ffn_b[l])
        xp = _moe_layer(xp, router_w, we_in, we_out, l, fg, fb, expert_tile=512)
        xs = _moe_layer(xs, router_w, we_in, we_out, l, fg, fb, expert_tile=64)

    y_prompt = _batch_major(xp, bp)
    y_sample = _batch_major(xs, bs)
    st = lambda k: jnp.stack(outs[k])
    st1 = lambda k: jnp.stack(outs[k], axis=1)
    return (y_prompt, y_sample, st("ca_p"), st("ca_s"), st("cb_p"), st("cb_s"), st("h_p"), st("h_s"),
            st("cc_p"), st("cc_s"), st1("ckv_p"), st1("ckv_s"), st1("kpe_p"), st1("kpe_s"))
```

```python
import functools

import jax
import jax.numpy as jnp
from jax import lax
from jax.experimental import pallas as pl
from jax.experimental.pallas import tpu as pltpu

F32 = jnp.float32
BF16 = jnp.bfloat16

DEPTH = 4
N_HEADS = 8
QK_NOPE = 64
QK_ROPE = 32
V_DIM = 64
N_EXPERTS = 16
N_GROUPS = 4
EXPERTS_PER_GROUP = 4
PAIRS_PER_GROUP = 6
N_BUCKETS = N_GROUPS * PAIRS_PER_GROUP
PAIR_FIRST = (0, 0, 0, 1, 1, 2)
PAIR_SECOND = (1, 2, 3, 2, 3, 3)
LRU_C = 8.0
ROPE_THETA = 10000.0
ATTN_SCALE = (QK_NOPE + QK_ROPE) ** -0.5
ALPHA = (2 * DEPTH) ** 0.25
PAGE_SIZE = 128

SUBLANES = 8
BF16_SUBLANES = 16
LANES = 128
VMEM_LIMIT_BYTES = 56 * 1024 * 1024

TIME_STEPS_PER_TILE = 32
TOKEN_TILE = 512
MOE_ROW_TILE = 1024
ATTN_TILE = 512
PAGES_PER_STEP = 64
NEG_INF = float("-inf")


def _params(semantics):
    return pltpu.CompilerParams(dimension_semantics=semantics, vmem_limit_bytes=VMEM_LIMIT_BYTES)


def _full(shape):
    nd = len(shape)
    return pl.BlockSpec(shape, lambda *_: (0,) * nd)


def _layer_norm(x, g, b, eps=1e-5):
    mu = jnp.mean(x, axis=-1, keepdims=True)
    xc = x - mu
    var = jnp.mean(xc * xc, axis=-1, keepdims=True)
    return xc * lax.rsqrt(var + eps) * g + b


def _rms_norm(x, g, eps=1e-6):
    return x * lax.rsqrt(jnp.mean(x * x, axis=-1, keepdims=True) + eps) * g


def _sigmoid(x):
    return 1.0 / (1.0 + jnp.exp(-x))


def _dot(a, b):
    return jnp.dot(a, b, preferred_element_type=F32)


def _dot_nt(a, b):
    return lax.dot_general(a, b, (((1,), (1,)), ((), ())), preferred_element_type=F32)


def _seq_tile(nb, n_t):
    tt = min(n_t, TIME_STEPS_PER_TILE)
    assert n_t % tt == 0 and nb % SUBLANES == 0
    return tt


def _even_kernel(x_ref, win_ref, caw_ref, cbw_ref, cbb_ref, wg_ref, ba_ref, bx_ref, lam_ref, wout_ref,
                 g_ref, b_ref, sa_ref, sb_ref, h0_ref,
                 o_ref, na_ref, nb_ref, hl_ref,
                 ua_ext, vb_ext, h_sc, hs_sc, *, nb, tt):
    tm = tt * nb
    dh = caw_ref.shape[1]
    half = dh // 2

    @pl.when(pl.program_id(0) == 0)
    def _():
        ua_ext[0:2 * nb, :] = sa_ref[...]
        vb_ext[0:3 * nb, :] = sb_ref[...]
        h_sc[...] = h0_ref[...]

    x = x_ref[...]
    xb = x.astype(BF16)

    def proj(j):
        return _dot(xb, win_ref[:, j * dh:(j + 1) * dh])

    ua_ext[2 * nb:2 * nb + tm, :] = proj(1) * proj(2)
    caw = caw_ref[...]
    conv_a = (caw[0:1] * ua_ext[0:tm, :] + caw[1:2] * ua_ext[nb:nb + tm, :]
              + caw[2:3] * ua_ext[2 * nb:2 * nb + tm, :])
    y_a = proj(0) * conv_a
    tail_a = ua_ext[tm:tm + 2 * nb, :]
    na_ref[...] = tail_a
    ua_ext[0:2 * nb, :] = tail_a

    vb_ext[3 * nb:3 * nb + tm, :] = proj(4)
    cbw = cbw_ref[...]
    u_b = (cbb_ref[...] + cbw[0:1] * vb_ext[0:tm, :] + cbw[1:2] * vb_ext[nb:nb + tm, :]
           + cbw[2:3] * vb_ext[2 * nb:2 * nb + tm, :] + cbw[3:4] * vb_ext[3 * nb:3 * nb + tm, :])
    tail_b = vb_ext[tm:tm + 3 * nb, :]
    nb_ref[...] = tail_b
    vb_ext[0:3 * nb, :] = tail_b

    ub16 = u_b.astype(BF16)
    gk0 = _dot(ub16[:, :half], wg_ref[0])
    gk1 = _dot(ub16[:, half:], wg_ref[1])
    r = _sigmoid(jnp.concatenate([gk0[:, :half], gk1[:, :half]], axis=1) + ba_ref[...])
    ig = _sigmoid(jnp.concatenate([gk0[:, half:], gk1[:, half:]], axis=1) + bx_ref[...])
    nlam = -lam_ref[...]
    softplus = jnp.maximum(nlam, 0.0) + jnp.log(1.0 + jnp.exp(-jnp.abs(nlam)))
    log_a = (-LRU_C * r) * softplus
    a = jnp.exp(log_a)
    bterm = jnp.sqrt(1.0 - a * a) * (ig * u_b)

    h = h_sc[...]
    for t in range(tt):
        h = a[t * nb:(t + 1) * nb, :] * h + bterm[t * nb:(t + 1) * nb, :]
        hs_sc[t * nb:(t + 1) * nb, :] = h
    h_sc[...] = h
    hl_ref[...] = h
    y_b = jax.nn.gelu(proj(3), approximate=True) * hs_sc[...]

    m = _dot(y_a.astype(BF16), wout_ref[0:dh, :]) + _dot(y_b.astype(BF16), wout_ref[dh:2 * dh, :])
    o_ref[...] = _layer_norm(ALPHA * x + m, g_ref[...], b_ref[...])


def _even_layer(x, w, sa, sb, h0, *, nb, n_t):
    m_rows, d = x.shape
    dh = w["caw"].shape[1]
    tt = _seq_tile(nb, n_t)
    tm = tt * nb
    kern = functools.partial(_even_kernel, nb=nb, tt=tt)
    row = lambda i: (i, 0)
    in_specs = [pl.BlockSpec((tm, d), row), _full(w["win"].shape), _full(w["caw"].shape), _full(w["cbw"].shape),
                _full(w["cbb"].shape), _full(w["wg"].shape), _full(w["ba"].shape), _full(w["bx"].shape),
                _full(w["lam"].shape), _full(w["wout"].shape), _full(w["g"].shape), _full(w["b"].shape),
                _full(sa.shape), _full(sb.shape), _full(h0.shape)]
    out_shape = (jax.ShapeDtypeStruct((m_rows, d), F32), jax.ShapeDtypeStruct(sa.shape, F32),
                 jax.ShapeDtypeStruct(sb.shape, F32), jax.ShapeDtypeStruct(h0.shape, F32))
    out_specs = (pl.BlockSpec((tm, d), row), _full(sa.shape), _full(sb.shape), _full(h0.shape))
    scratch = [pltpu.VMEM((tm + 2 * nb, dh), F32), pltpu.VMEM((tm + 3 * nb, dh), F32),
               pltpu.VMEM((nb, dh), F32), pltpu.VMEM((tm, dh), F32)]
    return pl.pallas_call(
        kern, grid=(m_rows // tm,), in_specs=in_specs, out_specs=out_specs, out_shape=out_shape,
        scratch_shapes=scratch, compiler_params=_params(("arbitrary",)), name="even_mixer",
    )(x, w["win"], w["caw"], w["cbw"], w["cbb"], w["wg"], w["ba"], w["bx"], w["lam"], w["wout"], w["g"], w["b"],
      sa, sb, h0)


def _odd_pre_kernel(x_ref, win_ref, ccw_ref, ccb_ref, lcg_ref, lcb_ref, qg_ref, wqb_ref, kvg_ref, wuk_ref,
                    cos_ref, sin_ref, sc_ref, perm_ref,
                    yc_ref, q_ref, k_ref, ckv_ref, kpe_ref, nc_ref,
                    c_ext, rl_sc, *, nb, tt, width):
    tm = tt * nb
    dc = ccw_ref.shape[1]
    q_rank = qg_ref.shape[1]
    kv_rank = kvg_ref.shape[1]
    pe_all = N_HEADS * QK_ROPE
    hist = (width - 1) * nb

    @pl.when(pl.program_id(0) == 0)
    def _():
        c_ext[0:hist, :] = sc_ref[...]

    xb = x_ref[...].astype(BF16)
    o_q = 2 * dc
    o_kv = o_q + q_rank
    o_pe = o_kv + kv_rank

    glu = _dot(xb, win_ref[:, 0:dc]) * _sigmoid(_dot(xb, win_ref[:, dc:2 * dc]))
    c_ext[hist:hist + tm, :] = glu
    ccw = ccw_ref[...]
    u_c = ccb_ref[...] + ccw[0:1] * c_ext[0:tm, :]
    for k in range(1, width):
        u_c = u_c + ccw[k:k + 1] * c_ext[k * nb:k * nb + tm, :]
    tail = c_ext[tm:tm + hist, :]
    nc_ref[...] = tail
    c_ext[0:hist, :] = tail
    ln = _layer_norm(u_c, lcg_ref[...], lcb_ref[...])
    yc_ref[...] = ln * _sigmoid(ln)

    def to_sequences(val, store):
        n_lane_tiles = val.shape[1] // LANES
        for c in range(n_lane_tiles):
            rl_sc[c] = val[:, c * LANES:(c + 1) * LANES]

        def body(b, carry):
            store(b, jnp.concatenate([rl_sc[c, pl.ds(b, tt, stride=nb), :] for c in range(n_lane_tiles)], axis=1))
            return carry

        lax.fori_loop(0, nb, body, 0)

    cos = jnp.broadcast_to(cos_ref[...][:, None, :], (tt, nb, pe_all)).reshape(tm, pe_all)
    sin = jnp.broadcast_to(sin_ref[...][:, None, :], (tt, nb, pe_all)).reshape(tm, pe_all)
    lane = lax.broadcasted_iota(jnp.int32, (1, pe_all), 1)
    first_half = (lane % QK_ROPE) < (QK_ROPE // 2)
    head_of_lane = lane // QK_ROPE

    def rope(v):
        swapped = jnp.where(first_half, pltpu.roll(v, pe_all - QK_ROPE // 2, 1), pltpu.roll(v, QK_ROPE // 2, 1))
        return v * cos + swapped * sin

    permute_on_mxu = tt % BF16_SUBLANES == 0

    def to_sequences_bf16(val, store):
        if not permute_on_mxu:
            to_sequences(val, lambda b, v: store(b, v.astype(BF16)))
            return
        pv = _dot(perm_ref[...], val.astype(BF16)).astype(BF16)
        for b in range(nb):
            store(b, pv[b * tt:(b + 1) * tt])

    ckv = _rms_norm(_dot(xb, win_ref[:, o_kv:o_kv + kv_rank]), kvg_ref[...])
    kpe = rope(_dot(xb, win_ref[:, o_pe:o_pe + pe_all]))
    k_all = jnp.concatenate([ckv, kpe], axis=1)

    def store_k32(b, v):
        ckv_ref[b] = v[:, 0:kv_rank]
        kpe_ref[b] = v[:, kv_rank:kv_rank + QK_ROPE]

    def store_k16(b, v):
        k_ref[b] = v

    def store_k(b, v):
        store_k32(b, v)
        store_k16(b, v.astype(BF16))

    if permute_on_mxu:
        to_sequences(k_all, store_k32)
        to_sequences_bf16(k_all, store_k16)
    else:
        to_sequences(k_all, store_k)

    qn = _rms_norm(_dot(xb, win_ref[:, o_q:o_q + q_rank]), qg_ref[...]).astype(BF16)
    nope_all = N_HEADS * QK_NOPE
    q_nope = (_dot(qn, wqb_ref[:, 0:nope_all]) * ATTN_SCALE).astype(BF16)
    q_pe = rope(_dot(qn, wqb_ref[:, nope_all:nope_all + pe_all]) * ATTN_SCALE)
    for p in range(N_HEADS // 2):
        lat2 = _dot(q_nope[:, p * 2 * QK_NOPE:(p + 1) * 2 * QK_NOPE], wuk_ref[p])
        for s in range(2):
            h = 2 * p + s

            def store_q(b, v, h=h):
                q_ref[b, h] = v

            to_sequences_bf16(jnp.concatenate([lat2[:, s * kv_rank:(s + 1) * kv_rank],
                                               jnp.where(head_of_lane == h, q_pe, 0.0)], axis=1), store_q)


def _odd_pre(x, w, sc, cos, sin, *, nb, n_t):
    m_rows, d = x.shape
    width, dc = w["ccw"].shape
    kv_rank = w["kvg"].shape[1]
    pe_all = N_HEADS * QK_ROPE
    kq = kv_rank + pe_all
    tt = _seq_tile(nb, n_t)
    tm = tt * nb
    kern = functools.partial(_odd_pre_kernel, nb=nb, tt=tt, width=width)
    row = lambda i: (i, 0)
    seq = lambda i: (0, i, 0)
    in_specs = [pl.BlockSpec((tm, d), row)] + [_full(w[k].shape) for k in
                                                ("win", "ccw", "ccb", "lcg", "lcb", "qg", "wqb", "kvg", "wuk")]
    out_row = jnp.arange(tm, dtype=jnp.int32)
    src_row = (out_row % tt) * nb + out_row // tt
    perm = (src_row[:, None] == out_row[None, :]).astype(BF16)
    in_specs += [pl.BlockSpec((tt, pe_all), row), pl.BlockSpec((tt, pe_all), row), _full(sc.shape), _full(perm.shape)]
    out_shape = (jax.ShapeDtypeStruct((m_rows, dc), F32),
                 jax.ShapeDtypeStruct((nb, N_HEADS, n_t, kq), BF16),
                 jax.ShapeDtypeStruct((nb, n_t, kq), BF16),
                 jax.ShapeDtypeStruct((nb, n_t, kv_rank), F32),
                 jax.ShapeDtypeStruct((nb, n_t, QK_ROPE), F32),
                 jax.ShapeDtypeStruct(sc.shape, F32))
    out_specs = (pl.BlockSpec((tm, dc), row), pl.BlockSpec((nb, N_HEADS, tt, kq), lambda i: (0, 0, i, 0)),
                 pl.BlockSpec((nb, tt, kq), seq), pl.BlockSpec((nb, tt, kv_rank), seq),
                 pl.BlockSpec((nb, tt, QK_ROPE), seq), _full(sc.shape))
    scratch = [pltpu.VMEM((tm + (width - 1) * nb, dc), F32), pltpu.VMEM((kq // LANES, tm, LANES), F32)]
    return pl.pallas_call(
        kern, grid=(m_rows // tm,), in_specs=in_specs, out_specs=out_specs, out_shape=out_shape,
        scratch_shapes=scratch, compiler_params=_params(("arbitrary",)), name="odd_pre",
    )(x, w["win"], w["ccw"], w["ccb"], w["lcg"], w["lcb"], w["qg"], w["wqb"], w["kvg"], w["wuk"], cos, sin, sc, perm)


def _softmax_step(s, v16, m_sc, l_sc, acc_sc):
    m_prev = m_sc[...]
    m_new = jnp.maximum(m_prev, jnp.max(s, axis=-1, keepdims=True))
    alpha = jnp.exp(m_prev - m_new)
    p = jnp.exp(s - m_new)
    l_sc[...] = alpha * l_sc[...] + jnp.sum(p, axis=-1, keepdims=True)
    acc_sc[...] = alpha * acc_sc[...] + _dot(p.astype(BF16), v16)
    m_sc[...] = m_new


def _softmax_init(m_sc, l_sc, acc_sc):
    m_sc[...] = jnp.full(m_sc.shape, NEG_INF, F32)
    l_sc[...] = jnp.zeros(l_sc.shape, F32)
    acc_sc[...] = jnp.zeros(acc_sc.shape, F32)


def _prompt_attn_kernel(qi_ref, ki_ref, q_ref, k_ref, bias_ref, wuv_ref, o_ref, *state, kv_rank):
    m_sc, l_sc, acc_sc = state[0:N_HEADS], state[N_HEADS:2 * N_HEADS], state[2 * N_HEADS:3 * N_HEADS]
    j = pl.program_id(1)
    qi = qi_ref[j]
    ki = ki_ref[j]

    @pl.when(ki == 0)
    def _():
        for h in range(N_HEADS):
            _softmax_init(m_sc[h], l_sc[h], acc_sc[h])

    k = k_ref[...]
    v_t = k[:, 0:kv_rank].T
    diag = (ki == qi).astype(jnp.int32)

    def scores_t(h):
        return _dot_nt(k, q_ref[h]) + bias_ref[diag]

    s_next = scores_t(0)
    for h in range(N_HEADS):
        s = s_next
        if h + 1 < N_HEADS:
            s_next = scores_t(h + 1)
        m_prev = m_sc[h][...]
        m_new = jnp.maximum(m_prev, jnp.max(s, axis=0, keepdims=True))
        alpha = jnp.exp(m_prev - m_new)
        p = jnp.exp(s - m_new)
        l_sc[h][...] = alpha * l_sc[h][...] + jnp.sum(p, axis=0, keepdims=True)
        acc_sc[h][...] = alpha * acc_sc[h][...] + _dot(v_t, p.astype(BF16))
        m_sc[h][...] = m_new

    @pl.when(ki == qi)
    def _():
        for p in range(N_HEADS // 2):
            o2_t = jnp.concatenate([acc_sc[2 * p][...] / l_sc[2 * p][...],
                                    acc_sc[2 * p + 1][...] / l_sc[2 * p + 1][...]], axis=0)
            o_ref[:, p * 2 * V_DIM:(p + 1) * 2 * V_DIM] = _dot(o2_t.T.astype(BF16), wuv_ref[p])


def _prompt_attention(q, k, wuv):
    nb, n_heads, n_t, kq = q.shape
    kv_rank = wuv.shape[1] // 2
    dv = N_HEADS * V_DIM
    tile = min(n_t, ATTN_TILE)
    assert n_t % tile == 0
    nq = n_t // tile
    pairs = [(a, b) for a in range(nq) for b in range(a + 1)]
    qi_tab = jnp.asarray([p[0] for p in pairs], jnp.int32)
    ki_tab = jnp.asarray([p[1] for p in pairs], jnp.int32)
    causal = jnp.where(jnp.arange(tile)[:, None] <= jnp.arange(tile)[None, :], 0.0, NEG_INF).astype(F32)
    bias = jnp.stack([jnp.zeros((tile, tile), F32), causal])
    kern = functools.partial(_prompt_attn_kernel, kv_rank=kv_rank)
    grid_spec = pltpu.PrefetchScalarGridSpec(
        num_scalar_prefetch=2, grid=(nb, len(pairs)),
        in_specs=[pl.BlockSpec((None, n_heads, tile, kq), lambda b, j, qi, ki: (b, 0, qi[j], 0)),
                  pl.BlockSpec((None, tile, kq), lambda b, j, qi, ki: (b, ki[j], 0)),
                  pl.BlockSpec(bias.shape, lambda b, j, qi, ki: (0, 0, 0)),
                  pl.BlockSpec(wuv.shape, lambda b, j, qi, ki: (0, 0, 0))],
        out_specs=pl.BlockSpec((None, tile, dv), lambda b, j, qi, ki: (b, qi[j], 0)),
        scratch_shapes=([pltpu.VMEM((1, tile), F32)] * (2 * n_heads) + [pltpu.VMEM((kv_rank, tile), F32)] * n_heads))
    return pl.pallas_call(
        kern, grid_spec=grid_spec, out_shape=jax.ShapeDtypeStruct((nb, n_t, dv), F32),
        compiler_params=_params(("arbitrary", "arbitrary")), name="prompt_attention",
    )(qi_tab, ki_tab, q, k, bias, wuv)


def _sample_attn_kernel(pt_ref, q_ref, kn_ref, tsel_ref, wuv_ref, ckv_hbm, kpe_hbm, o_ref,
                        ck_buf, kp_buf, sem, qc_sc, m_sc, l_sc, acc_sc, *,
                        n_pages_step, layer, kv_rank, n_new, rows_per_head):
    b = pl.program_id(0)
    g = pl.program_id(1)
    n_g = pl.num_programs(1)
    step = b * n_g + g
    n_steps = pl.num_programs(0) * n_g
    slot = step % 2

    def page_copies(seq, grp, sl):
        copies = []
        for i in range(n_pages_step):
            page = pt_ref[seq, grp * n_pages_step + i]
            copies.append(pltpu.make_async_copy(
                ckv_hbm.at[page, layer], ck_buf.at[sl, pl.ds(i * PAGE_SIZE, PAGE_SIZE), :], sem.at[sl]))
            copies.append(pltpu.make_async_copy(
                kpe_hbm.at[page, layer], kp_buf.at[sl, :, pl.ds(i * PAGE_SIZE, PAGE_SIZE)], sem.at[sl]))
        return copies

    @pl.when(step == 0)
    def _():
        for c in page_copies(0, 0, 0):
            c.start()

    @pl.when(step + 1 < n_steps)
    def _():
        nxt = step + 1
        for c in page_copies(nxt // n_g, nxt % n_g, 1 - slot):
            c.start()

    @pl.when(g == 0)
    def _():
        _softmax_init(m_sc, l_sc, acc_sc)
        qc_sc[...] = _dot(q_ref[:, kv_rank:], tsel_ref[...]).astype(BF16)

    for c in page_copies(b, g, slot):
        c.wait()

    q = q_ref[...]
    ck = ck_buf[slot].astype(BF16)
    kp_t = kp_buf[slot].astype(BF16)
    s = _dot_nt(q[:, 0:kv_rank], ck) + _dot(qc_sc[...], kp_t)
    _softmax_step(s, ck, m_sc, l_sc, acc_sc)

    @pl.when(g == pl.num_programs(1) - 1)
    def _():
        kn = kn_ref[...]
        s_new = _dot_nt(q, kn)
        t_row = lax.broadcasted_iota(jnp.int32, s_new.shape, 0) % rows_per_head
        t_col = lax.broadcasted_iota(jnp.int32, s_new.shape, 1)
        s_new = jnp.where((t_col <= t_row) & (t_col < n_new), s_new, NEG_INF)
        _softmax_step(s_new, kn[:, 0:kv_rank], m_sc, l_sc, acc_sc)
        o = acc_sc[...] / l_sc[...]
        rp = rows_per_head
        for p in range(N_HEADS // 2):
            o2 = jnp.concatenate([o[2 * p * rp:(2 * p + 1) * rp], o[(2 * p + 1) * rp:(2 * p + 2) * rp]], axis=1)
            o_ref[:, p * 2 * V_DIM:(p + 1) * 2 * V_DIM] = _dot(o2.astype(BF16), wuv_ref[p])


def _sample_attention(q, k, wuv, cache_ckv, cache_kpe_t, page_table, layer):
    nb, n_heads, n_t, kq = q.shape
    kv_rank = wuv.shape[1] // 2
    dv = N_HEADS * V_DIM
    n_pages = page_table.shape[1]
    gp = min(PAGES_PER_STEP, n_pages)
    assert n_pages % gp == 0 and n_t <= SUBLANES
    rp = SUBLANES
    qb = jnp.pad(q, ((0, 0), (0, 0), (0, rp - n_t), (0, 0))).reshape(nb, n_heads * rp, kq)
    kb = jnp.pad(k, ((0, 0), (0, rp - n_t), (0, 0)))
    tsel = jnp.tile(jnp.eye(QK_ROPE, dtype=BF16), (N_HEADS, 1))
    kern = functools.partial(_sample_attn_kernel, n_pages_step=gp, layer=layer, kv_rank=kv_rank, n_new=n_t,
                             rows_per_head=rp)
    in_specs = [pl.BlockSpec((None, n_heads * rp, kq), lambda b, g, pt: (b, 0, 0)),
                pl.BlockSpec((None, rp, kq), lambda b, g, pt: (b, 0, 0)),
                pl.BlockSpec(tsel.shape, lambda b, g, pt: (0, 0)),
                pl.BlockSpec(wuv.shape, lambda b, g, pt: (0, 0, 0)),
                pl.BlockSpec(memory_space=pl.ANY), pl.BlockSpec(memory_space=pl.ANY)]
    grid_spec = pltpu.PrefetchScalarGridSpec(
        num_scalar_prefetch=1, grid=(nb, n_pages // gp), in_specs=in_specs,
        out_specs=pl.BlockSpec((None, rp, dv), lambda b, g, pt: (b, 0, 0)),
        scratch_shapes=[pltpu.VMEM((2, gp * PAGE_SIZE, kv_rank), F32), pltpu.VMEM((2, QK_ROPE, gp * PAGE_SIZE), F32),
                        pltpu.SemaphoreType.DMA((2,)),
                        pltpu.VMEM((n_heads * rp, QK_ROPE), BF16), pltpu.VMEM((n_heads * rp, 1), F32),
                        pltpu.VMEM((n_heads * rp, 1), F32), pltpu.VMEM((n_heads * rp, kv_rank), F32)])
    out = pl.pallas_call(
        kern, grid_spec=grid_spec, out_shape=jax.ShapeDtypeStruct((nb, rp, dv), F32),
        compiler_params=_params(("arbitrary", "arbitrary")), name="sample_attention",
    )(page_table, qb, kb, tsel, wuv, cache_ckv, cache_kpe_t)
    return out[:, :n_t]


def _odd_post_kernel(x_ref, yc_ref, yd_ref, wout_ref, g_ref, b_ref, o_ref, rl_sc, *, nb, tt):
    dc = yc_ref.shape[1]

    n_lane_tiles = rl_sc.shape[0]

    def body(b, carry):
        v = yd_ref[b]
        for c in range(n_lane_tiles):
            rl_sc[c, pl.ds(b, tt, stride=nb), :] = v[:, c * LANES:(c + 1) * LANES]
        return carry

    lax.fori_loop(0, nb, body, 0)
    yd = jnp.concatenate([rl_sc[c] for c in range(n_lane_tiles)], axis=1)
    m = _dot(yc_ref[...].astype(BF16), wout_ref[0:dc, :]) + _dot(yd.astype(BF16), wout_ref[dc:, :])
    o_ref[...] = _layer_norm(ALPHA * x_ref[...] + m, g_ref[...], b_ref[...])


def _odd_post(x, yc, yd, wout, g, b, *, nb, n_t):
    m_rows, d = x.shape
    dv = yd.shape[2]
    tt = _seq_tile(nb, n_t)
    tm = tt * nb
    row = lambda i: (i, 0)
    kern = functools.partial(_odd_post_kernel, nb=nb, tt=tt)
    return pl.pallas_call(
        kern, grid=(m_rows // tm,),
        in_specs=[pl.BlockSpec((tm, d), row), pl.BlockSpec((tm, yc.shape[1]), row),
                  pl.BlockSpec((nb, tt, dv), lambda i: (0, i, 0)),
                  _full(wout.shape), _full(g.shape), _full(b.shape)],
        out_specs=pl.BlockSpec((tm, d), row), out_shape=jax.ShapeDtypeStruct((m_rows, d), F32),
        scratch_shapes=[pltpu.VMEM((dv // LANES, tm, LANES), F32)],
        compiler_params=_params(("arbitrary",)), name="odd_post",
    )(x, yc, yd, wout, g, b)


def _token_tile(m_rows):
    tm = min(m_rows, TOKEN_TILE)
    assert m_rows % tm == 0
    return tm


def _router_kernel(x_ref, whi_ref, wlo_ref, br_ref, tri_ref, o_ref, cnt_ref, base_sc):
    tm = x_ref.shape[0]

    @pl.when(pl.program_id(0) == 0)
    def _():
        base_sc[...] = jnp.zeros(base_sc.shape, F32)

    x = x_ref[...]
    x_hi = x.astype(BF16)
    x_lo = (x - x_hi.astype(F32)).astype(BF16)
    logits = _dot(x_hi, whi_ref[...]) + (_dot(x_lo, whi_ref[...]) + _dot(x_hi, wlo_ref[...]))
    scores = _sigmoid(logits.T[0:N_EXPERTS, :])
    sel = scores + br_ref[...]

    def row(a, e):
        return a[e:e + 1, :]

    best_gs = None
    grp = None
    for gidx in range(N_GROUPS):
        v = [row(sel, gidx * EXPERTS_PER_GROUP + i) for i in range(EXPERTS_PER_GROUP)]
        gs = None
        for i in range(EXPERTS_PER_GROUP):
            for j in range(i + 1, EXPERTS_PER_GROUP):
                pair = v[i] + v[j]
                gs = pair if gs is None else jnp.maximum(gs, pair)
        if best_gs is None:
            best_gs, grp = gs, jnp.zeros(gs.shape, jnp.int32)
        else:
            better = gs > best_gs
            grp = jnp.where(better, gidx, grp)
            best_gs = jnp.where(better, gs, best_gs)

    def pick(a, i):
        out = row(a, i)
        for gidx in range(1, N_GROUPS):
            out = jnp.where(grp == gidx, row(a, gidx * EXPERTS_PER_GROUP + i), out)
        return out

    cand = [pick(sel, i) for i in range(EXPERTS_PER_GROUP)]
    cand_score = [pick(scores, i) for i in range(EXPERTS_PER_GROUP)]

    def argmax_first(vals, exclude=None):
        best, idx = None, None
        for i, v in enumerate(vals):
            if exclude is not None:
                v = jnp.where(exclude == i, NEG_INF, v)
            if best is None:
                best, idx = v, jnp.zeros(v.shape, jnp.int32)
            else:
                better = v > best
                idx = jnp.where(better, i, idx)
                best = jnp.where(better, v, best)
        return idx

    loc1 = argmax_first(cand)
    loc2 = argmax_first(cand, exclude=loc1)

    def take(vals, idx):
        out = vals[0]
        for i in range(1, len(vals)):
            out = jnp.where(idx == i, vals[i], out)
        return out

    g1 = take(cand_score, loc1)
    g2 = take(cand_score, loc2)
    gsum = g1 + g2
    first_is_a = loc1 < loc2
    a = jnp.minimum(loc1, loc2)
    b = jnp.maximum(loc1, loc2)
    pair = jnp.where(a == 0, b - 1, jnp.where(a == 1, b + 1, PAIRS_PER_GROUP - 1))
    bucket = grp * PAIRS_PER_GROUP + pair
    gate_a = jnp.where(first_is_a, g1, g2) / gsum
    gate_b = jnp.where(first_is_a, g2, g1) / gsum

    n_rows = base_sc.shape[0]
    hit = lax.broadcasted_iota(jnp.int32, (n_rows, tm), 0) == bucket
    ones = jnp.where(hit, 1.0, 0.0)
    before = _dot(ones.astype(BF16), tri_ref[...]) + base_sc[...]
    rank = jnp.sum(jnp.where(hit, before, 0.0), axis=0, keepdims=True)
    base_sc[...] = base_sc[...] + jnp.sum(ones, axis=1, keepdims=True)
    cnt_ref[...] = jnp.broadcast_to(base_sc[...], cnt_ref.shape)

    zero = jnp.zeros((1, tm), F32)
    o_ref[...] = jnp.concatenate([bucket.astype(F32), rank, gate_a, gate_b, zero, zero, zero, zero], axis=0)


def _router(x, whi, wlo, br):
    m_rows, d = x.shape
    tm = _token_tile(m_rows)
    tri = jnp.triu(jnp.ones((tm, tm), BF16), k=1)
    return pl.pallas_call(
        _router_kernel, grid=(m_rows // tm,),
        in_specs=[pl.BlockSpec((tm, d), lambda i: (i, 0)), _full(whi.shape), _full(wlo.shape), _full(br.shape),
                  _full(tri.shape)],
        out_specs=(pl.BlockSpec((SUBLANES, tm), lambda i: (0, i)), _full((N_BUCKETS, LANES))),
        out_shape=(jax.ShapeDtypeStruct((SUBLANES, m_rows), F32), jax.ShapeDtypeStruct((N_BUCKETS, LANES), F32)),
        scratch_shapes=[pltpu.VMEM((N_BUCKETS, 1), F32)],
        compiler_params=_params(("arbitrary",)), name="router",
    )(x, whi, wlo, br, tri)


def _row_copy(src_ref, src_row, dst_ref, dst_row, sem):
    return pltpu.make_async_copy(src_ref.at[pl.ds(src_row, 1), :], dst_ref.at[pl.ds(dst_row, 1), :], sem)


HI_HALF = 0xFFFF0000
ROW_DMA_UNROLL = 8


def _pack_bf16_pairs(x):
    n = x.shape[1] // 2
    lo = pltpu.bitcast(x[:, :n].astype(BF16).astype(F32), jnp.uint32)
    hi = pltpu.bitcast(x[:, n:].astype(BF16).astype(F32), jnp.uint32)
    return (lo >> 16) | (hi & jnp.uint32(HI_HALF))


def _unpack_bf16_pairs(p):
    lo = pltpu.bitcast(p << 16, F32).astype(BF16)
    hi = pltpu.bitcast(p & jnp.uint32(HI_HALF), F32).astype(BF16)
    return jnp.concatenate([lo, hi], axis=1)


def _dispatch_kernel(dest_ref, x_ref, gate_ref, buf_in_ref, buf_ref, xp_sc, sem):
    del buf_in_ref
    tm, d = x_ref.shape
    step = pl.program_id(0)
    slot = step % 2

    def wait_slot(sl):
        pltpu.make_async_copy(xp_sc.at[sl], buf_ref.at[pl.ds(0, tm), :], sem.at[sl]).wait()

    @pl.when(step >= 2)
    def _():
        wait_slot(slot)

    stage = xp_sc.at[slot]
    stage[:, 0:d // 2] = _pack_bf16_pairs(x_ref[...])
    gate = gate_ref[...]
    lane = lax.broadcasted_iota(jnp.int32, (tm, LANES), 1)
    stage[:, d // 2:] = pltpu.bitcast(jnp.where(lane == 0, gate[:, 0:1], jnp.where(lane == 1, gate[:, 1:2], 0.0)),
                                      jnp.uint32)

    def issue(i, carry):
        for u in range(ROW_DMA_UNROLL):
            r = i * ROW_DMA_UNROLL + u
            _row_copy(stage, r, buf_ref, dest_ref[0, 0, r], sem.at[slot]).start(priority=u % 2)
        return carry

    lax.fori_loop(0, tm // ROW_DMA_UNROLL, issue, 0)

    @pl.when(step == pl.num_programs(0) - 1)
    def _():
        wait_slot(slot)

        @pl.when(step >= 1)
        def _():
            wait_slot(1 - slot)


def _dispatch(x, gates, dest_tiles, n_slots):
    m_rows, d = x.shape
    tm = dest_tiles.shape[2]
    assert tm % ROW_DMA_UNROLL == 0
    width = d // 2 + LANES
    buf0 = jnp.zeros((n_slots, width), jnp.uint32)
    return pl.pallas_call(
        _dispatch_kernel, grid=(m_rows // tm,),
        in_specs=[pl.BlockSpec((1, 1, tm), lambda i: (i, 0, 0), memory_space=pltpu.SMEM),
                  pl.BlockSpec((tm, d), lambda i: (i, 0)),
                  pl.BlockSpec((tm, 2), lambda i: (i, 0)),
                  pl.BlockSpec(memory_space=pl.ANY)],
        out_specs=pl.BlockSpec(memory_space=pl.ANY),
        out_shape=jax.ShapeDtypeStruct((n_slots, width), jnp.uint32),
        scratch_shapes=[pltpu.VMEM((2, tm, width), jnp.uint32), pltpu.SemaphoreType.DMA((2,))],
        input_output_aliases={3: 0},
        compiler_params=_params(("arbitrary",)), name="moe_dispatch",
    )(dest_tiles, x, gates, buf0)


def _expert_kernel(tea_ref, teb_ref, tv_ref, x_ref, win_a_ref, wout_a_ref, win_b_ref, wout_b_ref, o_ref):
    del tea_ref, teb_ref
    i = pl.program_id(0)
    de = wout_a_ref.shape[0]
    half = win_a_ref.shape[0] // 2

    @pl.when(tv_ref[i] != 0)
    def _():
        xb = _unpack_bf16_pairs(x_ref[:, 0:half])
        gate = pltpu.bitcast(x_ref[:, half:], F32)

        def expert(win_ref, wout_ref):
            h = _dot(xb, win_ref[...])
            hg = h[:, 0:de]
            act = hg * _sigmoid(hg) * h[:, de:2 * de]
            return _dot(act.astype(BF16), wout_ref[...])

        o_ref[...] = gate[:, 0:1] * expert(win_a_ref, wout_a_ref) + gate[:, 1:2] * expert(win_b_ref, wout_b_ref)

    @pl.when(tv_ref[i] == 0)
    def _():
        o_ref[...] = jnp.zeros(o_ref.shape, F32)


def _experts(buf, tile_ea, tile_eb, tile_valid, w_in, w_out, layer, tile_rows):
    n_slots, width = buf.shape
    d, de2 = w_in.shape[2], w_in.shape[3]
    grid_spec = pltpu.PrefetchScalarGridSpec(
        num_scalar_prefetch=3, grid=(n_slots // tile_rows,),
        in_specs=[pl.BlockSpec((tile_rows, width), lambda i, ea, eb, tv: (i, 0)),
                  pl.BlockSpec((None, None, d, de2), lambda i, ea, eb, tv: (layer, ea[i], 0, 0)),
                  pl.BlockSpec((None, None, de2 // 2, d), lambda i, ea, eb, tv: (layer, ea[i], 0, 0)),
                  pl.BlockSpec((None, None, d, de2), lambda i, ea, eb, tv: (layer, eb[i], 0, 0)),
                  pl.BlockSpec((None, None, de2 // 2, d), lambda i, ea, eb, tv: (layer, eb[i], 0, 0))],
        out_specs=pl.BlockSpec((tile_rows, d), lambda i, ea, eb, tv: (i, 0)))
    return pl.pallas_call(
        _expert_kernel, grid_spec=grid_spec, out_shape=jax.ShapeDtypeStruct((n_slots, d), F32),
        compiler_params=_params(("arbitrary",)), name="moe_experts",
    )(tile_ea, tile_eb, tile_valid, buf, w_in, w_out, w_in, w_out)


def _combine_kernel(dest_ref, dest_next_ref, x_ref, g_ref, b_ref, y_ref, o_ref, y_sc, sem):
    tm = x_ref.shape[0]
    i = pl.program_id(0)
    slot = i % 2

    def gather(idx_ref, sl):
        def issue(j, carry):
            for u in range(ROW_DMA_UNROLL):
                r = j * ROW_DMA_UNROLL + u
                _row_copy(y_ref, idx_ref[0, 0, r], y_sc.at[sl], r, sem.at[sl]).start(priority=u % 2)
            return carry

        lax.fori_loop(0, tm // ROW_DMA_UNROLL, issue, 0)

    @pl.when(i == 0)
    def _():
        gather(dest_ref, 0)

    @pl.when(i + 1 < pl.num_programs(0))
    def _():
        gather(dest_next_ref, 1 - slot)

    pltpu.make_async_copy(y_ref.at[pl.ds(0, tm), :], y_sc.at[slot], sem.at[slot]).wait()
    o_ref[...] = _layer_norm(ALPHA * x_ref[...] + y_sc[slot], g_ref[...], b_ref[...])


def _combine(x, dest_tiles, y, g, b):
    m_rows, d = x.shape
    tm = dest_tiles.shape[2]
    assert tm % ROW_DMA_UNROLL == 0
    last = m_rows // tm - 1
    return pl.pallas_call(
        _combine_kernel, grid=(m_rows // tm,),
        in_specs=[pl.BlockSpec((1, 1, tm), lambda i: (i, 0, 0), memory_space=pltpu.SMEM),
                  pl.BlockSpec((1, 1, tm), lambda i: (jnp.minimum(i + 1, last), 0, 0), memory_space=pltpu.SMEM),
                  pl.BlockSpec((tm, d), lambda i: (i, 0)),
                  _full(g.shape), _full(b.shape),
                  pl.BlockSpec(memory_space=pl.ANY)],
        out_specs=pl.BlockSpec((tm, d), lambda i: (i, 0)),
        out_shape=jax.ShapeDtypeStruct((m_rows, d), F32),
        scratch_shapes=[pltpu.VMEM((2, tm, d), F32), pltpu.SemaphoreType.DMA((2,))],
        compiler_params=_params(("arbitrary",)), name="moe_combine",
    )(dest_tiles, dest_tiles, x, g, b, y)


def _moe_layer(x, rw, w_in, w_out, layer, g, b, expert_tile):
    m_rows, _ = x.shape
    tm = min(m_rows, MOE_ROW_TILE)
    assert m_rows % tm == 0
    route, counts = _router(x, rw["whi"], rw["wlo"], rw["br"])
    bucket = route[0].astype(jnp.int32)
    rank = route[1].astype(jnp.int32)
    gates = route[2:4].T
    counts = counts[:, 0].astype(jnp.int32)
    padded = (counts + expert_tile - 1) // expert_tile * expert_tile
    pad_ends = jnp.cumsum(padded)
    pad_starts = pad_ends - padded
    b_range = jnp.arange(N_BUCKETS, dtype=jnp.int32)
    dest = jnp.sum(jnp.where(bucket[:, None] == b_range, pad_starts, 0), axis=-1) + rank
    n_tiles = -(-(m_rows + N_BUCKETS * (expert_tile - 1)) // expert_tile)
    tile_start = jnp.arange(n_tiles, dtype=jnp.int32) * expert_tile
    tile_bucket = jnp.minimum(jnp.sum((tile_start[:, None] >= pad_ends[None, :]).astype(jnp.int32), axis=1),
                              N_BUCKETS - 1)
    tile_valid = (tile_start < pad_ends[-1]).astype(jnp.int32)
    pair = tile_bucket % PAIRS_PER_GROUP
    group_base = tile_bucket // PAIRS_PER_GROUP * EXPERTS_PER_GROUP
    pair_range = jnp.arange(PAIRS_PER_GROUP, dtype=jnp.int32)
    tile_ea = group_base + jnp.sum(jnp.where(pair[:, None] == pair_range, jnp.asarray(PAIR_FIRST, jnp.int32), 0), axis=1)
    tile_eb = group_base + jnp.sum(jnp.where(pair[:, None] == pair_range, jnp.asarray(PAIR_SECOND, jnp.int32), 0), axis=1)
    dest_tiles = dest.reshape(m_rows // tm, 1, tm)
    buf = _dispatch(x, gates, dest_tiles, n_tiles * expert_tile)
    y = _experts(buf, tile_ea, tile_eb, tile_valid, w_in, w_out, layer, expert_tile)
    return _combine(x, dest_tiles, y, g, b)


def _block_diag(w):
    h, n, _ = w.shape
    eye = jnp.eye(h, dtype=w.dtype)
    return (eye[:, None, :, None] * w[:, :, None, :]).reshape(h * n, h * n)


def _pair_blocks(w):
    h, a, b = w.shape
    w = w.reshape(h // 2, 2, a, b)
    eye = jnp.eye(2, dtype=w.dtype)
    return (eye[None, :, None, :, None] * w[:, :, :, None, :]).reshape(h // 2, 2 * a, 2 * b)


def _time_major(a):
    b, t, c = a.shape
    return a.transpose(1, 0, 2).reshape(t * b, c)


def _batch_major(a, nb):
    return a.reshape(-1, nb, a.shape[-1]).transpose(1, 0, 2)


def _rope_tables(pos):
    half = QK_ROPE // 2
    inv = ROPE_THETA ** (-jnp.arange(half, dtype=F32) / half)
    ang = pos.astype(F32)[:, None] * inv
    cos, sin = jnp.cos(ang), jnp.sin(ang)
    cos_t = jnp.tile(jnp.concatenate([cos, cos], axis=1), (1, N_HEADS))
    sin_t = jnp.tile(jnp.concatenate([-sin, sin], axis=1), (1, N_HEADS))
    return cos_t, sin_t


def kernel(x_prompt, x_sample, state_conv_a, state_conv_b, state_rglru_h, state_conv_c, cache_ckv, cache_kpe, page_table, w_in_even, conv_a_w, conv_b_w, conv_b_b, lru_wa, lru_ba, lru_wx, lru_bx, lru_lambda, w_out_even, w_in_odd, conv_c_w, conv_c_b, ln_c_g, ln_c_b, q_norm_g, w_q_b, kv_norm_g, w_uk, w_uv, w_out_odd, ln_mix_g, ln_mix_b, ln_ffn_g, ln_ffn_b, w_router, b_router, w_exp_in, w_exp_out):
    bp, n_tp, d = x_prompt.shape
    bs, n_ts, _ = x_sample.shape
    past_len = page_table.shape[1] * PAGE_SIZE
    dc = conv_c_w.shape[2]
    q_rank = q_norm_g.shape[1]
    kv_rank = kv_norm_g.shape[1]

    xp = _time_major(x_prompt)
    xs = _time_major(x_sample)
    row2 = lambda v: v.reshape(1, -1)

    wr = jnp.pad(w_router, ((0, 0), (0, LANES - N_EXPERTS)))
    wr_hi = wr.astype(BF16)
    router_w = {"whi": wr_hi, "wlo": (wr - wr_hi.astype(F32)).astype(BF16), "br": b_router.reshape(N_EXPERTS, 1)}
    rope_p = _rope_tables(jnp.arange(n_tp))
    rope_s = _rope_tables(past_len + jnp.arange(n_ts))
    we_in, we_out = w_exp_in.astype(BF16), w_exp_out.astype(BF16)
    cache_kpe_t = jnp.swapaxes(cache_kpe, 2, 3)

    outs = {k: [] for k in ("ca_p", "ca_s", "cb_p", "cb_s", "h_p", "h_s", "cc_p", "cc_s", "ckv_p", "ckv_s", "kpe_p", "kpe_s")}
    for l in range(DEPTH):
        j = l // 2
        lg, lb = row2(ln_mix_g[l]), row2(ln_mix_b[l])
        if l % 2 == 0:
            wa_bd, wx_bd = _block_diag(lru_wa[j]), _block_diag(lru_wx[j])
            hc = wa_bd.shape[0] // 2
            wg = jnp.stack([jnp.concatenate([wa_bd[s * hc:(s + 1) * hc, s * hc:(s + 1) * hc],
                                             wx_bd[s * hc:(s + 1) * hc, s * hc:(s + 1) * hc]], axis=1)
                            for s in range(2)]).astype(BF16)
            w = {"win": w_in_even[j].astype(BF16), "caw": conv_a_w[j], "cbw": conv_b_w[j], "cbb": row2(conv_b_b[j]),
                 "wg": wg, "ba": row2(lru_ba[j]), "bx": row2(lru_bx[j]), "lam": row2(lru_lambda[j]),
                 "wout": w_out_even[j].astype(BF16), "g": lg, "b": lb}
            da = conv_a_w.shape[2]
            db = conv_b_w.shape[2]
            xp, a1, b1, h1 = _even_layer(xp, w, jnp.zeros((2 * bp, da), F32), jnp.zeros((3 * bp, db), F32),
                                         jnp.zeros((bp, db), F32), nb=bp, n_t=n_tp)
            xs, a2, b2, h2 = _even_layer(xs, w, _time_major(state_conv_a[j]), _time_major(state_conv_b[j]),
                                         state_rglru_h[j], nb=bs, n_t=n_ts)
            outs["ca_p"].append(_batch_major(a1, bp)); outs["ca_s"].append(_batch_major(a2, bs))
            outs["cb_p"].append(_batch_major(b1, bp)); outs["cb_s"].append(_batch_major(b2, bs))
            outs["h_p"].append(h1); outs["h_s"].append(h2)
        else:
            nope_all = N_HEADS * QK_NOPE
            wq = w_q_b[j].reshape(q_rank, N_HEADS, QK_NOPE + QK_ROPE)
            wqb = jnp.concatenate([wq[:, :, :QK_NOPE].reshape(q_rank, nope_all),
                                   wq[:, :, QK_NOPE:].reshape(q_rank, N_HEADS * QK_ROPE)], axis=1).astype(BF16)
            o_pe = 2 * dc + q_rank + kv_rank
            win = jnp.concatenate([w_in_odd[j][:, :o_pe], jnp.tile(w_in_odd[j][:, o_pe:], (1, N_HEADS))], axis=1).astype(BF16)
            wuk = _pair_blocks(w_uk[j].transpose(1, 2, 0)).astype(BF16)
            wuv = _pair_blocks(w_uv[j].transpose(1, 0, 2)).astype(BF16)
            w = {"win": win, "ccw": conv_c_w[j], "ccb": row2(conv_c_b[j]), "lcg": row2(ln_c_g[j]), "lcb": row2(ln_c_b[j]),
                 "qg": row2(q_norm_g[j]), "wqb": wqb, "kvg": row2(kv_norm_g[j]), "wuk": wuk}
            width = conv_c_w.shape[1]
            wout = w_out_odd[j].astype(BF16)
            ycp, qp, kp, ckv1, kpe1, c1 = _odd_pre(xp, w, jnp.zeros(((width - 1) * bp, dc), F32), *rope_p, nb=bp, n_t=n_tp)
            ycs, qs, ks, ckv2, kpe2, c2 = _odd_pre(xs, w, _time_major(state_conv_c[j]), *rope_s, nb=bs, n_t=n_ts)
            ydp = _prompt_attention(qp, kp, wuv)
            yds = _sample_attention(qs, ks, wuv, cache_ckv, cache_kpe_t, page_table, j)
            xp = _odd_post(xp, ycp, ydp, wout, lg, lb, nb=bp, n_t=n_tp)
            xs = _odd_post(xs, ycs, yds, wout, lg, lb, nb=bs, n_t=n_ts)
            outs["cc_p"].append(_batch_major(c1, bp)); outs["cc_s"].append(_batch_major(c2, bs))
            outs["ckv_p"].append(ckv1); outs["ckv_s"].append(ckv2)
            outs["kpe_p"].append(kpe1); outs["kpe_s"].append(kpe2)
        fg, fb = row2(ln_ffn_g[l]), row2(ln_ffn_b[l])
        xp = _moe_layer(xp, router_w, we_in, we_out, l, fg, fb, expert_tile=512)
        xs = _moe_layer(xs, router_w, we_in, we_out, l, fg, fb, expert_tile=64)

    y_prompt = _batch_major(xp, bp)
    y_sample = _batch_major(xs, bs)
    st = lambda k: jnp.stack(outs[k])
    st1 = lambda k: jnp.stack(outs[k], axis=1)
    return (y_prompt, y_sample, st("ca_p"), st("ca_s"), st("cb_p"), st("cb_s"), st("h_p"), st("h_s"),
            st("cc_p"), st("cc_s"), st1("ckv_p"), st1("ckv_s"), st1("kpe_p"), st1("kpe_s"))
```

```python
import functools

import jax
import jax.numpy as jnp
from jax import lax
from jax.experimental import pallas as pl
from jax.experimental.pallas import tpu as pltpu

F32 = jnp.float32
BF16 = jnp.bfloat16

DEPTH = 4
N_HEADS = 8
QK_NOPE = 64
QK_ROPE = 32
V_DIM = 64
N_EXPERTS = 16
N_GROUPS = 4
EXPERTS_PER_GROUP = 4
PAIRS_PER_GROUP = 6
N_BUCKETS = N_GROUPS * PAIRS_PER_GROUP
PAIR_FIRST = (0, 0, 0, 1, 1, 2)
PAIR_SECOND = (1, 2, 3, 2, 3, 3)
LRU_C = 8.0
ROPE_THETA = 10000.0
ATTN_SCALE = (QK_NOPE + QK_ROPE) ** -0.5
ALPHA = (2 * DEPTH) ** 0.25
PAGE_SIZE = 128

SUBLANES = 8
BF16_SUBLANES = 16
LANES = 128
VMEM_LIMIT_BYTES = 56 * 1024 * 1024

TIME_STEPS_PER_TILE = 32
TOKEN_TILE = 512
MOE_ROW_TILE = 1024
ATTN_TILE = 512
PAGES_PER_STEP = 64
NEG_INF = float("-inf")


def _params(semantics):
    return pltpu.CompilerParams(dimension_semantics=semantics, vmem_limit_bytes=VMEM_LIMIT_BYTES)


def _full(shape):
    nd = len(shape)
    return pl.BlockSpec(shape, lambda *_: (0,) * nd)


def _layer_norm(x, g, b, eps=1e-5):
    mu = jnp.mean(x, axis=-1, keepdims=True)
    xc = x - mu
    var = jnp.mean(xc * xc, axis=-1, keepdims=True)
    return xc * lax.rsqrt(var + eps) * g + b


def _rms_norm(x, g, eps=1e-6):
    return x * lax.rsqrt(jnp.mean(x * x, axis=-1, keepdims=True) + eps) * g


def _sigmoid(x):
    return 1.0 / (1.0 + jnp.exp(-x))


def _dot(a, b):
    return jnp.dot(a, b, preferred_element_type=F32)


def _dot_nt(a, b):
    return lax.dot_general(a, b, (((1,), (1,)), ((), ())), preferred_element_type=F32)


def _seq_tile(nb, n_t):
    tt = min(n_t, TIME_STEPS_PER_TILE)
    assert n_t % tt == 0 and nb % SUBLANES == 0
    return tt


def _even_kernel(x_ref, win_ref, caw_ref, cbw_ref, cbb_ref, wg_ref, ba_ref, bx_ref, lam_ref, wout_ref,
                 g_ref, b_ref, sa_ref, sb_ref, h0_ref,
                 o_ref, na_ref, nb_ref, hl_ref,
                 ua_ext, vb_ext, h_sc, hs_sc, *, nb, tt):
    tm = tt * nb
    dh = caw_ref.shape[1]
    half = dh // 2

    @pl.when(pl.program_id(0) == 0)
    def _():
        ua_ext[0:2 * nb, :] = sa_ref[...]
        vb_ext[0:3 * nb, :] = sb_ref[...]
        h_sc[...] = h0_ref[...]

    x = x_ref[...]
    xb = x.astype(BF16)

    def proj(j):
        return _dot(xb, win_ref[:, j * dh:(j + 1) * dh])

    ua_ext[2 * nb:2 * nb + tm, :] = proj(1) * proj(2)
    caw = caw_ref[...]
    conv_a = (caw[0:1] * ua_ext[0:tm, :] + caw[1:2] * ua_ext[nb:nb + tm, :]
              + caw[2:3] * ua_ext[2 * nb:2 * nb + tm, :])
    y_a = proj(0) * conv_a
    tail_a = ua_ext[tm:tm + 2 * nb, :]
    na_ref[...] = tail_a
    ua_ext[0:2 * nb, :] = tail_a

    vb_ext[3 * nb:3 * nb + tm, :] = proj(4)
    cbw = cbw_ref[...]
    u_b = (cbb_ref[...] + cbw[0:1] * vb_ext[0:tm, :] + cbw[1:2] * vb_ext[nb:nb + tm, :]
           + cbw[2:3] * vb_ext[2 * nb:2 * nb + tm, :] + cbw[3:4] * vb_ext[3 * nb:3 * nb + tm, :])
    tail_b = vb_ext[tm:tm + 3 * nb, :]
    nb_ref[...] = tail_b
    vb_ext[0:3 * nb, :] = tail_b

    ub16 = u_b.astype(BF16)
    gk0 = _dot(ub16[:, :half], wg_ref[0])
    gk1 = _dot(ub16[:, half:], wg_ref[1])
    r = _sigmoid(jnp.concatenate([gk0[:, :half], gk1[:, :half]], axis=1) + ba_ref[...])
    ig = _sigmoid(jnp.concatenate([gk0[:, half:], gk1[:, half:]], axis=1) + bx_ref[...])
    nlam = -lam_ref[...]
    softplus = jnp.maximum(nlam, 0.0) + jnp.log(1.0 + jnp.exp(-jnp.abs(nlam)))
    log_a = (-LRU_C * r) * softplus
    a = jnp.exp(log_a)
    bterm = jnp.sqrt(1.0 - a * a) * (ig * u_b)

    h = h_sc[...]
    for t in range(tt):
        h = a[t * nb:(t + 1) * nb, :] * h + bterm[t * nb:(t + 1) * nb, :]
        hs_sc[t * nb:(t + 1) * nb, :] = h
    h_sc[...] = h
    hl_ref[...] = h
    y_b = jax.nn.gelu(proj(3), approximate=True) * hs_sc[...]

    m = _dot(y_a.astype(BF16), wout_ref[0:dh, :]) + _dot(y_b.astype(BF16), wout_ref[dh:2 * dh, :])
    o_ref[...] = _layer_norm(ALPHA * x + m, g_ref[...], b_ref[...])


def _even_layer(x, w, sa, sb, h0, *, nb, n_t):
    m_rows, d = x.shape
    dh = w["caw"].shape[1]
    tt = _seq_tile(nb, n_t)
    tm = tt * nb
    kern = functools.partial(_even_kernel, nb=nb, tt=tt)
    row = lambda i: (i, 0)
    in_specs = [pl.BlockSpec((tm, d), row), _full(w["win"].shape), _full(w["caw"].shape), _full(w["cbw"].shape),
                _full(w["cbb"].shape), _full(w["wg"].shape), _full(w["ba"].shape), _full(w["bx"].shape),
                _full(w["lam"].shape), _full(w["wout"].shape), _full(w["g"].shape), _full(w["b"].shape),
                _full(sa.shape), _full(sb.shape), _full(h0.shape)]
    out_shape = (jax.ShapeDtypeStruct((m_rows, d), F32), jax.ShapeDtypeStruct(sa.shape, F32),
                 jax.ShapeDtypeStruct(sb.shape, F32), jax.ShapeDtypeStruct(h0.shape, F32))
    out_specs = (pl.BlockSpec((tm, d), row), _full(sa.shape), _full(sb.shape), _full(h0.shape))
    scratch = [pltpu.VMEM((tm + 2 * nb, dh), F32), pltpu.VMEM((tm + 3 * nb, dh), F32),
               pltpu.VMEM((nb, dh), F32), pltpu.VMEM((tm, dh), F32)]
    return pl.pallas_call(
        kern, grid=(m_rows // tm,), in_specs=in_specs, out_specs=out_specs, out_shape=out_shape,
        scratch_shapes=scratch, compiler_params=_params(("arbitrary",)), name="even_mixer",
    )(x, w["win"], w["caw"], w["cbw"], w["cbb"], w["wg"], w["ba"], w["bx"], w["lam"], w["wout"], w["g"], w["b"],
      sa, sb, h0)


def _odd_pre_kernel(x_ref, win_ref, ccw_ref, ccb_ref, lcg_ref, lcb_ref, qg_ref, wqb_ref, kvg_ref, wuk_ref,
                    cos_ref, sin_ref, sc_ref, perm_ref,
                    yc_ref, q_ref, k_ref, ckv_ref, kpe_ref, nc_ref,
                    c_ext, rl_sc, *, nb, tt, width):
    tm = tt * nb
    dc = ccw_ref.shape[1]
    q_rank = qg_ref.shape[1]
    kv_rank = kvg_ref.shape[1]
    pe_all = N_HEADS * QK_ROPE
    hist = (width - 1) * nb

    @pl.when(pl.program_id(0) == 0)
    def _():
        c_ext[0:hist, :] = sc_ref[...]

    xb = x_ref[...].astype(BF16)
    o_q = 2 * dc
    o_kv = o_q + q_rank
    o_pe = o_kv + kv_rank

    glu = _dot(xb, win_ref[:, 0:dc]) * _sigmoid(_dot(xb, win_ref[:, dc:2 * dc]))
    c_ext[hist:hist + tm, :] = glu
    ccw = ccw_ref[...]
    u_c = ccb_ref[...] + ccw[0:1] * c_ext[0:tm, :]
    for k in range(1, width):
        u_c = u_c + ccw[k:k + 1] * c_ext[k * nb:k * nb + tm, :]
    tail = c_ext[tm:tm + hist, :]
    nc_ref[...] = tail
    c_ext[0:hist, :] = tail
    ln = _layer_norm(u_c, lcg_ref[...], lcb_ref[...])
    yc_ref[...] = ln * _sigmoid(ln)

    def to_sequences(val, store):
        n_lane_tiles = val.shape[1] // LANES
        for c in range(n_lane_tiles):
            rl_sc[c] = val[:, c * LANES:(c + 1) * LANES]

        def body(b, carry):
            store(b, jnp.concatenate([rl_sc[c, pl.ds(b, tt, stride=nb), :] for c in range(n_lane_tiles)], axis=1))
            return carry

        lax.fori_loop(0, nb, body, 0)

    cos = jnp.broadcast_to(cos_ref[...][:, None, :], (tt, nb, pe_all)).reshape(tm, pe_all)
    sin = jnp.broadcast_to(sin_ref[...][:, None, :], (tt, nb, pe_all)).reshape(tm, pe_all)
    lane = lax.broadcasted_iota(jnp.int32, (1, pe_all), 1)
    first_half = (lane % QK_ROPE) < (QK_ROPE // 2)
    head_of_lane = lane // QK_ROPE

    def rope(v):
        swapped = jnp.where(first_half, pltpu.roll(v, pe_all - QK_ROPE // 2, 1), pltpu.roll(v, QK_ROPE // 2, 1))
        return v * cos + swapped * sin

    permute_on_mxu = tt % BF16_SUBLANES == 0

    def to_sequences_bf16(val, store):
        if not permute_on_mxu:
            to_sequences(val, lambda b, v: store(b, v.astype(BF16)))
            return
        pv = _dot(perm_ref[...], val.astype(BF16)).astype(BF16)
        for b in range(nb):
            store(b, pv[b * tt:(b + 1) * tt])

    ckv = _rms_norm(_dot(xb, win_ref[:, o_kv:o_kv + kv_rank]), kvg_ref[...])
    kpe = rope(_dot(xb, win_ref[:, o_pe:o_pe + pe_all]))
    k_all = jnp.concatenate([ckv, kpe], axis=1)

    def store_k32(b, v):
        ckv_ref[b] = v[:, 0:kv_rank]
        kpe_ref[b] = v[:, kv_rank:kv_rank + QK_ROPE]

    def store_k16(b, v):
        k_ref[b] = v

    def store_k(b, v):
        store_k32(b, v)
        store_k16(b, v.astype(BF16))

    if permute_on_mxu:
        to_sequences(k_all, store_k32)
        to_sequences_bf16(k_all, store_k16)
    else:
        to_sequences(k_all, store_k)

    qn = _rms_norm(_dot(xb, win_ref[:, o_q:o_q + q_rank]), qg_ref[...]).astype(BF16)
    nope_all = N_HEADS * QK_NOPE
    q_nope = (_dot(qn, wqb_ref[:, 0:nope_all]) * ATTN_SCALE).astype(BF16)
    q_pe = rope(_dot(qn, wqb_ref[:, nope_all:nope_all + pe_all]) * ATTN_SCALE)
    for p in range(N_HEADS // 2):
        lat2 = _dot(q_nope[:, p * 2 * QK_NOPE:(p + 1) * 2 * QK_NOPE], wuk_ref[p])
        for s in range(2):
            h = 2 * p + s

            def store_q(b, v, h=h):
                q_ref[b, h] = v

            to_sequences_bf16(jnp.concatenate([lat2[:, s * kv_rank:(s + 1) * kv_rank],
                                               jnp.where(head_of_lane == h, q_pe, 0.0)], axis=1), store_q)


def _odd_pre(x, w, sc, cos, sin, *, nb, n_t):
    m_rows, d = x.shape
    width, dc = w["ccw"].shape
    kv_rank = w["kvg"].shape[1]
    pe_all = N_HEADS * QK_ROPE
    kq = kv_rank + pe_all
    tt = _seq_tile(nb, n_t)
    tm = tt * nb
    kern = functools.partial(_odd_pre_kernel, nb=nb, tt=tt, width=width)
    row = lambda i: (i, 0)
    seq = lambda i: (0, i, 0)
    in_specs = [pl.BlockSpec((tm, d), row)] + [_full(w[k].shape) for k in
                                                ("win", "ccw", "ccb", "lcg", "lcb", "qg", "wqb", "kvg", "wuk")]
    out_row = jnp.arange(tm, dtype=jnp.int32)
    src_row = (out_row % tt) * nb + out_row // tt
    perm = (src_row[:, None] == out_row[None, :]).astype(BF16)
    in_specs += [pl.BlockSpec((tt, pe_all), row), pl.BlockSpec((tt, pe_all), row), _full(sc.shape), _full(perm.shape)]
    out_shape = (jax.ShapeDtypeStruct((m_rows, dc), F32),
                 jax.ShapeDtypeStruct((nb, N_HEADS, n_t, kq), BF16),
                 jax.ShapeDtypeStruct((nb, n_t, kq), BF16),
                 jax.ShapeDtypeStruct((nb, n_t, kv_rank), F32),
                 jax.ShapeDtypeStruct((nb, n_t, QK_ROPE), F32),
                 jax.ShapeDtypeStruct(sc.shape, F32))
    out_specs = (pl.BlockSpec((tm, dc), row), pl.BlockSpec((nb, N_HEADS, tt, kq), lambda i: (0, 0, i, 0)),
                 pl.BlockSpec((nb, tt, kq), seq), pl.BlockSpec((nb, tt, kv_rank), seq),
                 pl.BlockSpec((nb, tt, QK_ROPE), seq), _full(sc.shape))
    scratch = [pltpu.VMEM((tm + (width - 1) * nb, dc), F32), pltpu.VMEM((kq // LANES, tm, LANES), F32)]
    return pl.pallas_call(
        kern, grid=(m_rows // tm,), in_specs=in_specs, out_specs=out_specs, out_shape=out_shape,
        scratch_shapes=scratch, compiler_params=_params(("arbitrary",)), name="odd_pre",
    )(x, w["win"], w["ccw"], w["ccb"], w["lcg"], w["lcb"], w["qg"], w["wqb"], w["kvg"], w["wuk"], cos, sin, sc, perm)


def _softmax_step(s, v16, m_sc, l_sc, acc_sc):
    m_prev = m_sc[...]
    m_new = jnp.maximum(m_prev, jnp.max(s, axis=-1, keepdims=True))
    alpha = jnp.exp(m_prev - m_new)
    p = jnp.exp(s - m_new)
    l_sc[...] = alpha * l_sc[...] + jnp.sum(p, axis=-1, keepdims=True)
    acc_sc[...] = alpha * acc_sc[...] + _dot(p.astype(BF16), v16)
    m_sc[...] = m_new


def _softmax_init(m_sc, l_sc, acc_sc):
    m_sc[...] = jnp.full(m_sc.shape, NEG_INF, F32)
    l_sc[...] = jnp.zeros(l_sc.shape, F32)
    acc_sc[...] = jnp.zeros(acc_sc.shape, F32)


def _prompt_attn_kernel(qi_ref, ki_ref, q_ref, k_ref, bias_ref, wuv_ref, o_ref, *state, kv_rank):
    m_sc, l_sc, acc_sc = state[0:N_HEADS], state[N_HEADS:2 * N_HEADS], state[2 * N_HEADS:3 * N_HEADS]
    j = pl.program_id(1)
    qi = qi_ref[j]
    ki = ki_ref[j]

    @pl.when(ki == 0)
    def _():
        for h in range(N_HEADS):
            _softmax_init(m_sc[h], l_sc[h], acc_sc[h])

    k = k_ref[...]
    v_t = k[:, 0:kv_rank].T
    diag = (ki == qi).astype(jnp.int32)

    def scores_t(h):
        return _dot_nt(k, q_ref[h]) + bias_ref[diag]

    s_next = scores_t(0)
    for h in range(N_HEADS):
        s = s_next
        if h + 1 < N_HEADS:
            s_next = scores_t(h + 1)
        m_prev = m_sc[h][...]
        m_new = jnp.maximum(m_prev, jnp.max(s, axis=0, keepdims=True))
        alpha = jnp.exp(m_prev - m_new)
        p = jnp.exp(s - m_new)
        l_sc[h][...] = alpha * l_sc[h][...] + jnp.sum(p, axis=0, keepdims=True)
        acc_sc[h][...] = alpha * acc_sc[h][...] + _dot(v_t, p.astype(BF16))
        m_sc[h][...] = m_new

    @pl.when(ki == qi)
    def _():
        for p in range(N_HEADS // 2):
            o2_t = jnp.concatenate([acc_sc[2 * p][...] / l_sc[2 * p][...],
                                    acc_sc[2 * p + 1][...] / l_sc[2 * p + 1][...]], axis=0)
            o_ref[:, p * 2 * V_DIM:(p + 1) * 2 * V_DIM] = _dot(o2_t.T.astype(BF16), wuv_ref[p])


def _prompt_attention(q, k, wuv):
    nb, n_heads, n_t, kq = q.shape
    kv_rank = wuv.shape[1] // 2
    dv = N_HEADS * V_DIM
    tile = min(n_t, ATTN_TILE)
    assert n_t % tile == 0
    nq = n_t // tile
    pairs = [(a, b) for a in range(nq) for b in range(a + 1)]
    qi_tab = jnp.asarray([p[0] for p in pairs], jnp.int32)
    ki_tab = jnp.asarray([p[1] for p in pairs], jnp.int32)
    causal = jnp.where(jnp.arange(tile)[:, None] <= jnp.arange(tile)[None, :], 0.0, NEG_INF).astype(F32)
    bias = jnp.stack([jnp.zeros((tile, tile), F32), causal])
    kern = functools.partial(_prompt_attn_kernel, kv_rank=kv_rank)
    grid_spec = pltpu.PrefetchScalarGridSpec(
        num_scalar_prefetch=2, grid=(nb, len(pairs)),
        in_specs=[pl.BlockSpec((None, n_heads, tile, kq), lambda b, j, qi, ki: (b, 0, qi[j], 0)),
                  pl.BlockSpec((None, tile, kq), lambda b, j, qi, ki: (b, ki[j], 0)),
                  pl.BlockSpec(bias.shape, lambda b, j, qi, ki: (0, 0, 0)),
                  pl.BlockSpec(wuv.shape, lambda b, j, qi, ki: (0, 0, 0))],
        out_specs=pl.BlockSpec((None, tile, dv), lambda b, j, qi, ki: (b, qi[j], 0)),
        scratch_shapes=([pltpu.VMEM((1, tile), F32)] * (2 * n_heads) + [pltpu.VMEM((kv_rank, tile), F32)] * n_heads))
    return pl.pallas_call(
        kern, grid_spec=grid_spec, out_shape=jax.ShapeDtypeStruct((nb, n_t, dv), F32),
        compiler_params=_params(("arbitrary", "arbitrary")), name="prompt_attention",
    )(qi_tab, ki_tab, q, k, bias, wuv)


def _sample_attn_kernel(pt_ref, q_ref, kn_ref, tsel_ref, wuv_ref, ckv_hbm, kpe_hbm, o_ref,
                        ck_buf, kp_buf, sem, qc_sc, m_sc, l_sc, acc_sc, *,
                        n_pages_step, layer, kv_rank, n_new, rows_per_head):
    b = pl.program_id(0)
    g = pl.program_id(1)
    n_g = pl.num_programs(1)
    step = b * n_g + g
    n_steps = pl.num_programs(0) * n_g
    slot = step % 2

    def page_copies(seq, grp, sl):
        copies = []
        for i in range(n_pages_step):
            page = pt_ref[seq, grp * n_pages_step + i]
            copies.append(pltpu.make_async_copy(
                ckv_hbm.at[page, layer], ck_buf.at[sl, pl.ds(i * PAGE_SIZE, PAGE_SIZE), :], sem.at[sl]))
            copies.append(pltpu.make_async_copy(
                kpe_hbm.at[page, layer], kp_buf.at[sl, :, pl.ds(i * PAGE_SIZE, PAGE_SIZE)], sem.at[sl]))
        return copies

    @pl.when(step == 0)
    def _():
        for c in page_copies(0, 0, 0):
            c.start()

    @pl.when(step + 1 < n_steps)
    def _():
        nxt = step + 1
        for c in page_copies(nxt // n_g, nxt % n_g, 1 - slot):
            c.start()

    @pl.when(g == 0)
    def _():
        _softmax_init(m_sc, l_sc, acc_sc)
        qc_sc[...] = _dot(q_ref[:, kv_rank:], tsel_ref[...]).astype(BF16)

    for c in page_copies(b, g, slot):
        c.wait()

    q = q_ref[...]
    ck = ck_buf[slot].astype(BF16)
    kp_t = kp_buf[slot].astype(BF16)
    s = _dot_nt(q[:, 0:kv_rank], ck) + _dot(qc_sc[...], kp_t)
    _softmax_step(s, ck, m_sc, l_sc, acc_sc)

    @pl.when(g == pl.num_programs(1) - 1)
    def _():
        kn = kn_ref[...]
        s_new = _dot_nt(q, kn)
        t_row = lax.broadcasted_iota(jnp.int32, s_new.shape, 0) % rows_per_head
        t_col = lax.broadcasted_iota(jnp.int32, s_new.shape, 1)
        s_new = jnp.where((t_col <= t_row) & (t_col < n_new), s_new, NEG_INF)
        _softmax_step(s_new, kn[:, 0:kv_rank], m_sc, l_sc, acc_sc)
        o = acc_sc[...] / l_sc[...]
        rp = rows_per_head
        for p in range(N_HEADS // 2):
            o2 = jnp.concatenate([o[2 * p * rp:(2 * p + 1) * rp], o[(2 * p + 1) * rp:(2 * p + 2) * rp]], axis=1)
            o_ref[:, p * 2 * V_DIM:(p + 1) * 2 * V_DIM] = _dot(o2.astype(BF16), wuv_ref[p])


def _sample_attention(q, k, wuv, cache_ckv, cache_kpe_t, page_table, layer):
    nb, n_heads, n_t, kq = q.shape
    kv_rank = wuv.shape[1] // 2
    dv = N_HEADS * V_DIM
    n_pages = page_table.shape[1]
    gp = min(PAGES_PER_STEP, n_pages)
    assert n_pages % gp == 0 and n_t <= SUBLANES
    rp = SUBLANES
    qb = jnp.pad(q, ((0, 0), (0, 0), (0, rp - n_t), (0, 0))).reshape(nb, n_heads * rp, kq)
    kb = jnp.pad(k, ((0, 0), (0, rp - n_t), (0, 0)))
    tsel = jnp.tile(jnp.eye(QK_ROPE, dtype=BF16), (N_HEADS, 1))
    kern = functools.partial(_sample_attn_kernel, n_pages_step=gp, layer=layer, kv_rank=kv_rank, n_new=n_t,
                             rows_per_head=rp)
    in_specs = [pl.BlockSpec((None, n_heads * rp, kq), lambda b, g, pt: (b, 0, 0)),
                pl.BlockSpec((None, rp, kq), lambda b, g, pt: (b, 0, 0)),
                pl.BlockSpec(tsel.shape, lambda b, g, pt: (0, 0)),
                pl.BlockSpec(wuv.shape, lambda b, g, pt: (0, 0, 0)),
                pl.BlockSpec(memory_space=pl.ANY), pl.BlockSpec(memory_space=pl.ANY)]
    grid_spec = pltpu.PrefetchScalarGridSpec(
        num_scalar_prefetch=1, grid=(nb, n_pages // gp), in_specs=in_specs,
        out_specs=pl.BlockSpec((None, rp, dv), lambda b, g, pt: (b, 0, 0)),
        scratch_shapes=[pltpu.VMEM((2, gp * PAGE_SIZE, kv_rank), F32), pltpu.VMEM((2, QK_ROPE, gp * PAGE_SIZE), F32),
                        pltpu.SemaphoreType.DMA((2,)),
                        pltpu.VMEM((n_heads * rp, QK_ROPE), BF16), pltpu.VMEM((n_heads * rp, 1), F32),
                        pltpu.VMEM((n_heads * rp, 1), F32), pltpu.VMEM((n_heads * rp, kv_rank), F32)])
    out = pl.pallas_call(
        kern, grid_spec=grid_spec, out_shape=jax.ShapeDtypeStruct((nb, rp, dv), F32),
        compiler_params=_params(("arbitrary", "arbitrary")), name="sample_attention",
    )(page_table, qb, kb, tsel, wuv, cache_ckv, cache_kpe_t)
    return out[:, :n_t]


def _odd_post_kernel(x_ref, yc_ref, yd_ref, wout_ref, g_ref, b_ref, o_ref, rl_sc, *, nb, tt):
    dc = yc_ref.shape[1]

    n_lane_tiles = rl_sc.shape[0]

    def body(b, carry):
        v = yd_ref[b]
        for c in range(n_lane_tiles):
            rl_sc[c, pl.ds(b, tt, stride=nb), :] = v[:, c * LANES:(c + 1) * LANES]
        return carry

    lax.fori_loop(0, nb, body, 0)
    yd = jnp.concatenate([rl_sc[c] for c in range(n_lane_tiles)], axis=1)
    m = _dot(yc_ref[...].astype(BF16), wout_ref[0:dc, :]) + _dot(yd.astype(BF16), wout_ref[dc:, :])
    o_ref[...] = _layer_norm(ALPHA * x_ref[...] + m, g_ref[...], b_ref[...])


def _odd_post(x, yc, yd, wout, g, b, *, nb, n_t):
    m_rows, d = x.shape
    dv = yd.shape[2]
    tt = _seq_tile(nb, n_t)
    tm = tt * nb
    row = lambda i: (i, 0)
    kern = functools.partial(_odd_post_kernel, nb=nb, tt=tt)
    return pl.pallas_call(
        kern, grid=(m_rows // tm,),
        in_specs=[pl.BlockSpec((tm, d), row), pl.BlockSpec((tm, yc.shape[1]), row),
                  pl.BlockSpec((nb, tt, dv), lambda i: (0, i, 0)),
                  _full(wout.shape), _full(g.shape), _full(b.shape)],
        out_specs=pl.BlockSpec((tm, d), row), out_shape=jax.ShapeDtypeStruct((m_rows, d), F32),
        scratch_shapes=[pltpu.VMEM((dv // LANES, tm, LANES), F32)],
        compiler_params=_params(("arbitrary",)), name="odd_post",
    )(x, yc, yd, wout, g, b)


def _token_tile(m_rows):
    tm = min(m_rows, TOKEN_TILE)
    assert m_rows % tm == 0
    return tm


def _router_kernel(x_ref, whi_ref, wlo_ref, br_ref, tri_ref, o_ref, cnt_ref, base_sc):
    tm = x_ref.shape[0]

    @pl.when(pl.program_id(0) == 0)
    def _():
        base_sc[...] = jnp.zeros(base_sc.shape, F32)

    x = x_ref[...]
    x_hi = x.astype(BF16)
    x_lo = (x - x_hi.astype(F32)).astype(BF16)
    logits = _dot(x_hi, whi_ref[...]) + (_dot(x_lo, whi_ref[...]) + _dot(x_hi, wlo_ref[...]))
    scores = _sigmoid(logits.T[0:N_EXPERTS, :])
    sel = scores + br_ref[...]

    def row(a, e):
        return a[e:e + 1, :]

    best_gs = None
    grp = None
    for gidx in range(N_GROUPS):
        v = [row(sel, gidx * EXPERTS_PER_GROUP + i) for i in range(EXPERTS_PER_GROUP)]
        gs = None
        for i in range(EXPERTS_PER_GROUP):
            for j in range(i + 1, EXPERTS_PER_GROUP):
                pair = v[i] + v[j]
                gs = pair if gs is None else jnp.maximum(gs, pair)
        if best_gs is None:
            best_gs, grp = gs, jnp.zeros(gs.shape, jnp.int32)
        else:
            better = gs > best_gs
            grp = jnp.where(better, gidx, grp)
            best_gs = jnp.where(better, gs, best_gs)

    def pick(a, i):
        out = row(a, i)
        for gidx in range(1, N_GROUPS):
            out = jnp.where(grp == gidx, row(a, gidx * EXPERTS_PER_GROUP + i), out)
        return out

    cand = [pick(sel, i) for i in range(EXPERTS_PER_GROUP)]
    cand_score = [pick(scores, i) for i in range(EXPERTS_PER_GROUP)]

    def argmax_first(vals, exclude=None):
        best, idx = None, None
        for i, v in enumerate(vals):
            if exclude is not None:
                v = jnp.where(exclude == i, NEG_INF, v)
            if best is None:
                best, idx = v, jnp.zeros(v.shape, jnp.int32)
            else:
                better = v > best
                idx = jnp.where(better, i, idx)
                best = jnp.where(better, v, best)
        return idx

    loc1 = argmax_first(cand)
    loc2 = argmax_first(cand, exclude=loc1)

    def take(vals, idx):
        out = vals[0]
        for i in range(1, len(vals)):
            out = jnp.where(idx == i, vals[i], out)
        return out

    g1 = take(cand_score, loc1)
    g2 = take(cand_score, loc2)
    gsum = g1 + g2
    first_is_a = loc1 < loc2
    a = jnp.minimum(loc1, loc2)
    b = jnp.maximum(loc1, loc2)
    pair = jnp.where(a == 0, b - 1, jnp.where(a == 1, b + 1, PAIRS_PER_GROUP - 1))
    bucket = grp * PAIRS_PER_GROUP + pair
    gate_a = jnp.where(first_is_a, g1, g2) / gsum
    gate_b = jnp.where(first_is_a, g2, g1) / gsum

    n_rows = base_sc.shape[0]
    hit = lax.broadcasted_iota(jnp.int32, (n_rows, tm), 0) == bucket
    ones = jnp.where(hit, 1.0, 0.0)
    before = _dot(ones.astype(BF16), tri_ref[...]) + base_sc[...]
    rank = jnp.sum(jnp.where(hit, before, 0.0), axis=0, keepdims=True)
    base_sc[...] = base_sc[...] + jnp.sum(ones, axis=1, keepdims=True)
    cnt_ref[...] = jnp.broadcast_to(base_sc[...], cnt_ref.shape)

    zero = jnp.zeros((1, tm), F32)
    o_ref[...] = jnp.concatenate([bucket.astype(F32), rank, gate_a, gate_b, zero, zero, zero, zero], axis=0)


def _router(x, whi, wlo, br):
    m_rows, d = x.shape
    tm = _token_tile(m_rows)
    tri = jnp.triu(jnp.ones((tm, tm), BF16), k=1)
    return pl.pallas_call(
        _router_kernel, grid=(m_rows // tm,),
        in_specs=[pl.BlockSpec((tm, d), lambda i: (i, 0)), _full(whi.shape), _full(wlo.shape), _full(br.shape),
                  _full(tri.shape)],
        out_specs=(pl.BlockSpec((SUBLANES, tm), lambda i: (0, i)), _full((N_BUCKETS, LANES))),
        out_shape=(jax.ShapeDtypeStruct((SUBLANES, m_rows), F32), jax.ShapeDtypeStruct((N_BUCKETS, LANES), F32)),
        scratch_shapes=[pltpu.VMEM((N_BUCKETS, 1), F32)],
        compiler_params=_params(("arbitrary",)), name="router",
    )(x, whi, wlo, br, tri)


def _row_copy(src_ref, src_row, dst_ref, dst_row, sem):
    return pltpu.make_async_copy(src_ref.at[pl.ds(src_row, 1), :], dst_ref.at[pl.ds(dst_row, 1), :], sem)


HI_HALF = 0xFFFF0000


def _pack_bf16_pairs(x):
    n = x.shape[1] // 2
    lo = pltpu.bitcast(x[:, :n].astype(BF16).astype(F32), jnp.uint32)
    hi = pltpu.bitcast(x[:, n:].astype(BF16).astype(F32), jnp.uint32)
    return (lo >> 16) | (hi & jnp.uint32(HI_HALF))


def _unpack_bf16_pairs(p):
    lo = pltpu.bitcast(p << 16, F32).astype(BF16)
    hi = pltpu.bitcast(p & jnp.uint32(HI_HALF), F32).astype(BF16)
    return jnp.concatenate([lo, hi], axis=1)


def _dispatch_kernel(dest_ref, x_ref, gate_ref, buf_in_ref, buf_ref, xp_sc, sem):
    del buf_in_ref
    tm, d = x_ref.shape
    step = pl.program_id(0)
    slot = step % 2

    def wait_slot(sl):
        pltpu.make_async_copy(xp_sc.at[sl], buf_ref.at[pl.ds(0, tm), :], sem.at[sl]).wait()

    @pl.when(step >= 2)
    def _():
        wait_slot(slot)

    stage = xp_sc.at[slot]
    stage[:, 0:d // 2] = _pack_bf16_pairs(x_ref[...])
    gate = gate_ref[...]
    lane = lax.broadcasted_iota(jnp.int32, (tm, LANES), 1)
    stage[:, d // 2:] = pltpu.bitcast(jnp.where(lane == 0, gate[:, 0:1], jnp.where(lane == 1, gate[:, 1:2], 0.0)),
                                      jnp.uint32)

    for r in range(tm):
        _row_copy(stage, r, buf_ref, dest_ref[0, 0, r], sem.at[slot]).start(priority=r % 2)

    @pl.when(step == pl.num_programs(0) - 1)
    def _():
        wait_slot(slot)

        @pl.when(step >= 1)
        def _():
            wait_slot(1 - slot)


def _dispatch(x, gates, dest_tiles, n_slots):
    m_rows, d = x.shape
    tm = dest_tiles.shape[2]
    width = d // 2 + LANES
    buf0 = jnp.zeros((n_slots, width), jnp.uint32)
    return pl.pallas_call(
        _dispatch_kernel, grid=(m_rows // tm,),
        in_specs=[pl.BlockSpec((1, 1, tm), lambda i: (i, 0, 0), memory_space=pltpu.SMEM),
                  pl.BlockSpec((tm, d), lambda i: (i, 0)),
                  pl.BlockSpec((tm, 2), lambda i: (i, 0)),
                  pl.BlockSpec(memory_space=pl.ANY)],
        out_specs=pl.BlockSpec(memory_space=pl.ANY),
        out_shape=jax.ShapeDtypeStruct((n_slots, width), jnp.uint32),
        scratch_shapes=[pltpu.VMEM((2, tm, width), jnp.uint32), pltpu.SemaphoreType.DMA((2,))],
        input_output_aliases={3: 0},
        compiler_params=_params(("arbitrary",)), name="moe_dispatch",
    )(dest_tiles, x, gates, buf0)


def _expert_kernel(tea_ref, teb_ref, tv_ref, x_ref, win_a_ref, wout_a_ref, win_b_ref, wout_b_ref, o_ref):
    del tea_ref, teb_ref
    i = pl.program_id(0)
    de = wout_a_ref.shape[0]
    half = win_a_ref.shape[0] // 2

    @pl.when(tv_ref[i] != 0)
    def _():
        xb = _unpack_bf16_pairs(x_ref[:, 0:half])
        gate = pltpu.bitcast(x_ref[:, half:], F32)

        def expert(win_ref, wout_ref):
            h = _dot(xb, win_ref[...])
            hg = h[:, 0:de]
            act = hg * _sigmoid(hg) * h[:, de:2 * de]
            return _dot(act.astype(BF16), wout_ref[...])

        o_ref[...] = gate[:, 0:1] * expert(win_a_ref, wout_a_ref) + gate[:, 1:2] * expert(win_b_ref, wout_b_ref)

    @pl.when(tv_ref[i] == 0)
    def _():
        o_ref[...] = jnp.zeros(o_ref.shape, F32)


def _experts(buf, tile_ea, tile_eb, tile_valid, w_in, w_out, layer, tile_rows):
    n_slots, width = buf.shape
    d, de2 = w_in.shape[2], w_in.shape[3]
    grid_spec = pltpu.PrefetchScalarGridSpec(
        num_scalar_prefetch=3, grid=(n_slots // tile_rows,),
        in_specs=[pl.BlockSpec((tile_rows, width), lambda i, ea, eb, tv: (i, 0)),
                  pl.BlockSpec((None, None, d, de2), lambda i, ea, eb, tv: (layer, ea[i], 0, 0)),
                  pl.BlockSpec((None, None, de2 // 2, d), lambda i, ea, eb, tv: (layer, ea[i], 0, 0)),
                  pl.BlockSpec((None, None, d, de2), lambda i, ea, eb, tv: (layer, eb[i], 0, 0)),
                  pl.BlockSpec((None, None, de2 // 2, d), lambda i, ea, eb, tv: (layer, eb[i], 0, 0))],
        out_specs=pl.BlockSpec((tile_rows, d), lambda i, ea, eb, tv: (i, 0)))
    return pl.pallas_call(
        _expert_kernel, grid_spec=grid_spec, out_shape=jax.ShapeDtypeStruct((n_slots, d), F32),
        compiler_params=_params(("arbitrary",)), name="moe_experts",
    )(tile_ea, tile_eb, tile_valid, buf, w_in, w_out, w_in, w_out)


def _combine_kernel(dest_ref, dest_next_ref, x_ref, g_ref, b_ref, y_ref, o_ref, y_sc, sem):
    tm = x_ref.shape[0]
    i = pl.program_id(0)
    slot = i % 2

    def gather(idx_ref, sl):
        for r in range(tm):
            _row_copy(y_ref, idx_ref[0, 0, r], y_sc.at[sl], r, sem.at[sl]).start(priority=r % 2)

    @pl.when(i == 0)
    def _():
        gather(dest_ref, 0)

    @pl.when(i + 1 < pl.num_programs(0))
    def _():
        gather(dest_next_ref, 1 - slot)

    pltpu.make_async_copy(y_ref.at[pl.ds(0, tm), :], y_sc.at[slot], sem.at[slot]).wait()
    o_ref[...] = _layer_norm(ALPHA * x_ref[...] + y_sc[slot], g_ref[...], b_ref[...])


def _combine(x, dest_tiles, y, g, b):
    m_rows, d = x.shape
    tm = dest_tiles.shape[2]
    last = m_rows // tm - 1
    return pl.pallas_call(
        _combine_kernel, grid=(m_rows // tm,),
        in_specs=[pl.BlockSpec((1, 1, tm), lambda i: (i, 0, 0), memory_space=pltpu.SMEM),
                  pl.BlockSpec((1, 1, tm), lambda i: (jnp.minimum(i + 1, last), 0, 0), memory_space=pltpu.SMEM),
                  pl.BlockSpec((tm, d), lambda i: (i, 0)),
                  _full(g.shape), _full(b.shape),
                  pl.BlockSpec(memory_space=pl.ANY)],
        out_specs=pl.BlockSpec((tm, d), lambda i: (i, 0)),
        out_shape=jax.ShapeDtypeStruct((m_rows, d), F32),
        scratch_shapes=[pltpu.VMEM((2, tm, d), F32), pltpu.SemaphoreType.DMA((2,))],
        compiler_params=_params(("arbitrary",)), name="moe_combine",
    )(dest_tiles, dest_tiles, x, g, b, y)


def _moe_layer(x, rw, w_in, w_out, layer, g, b, expert_tile):
    m_rows, _ = x.shape
    tm = min(m_rows, MOE_ROW_TILE)
    assert m_rows % tm == 0
    route, counts = _router(x, rw["whi"], rw["wlo"], rw["br"])
    bucket = route[0].astype(jnp.int32)
    rank = route[1].astype(jnp.int32)
    gates = route[2:4].T
    counts = counts[:, 0].astype(jnp.int32)
    padded = (counts + expert_tile - 1) // expert_tile * expert_tile
    pad_ends = jnp.cumsum(padded)
    pad_starts = pad_ends - padded
    b_range = jnp.arange(N_BUCKETS, dtype=jnp.int32)
    dest = jnp.sum(jnp.where(bucket[:, None] == b_range, pad_starts, 0), axis=-1) + rank
    n_tiles = -(-(m_rows + N_BUCKETS * (expert_tile - 1)) // expert_tile)
    tile_start = jnp.arange(n_tiles, dtype=jnp.int32) * expert_tile
    tile_bucket = jnp.minimum(jnp.sum((tile_start[:, None] >= pad_ends[None, :]).astype(jnp.int32), axis=1),
                              N_BUCKETS - 1)
    tile_valid = (tile_start < pad_ends[-1]).astype(jnp.int32)
    pair = tile_bucket % PAIRS_PER_GROUP
    group_base = tile_bucket // PAIRS_PER_GROUP * EXPERTS_PER_GROUP
    pair_range = jnp.arange(PAIRS_PER_GROUP, dtype=jnp.int32)
    tile_ea = group_base + jnp.sum(jnp.where(pair[:, None] == pair_range, jnp.asarray(PAIR_FIRST, jnp.int32), 0), axis=1)
    tile_eb = group_base + jnp.sum(jnp.where(pair[:, None] == pair_range, jnp.asarray(PAIR_SECOND, jnp.int32), 0), axis=1)
    dest_tiles = dest.reshape(m_rows // tm, 1, tm)
    buf = _dispatch(x, gates, dest_tiles, n_tiles * expert_tile)
    y = _experts(buf, tile_ea, tile_eb, tile_valid, w_in, w_out, layer, expert_tile)
    return _combine(x, dest_tiles, y, g, b)


def _block_diag(w):
    h, n, _ = w.shape
    eye = jnp.eye(h, dtype=w.dtype)
    return (eye[:, None, :, None] * w[:, :, None, :]).reshape(h * n, h * n)


def _pair_blocks(w):
    h, a, b = w.shape
    w = w.reshape(h // 2, 2, a, b)
    eye = jnp.eye(2, dtype=w.dtype)
    return (eye[None, :, None, :, None] * w[:, :, :, None, :]).reshape(h // 2, 2 * a, 2 * b)


def _time_major(a):
    b, t, c = a.shape
    return a.transpose(1, 0, 2).reshape(t * b, c)


def _batch_major(a, nb):
    return a.reshape(-1, nb, a.shape[-1]).transpose(1, 0, 2)


def _rope_tables(pos):
    half = QK_ROPE // 2
    inv = ROPE_THETA ** (-jnp.arange(half, dtype=F32) / half)
    ang = pos.astype(F32)[:, None] * inv
    cos, sin = jnp.cos(ang), jnp.sin(ang)
    cos_t = jnp.tile(jnp.concatenate([cos, cos], axis=1), (1, N_HEADS))
    sin_t = jnp.tile(jnp.concatenate([-sin, sin], axis=1), (1, N_HEADS))
    return cos_t, sin_t


def kernel(x_prompt, x_sample, state_conv_a, state_conv_b, state_rglru_h, state_conv_c, cache_ckv, cache_kpe, page_table, w_in_even, conv_a_w, conv_b_w, conv_b_b, lru_wa, lru_ba, lru_wx, lru_bx, lru_lambda, w_out_even, w_in_odd, conv_c_w, conv_c_b, ln_c_g, ln_c_b, q_norm_g, w_q_b, kv_norm_g, w_uk, w_uv, w_out_odd, ln_mix_g, ln_mix_b, ln_ffn_g, ln_ffn_b, w_router, b_router, w_exp_in, w_exp_out):
    bp, n_tp, d = x_prompt.shape
    bs, n_ts, _ = x_sample.shape
    past_len = page_table.shape[1] * PAGE_SIZE
    dc = conv_c_w.shape[2]
    q_rank = q_norm_g.shape[1]
    kv_rank = kv_norm_g.shape[1]

    xp = _time_major(x_prompt)
    xs = _time_major(x_sample)
    row2 = lambda v: v.reshape(1, -1)

    wr = jnp.pad(w_router, ((0, 0), (0, LANES - N_EXPERTS)))
    wr_hi = wr.astype(BF16)
    router_w = {"whi": wr_hi, "wlo": (wr - wr_hi.astype(F32)).astype(BF16), "br": b_router.reshape(N_EXPERTS, 1)}
    rope_p = _rope_tables(jnp.arange(n_tp))
    rope_s = _rope_tables(past_len + jnp.arange(n_ts))
    we_in, we_out = w_exp_in.astype(BF16), w_exp_out.astype(BF16)
    cache_kpe_t = jnp.swapaxes(cache_kpe, 2, 3)

    outs = {k: [] for k in ("ca_p", "ca_s", "cb_p", "cb_s", "h_p", "h_s", "cc_p", "cc_s", "ckv_p", "ckv_s", "kpe_p", "kpe_s")}
    for l in range(DEPTH):
        j = l // 2
        lg, lb = row2(ln_mix_g[l]), row2(ln_mix_b[l])
        if l % 2 == 0:
            wa_bd, wx_bd = _block_diag(lru_wa[j]), _block_diag(lru_wx[j])
            hc = wa_bd.shape[0] // 2
            wg = jnp.stack([jnp.concatenate([wa_bd[s * hc:(s + 1) * hc, s * hc:(s + 1) * hc],
                                             wx_bd[s * hc:(s + 1) * hc, s * hc:(s + 1) * hc]], axis=1)
                            for s in range(2)]).astype(BF16)
            w = {"win": w_in_even[j].astype(BF16), "caw": conv_a_w[j], "cbw": conv_b_w[j], "cbb": row2(conv_b_b[j]),
                 "wg": wg, "ba": row2(lru_ba[j]), "bx": row2(lru_bx[j]), "lam": row2(lru_lambda[j]),
                 "wout": w_out_even[j].astype(BF16), "g": lg, "b": lb}
            da = conv_a_w.shape[2]
            db = conv_b_w.shape[2]
            xp, a1, b1, h1 = _even_layer(xp, w, jnp.zeros((2 * bp, da), F32), jnp.zeros((3 * bp, db), F32),
                                         jnp.zeros((bp, db), F32), nb=bp, n_t=n_tp)
            xs, a2, b2, h2 = _even_layer(xs, w, _time_major(state_conv_a[j]), _time_major(state_conv_b[j]),
                                         state_rglru_h[j], nb=bs, n_t=n_ts)
            outs["ca_p"].append(_batch_major(a1, bp)); outs["ca_s"].append(_batch_major(a2, bs))
            outs["cb_p"].append(_batch_major(b1, bp)); outs["cb_s"].append(_batch_major(b2, bs))
            outs["h_p"].append(h1); outs["h_s"].append(h2)
        else:
            nope_all = N_HEADS * QK_NOPE
            wq = w_q_b[j].reshape(q_rank, N_HEADS, QK_NOPE + QK_ROPE)
            wqb = jnp.concatenate([wq[:, :, :QK_NOPE].reshape(q_rank, nope_all),
                                   wq[:, :, QK_NOPE:].reshape(q_rank, N_HEADS * QK_ROPE)], axis=1).astype(BF16)
            o_pe = 2 * dc + q_rank + kv_rank
            win = jnp.concatenate([w_in_odd[j][:, :o_pe], jnp.tile(w_in_odd[j][:, o_pe:], (1, N_HEADS))], axis=1).astype(BF16)
            wuk = _pair_blocks(w_uk[j].transpose(1, 2, 0)).astype(BF16)
            wuv = _pair_blocks(w_uv[j].transpose(1, 0, 2)).astype(BF16)
            w = {"win": win, "ccw": conv_c_w[j], "ccb": row2(conv_c_b[j]), "lcg": row2(ln_c_g[j]), "lcb": row2(ln_c_b[j]),
                 "qg": row2(q_norm_g[j]), "wqb": wqb, "kvg": row2(kv_norm_g[j]), "wuk": wuk}
            width = conv_c_w.shape[1]
            wout = w_out_odd[j].astype(BF16)
            ycp, qp, kp, ckv1, kpe1, c1 = _odd_pre(xp, w, jnp.zeros(((width - 1) * bp, dc), F32), *rope_p, nb=bp, n_t=n_tp)
            ycs, qs, ks, ckv2, kpe2, c2 = _odd_pre(xs, w, _time_major(state_conv_c[j]), *rope_s, nb=bs, n_t=n_ts)
            ydp = _prompt_attention(qp, kp, wuv)
            yds = _sample_attention(qs, ks, wuv, cache_ckv, cache_kpe_t, page_table, j)
            xp = _odd_post(xp, ycp, ydp, wout, lg, lb, nb=bp, n_t=n_tp)
            xs = _odd_post(xs, ycs, yds, wout, lg, lb, nb=bs, n_t=n_ts)
            outs["cc_p"].append(_batch_major(c1, bp)); outs["cc_s"].append(_batch_major(c2, bs))
            outs["ckv_p"].append(ckv1); outs["ckv_s"].append(ckv2)
            outs["kpe_p"].append(kpe1); outs["kpe_s"].append(kpe2)
        fg, fb = row2(ln_ffn_g[l]), row2(ln_ffn_b[l])
        xp = _moe_layer(xp, router_w, we_in, we_out, l, fg, fb, expert_tile=512)
        xs = _moe_layer(xs, router_w, we_in, we_out, l, fg, fb, expert_tile=64)

    y_prompt = _batch_major(xp, bp)
    y_sample = _batch_major(xs, bs)
    st = lambda k: jnp.stack(outs[k])
    st1 = lambda k: jnp.stack(outs[k], axis=1)
    return (y_prompt, y_sample, st("ca_p"), st("ca_s"), st("cb_p"), st("cb_s"), st("h_p"), st("h_s"),
            st("cc_p"), st("cc_s"), st1("ckv_p"), st1("ckv_s"), st1("kpe_p"), st1("kpe_s"))
```

```python
import functools

import jax
import jax.numpy as jnp
from jax import lax
from jax.experimental import pallas as pl
from jax.experimental.pallas import tpu as pltpu

F32 = jnp.float32
BF16 = jnp.bfloat16

DEPTH = 4
N_HEADS = 8
QK_NOPE = 64
QK_ROPE = 32
V_DIM = 64
N_EXPERTS = 16
N_GROUPS = 4
EXPERTS_PER_GROUP = 4
PAIRS_PER_GROUP = 6
N_BUCKETS = N_GROUPS * PAIRS_PER_GROUP
PAIR_FIRST = (0, 0, 0, 1, 1, 2)
PAIR_SECOND = (1, 2, 3, 2, 3, 3)
LRU_C = 8.0
ROPE_THETA = 10000.0
ATTN_SCALE = (QK_NOPE + QK_ROPE) ** -0.5
ALPHA = (2 * DEPTH) ** 0.25
PAGE_SIZE = 128

SUBLANES = 8
BF16_SUBLANES = 16
LANES = 128
VMEM_LIMIT_BYTES = 56 * 1024 * 1024

TIME_STEPS_PER_TILE = 32
TOKEN_TILE = 512
MOE_ROW_TILE = 1024
ATTN_TILE = 512
PAGES_PER_STEP = 64
NEG_INF = float("-inf")


def _params(semantics):
    return pltpu.CompilerParams(dimension_semantics=semantics, vmem_limit_bytes=VMEM_LIMIT_BYTES)


def _full(shape):
    nd = len(shape)
    return pl.BlockSpec(shape, lambda *_: (0,) * nd)


def _layer_norm(x, g, b, eps=1e-5):
    mu = jnp.mean(x, axis=-1, keepdims=True)
    xc = x - mu
    var = jnp.mean(xc * xc, axis=-1, keepdims=True)
    return xc * lax.rsqrt(var + eps) * g + b


def _rms_norm(x, g, eps=1e-6):
    return x * lax.rsqrt(jnp.mean(x * x, axis=-1, keepdims=True) + eps) * g


def _sigmoid(x):
    return 1.0 / (1.0 + jnp.exp(-x))


def _dot(a, b):
    return jnp.dot(a, b, preferred_element_type=F32)


def _dot_nt(a, b):
    return lax.dot_general(a, b, (((1,), (1,)), ((), ())), preferred_element_type=F32)


def _seq_tile(nb, n_t):
    tt = min(n_t, TIME_STEPS_PER_TILE)
    assert n_t % tt == 0 and nb % SUBLANES == 0
    return tt


def _even_kernel(x_ref, win_ref, caw_ref, cbw_ref, cbb_ref, wg_ref, ba_ref, bx_ref, lam_ref, wout_ref,
                 g_ref, b_ref, sa_ref, sb_ref, h0_ref,
                 o_ref, na_ref, nb_ref, hl_ref,
                 ua_ext, vb_ext, h_sc, hs_sc, *, nb, tt):
    tm = tt * nb
    dh = caw_ref.shape[1]
    half = dh // 2

    @pl.when(pl.program_id(0) == 0)
    def _():
        ua_ext[0:2 * nb, :] = sa_ref[...]
        vb_ext[0:3 * nb, :] = sb_ref[...]
        h_sc[...] = h0_ref[...]

    x = x_ref[...]
    xb = x.astype(BF16)

    def proj(j):
        return _dot(xb, win_ref[:, j * dh:(j + 1) * dh])

    ua_ext[2 * nb:2 * nb + tm, :] = proj(1) * proj(2)
    caw = caw_ref[...]
    conv_a = (caw[0:1] * ua_ext[0:tm, :] + caw[1:2] * ua_ext[nb:nb + tm, :]
              + caw[2:3] * ua_ext[2 * nb:2 * nb + tm, :])
    y_a = proj(0) * conv_a
    tail_a = ua_ext[tm:tm + 2 * nb, :]
    na_ref[...] = tail_a
    ua_ext[0:2 * nb, :] = tail_a

    vb_ext[3 * nb:3 * nb + tm, :] = proj(4)
    cbw = cbw_ref[...]
    u_b = (cbb_ref[...] + cbw[0:1] * vb_ext[0:tm, :] + cbw[1:2] * vb_ext[nb:nb + tm, :]
           + cbw[2:3] * vb_ext[2 * nb:2 * nb + tm, :] + cbw[3:4] * vb_ext[3 * nb:3 * nb + tm, :])
    tail_b = vb_ext[tm:tm + 3 * nb, :]
    nb_ref[...] = tail_b
    vb_ext[0:3 * nb, :] = tail_b

    ub16 = u_b.astype(BF16)
    gk0 = _dot(ub16[:, :half], wg_ref[0])
    gk1 = _dot(ub16[:, half:], wg_ref[1])
    r = _sigmoid(jnp.concatenate([gk0[:, :half], gk1[:, :half]], axis=1) + ba_ref[...])
    ig = _sigmoid(jnp.concatenate([gk0[:, half:], gk1[:, half:]], axis=1) + bx_ref[...])
    nlam = -lam_ref[...]
    softplus = jnp.maximum(nlam, 0.0) + jnp.log(1.0 + jnp.exp(-jnp.abs(nlam)))
    log_a = (-LRU_C * r) * softplus
    a = jnp.exp(log_a)
    bterm = jnp.sqrt(1.0 - a * a) * (ig * u_b)

    h = h_sc[...]
    for t in range(tt):
        h = a[t * nb:(t + 1) * nb, :] * h + bterm[t * nb:(t + 1) * nb, :]
        hs_sc[t * nb:(t + 1) * nb, :] = h
    h_sc[...] = h
    hl_ref[...] = h
    y_b = jax.nn.gelu(proj(3), approximate=True) * hs_sc[...]

    m = _dot(y_a.astype(BF16), wout_ref[0:dh, :]) + _dot(y_b.astype(BF16), wout_ref[dh:2 * dh, :])
    o_ref[...] = _layer_norm(ALPHA * x + m, g_ref[...], b_ref[...])


def _even_layer(x, w, sa, sb, h0, *, nb, n_t):
    m_rows, d = x.shape
    dh = w["caw"].shape[1]
    tt = _seq_tile(nb, n_t)
    tm = tt * nb
    kern = functools.partial(_even_kernel, nb=nb, tt=tt)
    row = lambda i: (i, 0)
    in_specs = [pl.BlockSpec((tm, d), row), _full(w["win"].shape), _full(w["caw"].shape), _full(w["cbw"].shape),
                _full(w["cbb"].shape), _full(w["wg"].shape), _full(w["ba"].shape), _full(w["bx"].shape),
                _full(w["lam"].shape), _full(w["wout"].shape), _full(w["g"].shape), _full(w["b"].shape),
                _full(sa.shape), _full(sb.shape), _full(h0.shape)]
    out_shape = (jax.ShapeDtypeStruct((m_rows, d), F32), jax.ShapeDtypeStruct(sa.shape, F32),
                 jax.ShapeDtypeStruct(sb.shape, F32), jax.ShapeDtypeStruct(h0.shape, F32))
    out_specs = (pl.BlockSpec((tm, d), row), _full(sa.shape), _full(sb.shape), _full(h0.shape))
    scratch = [pltpu.VMEM((tm + 2 * nb, dh), F32), pltpu.VMEM((tm + 3 * nb, dh), F32),
               pltpu.VMEM((nb, dh), F32), pltpu.VMEM((tm, dh), F32)]
    return pl.pallas_call(
        kern, grid=(m_rows // tm,), in_specs=in_specs, out_specs=out_specs, out_shape=out_shape,
        scratch_shapes=scratch, compiler_params=_params(("arbitrary",)), name="even_mixer",
    )(x, w["win"], w["caw"], w["cbw"], w["cbb"], w["wg"], w["ba"], w["bx"], w["lam"], w["wout"], w["g"], w["b"],
      sa, sb, h0)


def _odd_pre_kernel(x_ref, win_ref, ccw_ref, ccb_ref, lcg_ref, lcb_ref, qg_ref, wqb_ref, kvg_ref, wuk_ref,
                    cos_ref, sin_ref, sc_ref, perm_ref,
                    yc_ref, q_ref, k_ref, ckv_ref, kpe_ref, nc_ref,
                    c_ext, rl_sc, *, nb, tt, width):
    tm = tt * nb
    dc = ccw_ref.shape[1]
    q_rank = qg_ref.shape[1]
    kv_rank = kvg_ref.shape[1]
    pe_all = N_HEADS * QK_ROPE
    hist = (width - 1) * nb

    @pl.when(pl.program_id(0) == 0)
    def _():
        c_ext[0:hist, :] = sc_ref[...]

    xb = x_ref[...].astype(BF16)
    o_q = 2 * dc
    o_kv = o_q + q_rank
    o_pe = o_kv + kv_rank

    glu = _dot(xb, win_ref[:, 0:dc]) * _sigmoid(_dot(xb, win_ref[:, dc:2 * dc]))
    c_ext[hist:hist + tm, :] = glu
    ccw = ccw_ref[...]
    u_c = ccb_ref[...] + ccw[0:1] * c_ext[0:tm, :]
    for k in range(1, width):
        u_c = u_c + ccw[k:k + 1] * c_ext[k * nb:k * nb + tm, :]
    tail = c_ext[tm:tm + hist, :]
    nc_ref[...] = tail
    c_ext[0:hist, :] = tail
    ln = _layer_norm(u_c, lcg_ref[...], lcb_ref[...])
    yc_ref[...] = ln * _sigmoid(ln)

    def to_sequences(val, store):
        n_lane_tiles = val.shape[1] // LANES
        for c in range(n_lane_tiles):
            rl_sc[c] = val[:, c * LANES:(c + 1) * LANES]

        def body(b, carry):
            store(b, jnp.concatenate([rl_sc[c, pl.ds(b, tt, stride=nb), :] for c in range(n_lane_tiles)], axis=1))
            return carry

        lax.fori_loop(0, nb, body, 0)

    lane = lax.broadcasted_iota(jnp.int32, (1, pe_all), 1)
    first_half = (lane % QK_ROPE) < (QK_ROPE // 2)
    head_of_lane = lane // QK_ROPE

    permute_on_mxu = tt % BF16_SUBLANES == 0
    if permute_on_mxu:
        src = _dot(perm_ref[...], xb).astype(BF16)
        cos = jnp.tile(cos_ref[...], (nb, 1))
        sin = jnp.tile(sin_ref[...], (nb, 1))
    else:
        src = xb
        cos = jnp.broadcast_to(cos_ref[...][:, None, :], (tt, nb, pe_all)).reshape(tm, pe_all)
        sin = jnp.broadcast_to(sin_ref[...][:, None, :], (tt, nb, pe_all)).reshape(tm, pe_all)

    def rope(v):
        swapped = jnp.where(first_half, pltpu.roll(v, pe_all - QK_ROPE // 2, 1), pltpu.roll(v, QK_ROPE // 2, 1))
        return v * cos + swapped * sin

    def emit(val, store):
        if permute_on_mxu:
            for b in range(nb):
                store(b, val[b * tt:(b + 1) * tt])
        else:
            to_sequences(val, store)

    ckv = _rms_norm(_dot(src, win_ref[:, o_kv:o_kv + kv_rank]), kvg_ref[...])
    kpe = rope(_dot(src, win_ref[:, o_pe:o_pe + pe_all]))

    def store_k(b, v):
        k_ref[b] = v.astype(BF16)
        ckv_ref[b] = v[:, 0:kv_rank]
        kpe_ref[b] = v[:, kv_rank:kv_rank + QK_ROPE]

    emit(jnp.concatenate([ckv, kpe], axis=1), store_k)

    qn = _rms_norm(_dot(src, win_ref[:, o_q:o_q + q_rank]), qg_ref[...]).astype(BF16)
    nope_all = N_HEADS * QK_NOPE
    q_nope = (_dot(qn, wqb_ref[:, 0:nope_all]) * ATTN_SCALE).astype(BF16)
    q_pe = rope(_dot(qn, wqb_ref[:, nope_all:nope_all + pe_all]) * ATTN_SCALE)
    for p in range(N_HEADS // 2):
        lat2 = _dot(q_nope[:, p * 2 * QK_NOPE:(p + 1) * 2 * QK_NOPE], wuk_ref[p])
        for s in range(2):
            h = 2 * p + s

            def store_q(b, v, h=h):
                q_ref[b, h] = v.astype(BF16)

            emit(jnp.concatenate([lat2[:, s * kv_rank:(s + 1) * kv_rank],
                                  jnp.where(head_of_lane == h, q_pe, 0.0)], axis=1), store_q)


def _odd_pre(x, w, sc, cos, sin, *, nb, n_t):
    m_rows, d = x.shape
    width, dc = w["ccw"].shape
    kv_rank = w["kvg"].shape[1]
    pe_all = N_HEADS * QK_ROPE
    kq = kv_rank + pe_all
    tt = _seq_tile(nb, n_t)
    tm = tt * nb
    kern = functools.partial(_odd_pre_kernel, nb=nb, tt=tt, width=width)
    row = lambda i: (i, 0)
    seq = lambda i: (0, i, 0)
    in_specs = [pl.BlockSpec((tm, d), row)] + [_full(w[k].shape) for k in
                                                ("win", "ccw", "ccb", "lcg", "lcb", "qg", "wqb", "kvg", "wuk")]
    out_row = jnp.arange(tm, dtype=jnp.int32)
    src_row = (out_row % tt) * nb + out_row // tt
    perm = (src_row[:, None] == out_row[None, :]).astype(BF16)
    in_specs += [pl.BlockSpec((tt, pe_all), row), pl.BlockSpec((tt, pe_all), row), _full(sc.shape), _full(perm.shape)]
    out_shape = (jax.ShapeDtypeStruct((m_rows, dc), F32),
                 jax.ShapeDtypeStruct((nb, N_HEADS, n_t, kq), BF16),
                 jax.ShapeDtypeStruct((nb, n_t, kq), BF16),
                 jax.ShapeDtypeStruct((nb, n_t, kv_rank), F32),
                 jax.ShapeDtypeStruct((nb, n_t, QK_ROPE), F32),
                 jax.ShapeDtypeStruct(sc.shape, F32))
    out_specs = (pl.BlockSpec((tm, dc), row), pl.BlockSpec((nb, N_HEADS, tt, kq), lambda i: (0, 0, i, 0)),
                 pl.BlockSpec((nb, tt, kq), seq), pl.BlockSpec((nb, tt, kv_rank), seq),
                 pl.BlockSpec((nb, tt, QK_ROPE), seq), _full(sc.shape))
    scratch = [pltpu.VMEM((tm + (width - 1) * nb, dc), F32), pltpu.VMEM((kq // LANES, tm, LANES), F32)]
    return pl.pallas_call(
        kern, grid=(m_rows // tm,), in_specs=in_specs, out_specs=out_specs, out_shape=out_shape,
        scratch_shapes=scratch, compiler_params=_params(("arbitrary",)), name="odd_pre",
    )(x, w["win"], w["ccw"], w["ccb"], w["lcg"], w["lcb"], w["qg"], w["wqb"], w["kvg"], w["wuk"], cos, sin, sc, perm)


def _softmax_step(s, v16, m_sc, l_sc, acc_sc):
    m_prev = m_sc[...]
    m_new = jnp.maximum(m_prev, jnp.max(s, axis=-1, keepdims=True))
    alpha = jnp.exp(m_prev - m_new)
    p = jnp.exp(s - m_new)
    l_sc[...] = alpha * l_sc[...] + jnp.sum(p, axis=-1, keepdims=True)
    acc_sc[...] = alpha * acc_sc[...] + _dot(p.astype(BF16), v16)
    m_sc[...] = m_new


def _softmax_init(m_sc, l_sc, acc_sc):
    m_sc[...] = jnp.full(m_sc.shape, NEG_INF, F32)
    l_sc[...] = jnp.zeros(l_sc.shape, F32)
    acc_sc[...] = jnp.zeros(acc_sc.shape, F32)


def _prompt_attn_kernel(qi_ref, ki_ref, q_ref, k_ref, bias_ref, wuv_ref, o_ref, *state, kv_rank):
    m_sc, l_sc, acc_sc = state[0:N_HEADS], state[N_HEADS:2 * N_HEADS], state[2 * N_HEADS:3 * N_HEADS]
    j = pl.program_id(1)
    qi = qi_ref[j]
    ki = ki_ref[j]

    @pl.when(ki == 0)
    def _():
        for h in range(N_HEADS):
            _softmax_init(m_sc[h], l_sc[h], acc_sc[h])

    k = k_ref[...]
    v_t = k[:, 0:kv_rank].T
    diag = (ki == qi).astype(jnp.int32)

    def scores_t(h):
        return _dot_nt(k, q_ref[h]) + bias_ref[diag]

    s_next = scores_t(0)
    for h in range(N_HEADS):
        s = s_next
        if h + 1 < N_HEADS:
            s_next = scores_t(h + 1)
        m_prev = m_sc[h][...]
        m_new = jnp.maximum(m_prev, jnp.max(s, axis=0, keepdims=True))
        alpha = jnp.exp(m_prev - m_new)
        p = jnp.exp(s - m_new)
        l_sc[h][...] = alpha * l_sc[h][...] + jnp.sum(p, axis=0, keepdims=True)
        acc_sc[h][...] = alpha * acc_sc[h][...] + _dot(v_t, p.astype(BF16))
        m_sc[h][...] = m_new

    @pl.when(ki == qi)
    def _():
        for p in range(N_HEADS // 2):
            o2_t = jnp.concatenate([acc_sc[2 * p][...] / l_sc[2 * p][...],
                                    acc_sc[2 * p + 1][...] / l_sc[2 * p + 1][...]], axis=0)
            o_ref[:, p * 2 * V_DIM:(p + 1) * 2 * V_DIM] = _dot(o2_t.T.astype(BF16), wuv_ref[p])


def _prompt_attention(q, k, wuv):
    nb, n_heads, n_t, kq = q.shape
    kv_rank = wuv.shape[1] // 2
    dv = N_HEADS * V_DIM
    tile = min(n_t, ATTN_TILE)
    assert n_t % tile == 0
    nq = n_t // tile
    pairs = [(a, b) for a in range(nq) for b in range(a + 1)]
    qi_tab = jnp.asarray([p[0] for p in pairs], jnp.int32)
    ki_tab = jnp.asarray([p[1] for p in pairs], jnp.int32)
    causal = jnp.where(jnp.arange(tile)[:, None] <= jnp.arange(tile)[None, :], 0.0, NEG_INF).astype(F32)
    bias = jnp.stack([jnp.zeros((tile, tile), F32), causal])
    kern = functools.partial(_prompt_attn_kernel, kv_rank=kv_rank)
    grid_spec = pltpu.PrefetchScalarGridSpec(
        num_scalar_prefetch=2, grid=(nb, len(pairs)),
        in_specs=[pl.BlockSpec((None, n_heads, tile, kq), lambda b, j, qi, ki: (b, 0, qi[j], 0)),
                  pl.BlockSpec((None, tile, kq), lambda b, j, qi, ki: (b, ki[j], 0)),
                  pl.BlockSpec(bias.shape, lambda b, j, qi, ki: (0, 0, 0)),
                  pl.BlockSpec(wuv.shape, lambda b, j, qi, ki: (0, 0, 0))],
        out_specs=pl.BlockSpec((None, tile, dv), lambda b, j, qi, ki: (b, qi[j], 0)),
        scratch_shapes=([pltpu.VMEM((1, tile), F32)] * (2 * n_heads) + [pltpu.VMEM((kv_rank, tile), F32)] * n_heads))
    return pl.pallas_call(
        kern, grid_spec=grid_spec, out_shape=jax.ShapeDtypeStruct((nb, n_t, dv), F32),
        compiler_params=_params(("arbitrary", "arbitrary")), name="prompt_attention",
    )(qi_tab, ki_tab, q, k, bias, wuv)


def _sample_attn_kernel(pt_ref, q_ref, kn_ref, tsel_ref, wuv_ref, ckv_hbm, kpe_hbm, o_ref,
                        ck_buf, kp_buf, sem, qc_sc, m_sc, l_sc, acc_sc, *,
                        n_pages_step, layer, kv_rank, n_new, rows_per_head):
    b = pl.program_id(0)
    g = pl.program_id(1)
    n_g = pl.num_programs(1)
    step = b * n_g + g
    n_steps = pl.num_programs(0) * n_g
    slot = step % 2

    def page_copies(seq, grp, sl):
        copies = []
        for i in range(n_pages_step):
            page = pt_ref[seq, grp * n_pages_step + i]
            copies.append(pltpu.make_async_copy(
                ckv_hbm.at[page, layer], ck_buf.at[sl, pl.ds(i * PAGE_SIZE, PAGE_SIZE), :], sem.at[sl]))
            copies.append(pltpu.make_async_copy(
                kpe_hbm.at[page, layer], kp_buf.at[sl, :, pl.ds(i * PAGE_SIZE, PAGE_SIZE)], sem.at[sl]))
        return copies

    @pl.when(step == 0)
    def _():
        for c in page_copies(0, 0, 0):
            c.start()

    @pl.when(step + 1 < n_steps)
    def _():
        nxt = step + 1
        for c in page_copies(nxt // n_g, nxt % n_g, 1 - slot):
            c.start()

    @pl.when(g == 0)
    def _():
        _softmax_init(m_sc, l_sc, acc_sc)
        qc_sc[...] = _dot(q_ref[:, kv_rank:], tsel_ref[...]).astype(BF16)

    for c in page_copies(b, g, slot):
        c.wait()

    q = q_ref[...]
    ck = ck_buf[slot].astype(BF16)
    kp_t = kp_buf[slot].astype(BF16)
    s = _dot_nt(q[:, 0:kv_rank], ck) + _dot(qc_sc[...], kp_t)
    _softmax_step(s, ck, m_sc, l_sc, acc_sc)

    @pl.when(g == pl.num_programs(1) - 1)
    def _():
        kn = kn_ref[...]
        s_new = _dot_nt(q, kn)
        t_row = lax.broadcasted_iota(jnp.int32, s_new.shape, 0) % rows_per_head
        t_col = lax.broadcasted_iota(jnp.int32, s_new.shape, 1)
        s_new = jnp.where((t_col <= t_row) & (t_col < n_new), s_new, NEG_INF)
        _softmax_step(s_new, kn[:, 0:kv_rank], m_sc, l_sc, acc_sc)
        o = acc_sc[...] / l_sc[...]
        rp = rows_per_head
        for p in range(N_HEADS // 2):
            o2 = jnp.concatenate([o[2 * p * rp:(2 * p + 1) * rp], o[(2 * p + 1) * rp:(2 * p + 2) * rp]], axis=1)
            o_ref[:, p * 2 * V_DIM:(p + 1) * 2 * V_DIM] = _dot(o2.astype(BF16), wuv_ref[p])


def _sample_attention(q, k, wuv, cache_ckv, cache_kpe_t, page_table, layer):
    nb, n_heads, n_t, kq = q.shape
    kv_rank = wuv.shape[1] // 2
    dv = N_HEADS * V_DIM
    n_pages = page_table.shape[1]
    gp = min(PAGES_PER_STEP, n_pages)
    assert n_pages % gp == 0 and n_t <= SUBLANES
    rp = SUBLANES
    qb = jnp.pad(q, ((0, 0), (0, 0), (0, rp - n_t), (0, 0))).reshape(nb, n_heads * rp, kq)
    kb = jnp.pad(k, ((0, 0), (0, rp - n_t), (0, 0)))
    tsel = jnp.tile(jnp.eye(QK_ROPE, dtype=BF16), (N_HEADS, 1))
    kern = functools.partial(_sample_attn_kernel, n_pages_step=gp, layer=layer, kv_rank=kv_rank, n_new=n_t,
                             rows_per_head=rp)
    in_specs = [pl.BlockSpec((None, n_heads * rp, kq), lambda b, g, pt: (b, 0, 0)),
                pl.BlockSpec((None, rp, kq), lambda b, g, pt: (b, 0, 0)),
                pl.BlockSpec(tsel.shape, lambda b, g, pt: (0, 0)),
                pl.BlockSpec(wuv.shape, lambda b, g, pt: (0, 0, 0)),
                pl.BlockSpec(memory_space=pl.ANY), pl.BlockSpec(memory_space=pl.ANY)]
    grid_spec = pltpu.PrefetchScalarGridSpec(
        num_scalar_prefetch=1, grid=(nb, n_pages // gp), in_specs=in_specs,
        out_specs=pl.BlockSpec((None, rp, dv), lambda b, g, pt: (b, 0, 0)),
        scratch_shapes=[pltpu.VMEM((2, gp * PAGE_SIZE, kv_rank), F32), pltpu.VMEM((2, QK_ROPE, gp * PAGE_SIZE), F32),
                        pltpu.SemaphoreType.DMA((2,)),
                        pltpu.VMEM((n_heads * rp, QK_ROPE), BF16), pltpu.VMEM((n_heads * rp, 1), F32),
                        pltpu.VMEM((n_heads * rp, 1), F32), pltpu.VMEM((n_heads * rp, kv_rank), F32)])
    out = pl.pallas_call(
        kern, grid_spec=grid_spec, out_shape=jax.ShapeDtypeStruct((nb, rp, dv), F32),
        compiler_params=_params(("arbitrary", "arbitrary")), name="sample_attention",
    )(page_table, qb, kb, tsel, wuv, cache_ckv, cache_kpe_t)
    return out[:, :n_t]


def _odd_post_kernel(x_ref, yc_ref, yd_ref, wout_ref, g_ref, b_ref, o_ref, rl_sc, *, nb, tt):
    dc = yc_ref.shape[1]

    n_lane_tiles = rl_sc.shape[0]

    def body(b, carry):
        v = yd_ref[b]
        for c in range(n_lane_tiles):
            rl_sc[c, pl.ds(b, tt, stride=nb), :] = v[:, c * LANES:(c + 1) * LANES]
        return carry

    lax.fori_loop(0, nb, body, 0)
    yd = jnp.concatenate([rl_sc[c] for c in range(n_lane_tiles)], axis=1)
    m = _dot(yc_ref[...].astype(BF16), wout_ref[0:dc, :]) + _dot(yd.astype(BF16), wout_ref[dc:, :])
    o_ref[...] = _layer_norm(ALPHA * x_ref[...] + m, g_ref[...], b_ref[...])


def _odd_post(x, yc, yd, wout, g, b, *, nb, n_t):
    m_rows, d = x.shape
    dv = yd.shape[2]
    tt = _seq_tile(nb, n_t)
    tm = tt * nb
    row = lambda i: (i, 0)
    kern = functools.partial(_odd_post_kernel, nb=nb, tt=tt)
    return pl.pallas_call(
        kern, grid=(m_rows // tm,),
        in_specs=[pl.BlockSpec((tm, d), row), pl.BlockSpec((tm, yc.shape[1]), row),
                  pl.BlockSpec((nb, tt, dv), lambda i: (0, i, 0)),
                  _full(wout.shape), _full(g.shape), _full(b.shape)],
        out_specs=pl.BlockSpec((tm, d), row), out_shape=jax.ShapeDtypeStruct((m_rows, d), F32),
        scratch_shapes=[pltpu.VMEM((dv // LANES, tm, LANES), F32)],
        compiler_params=_params(("arbitrary",)), name="odd_post",
    )(x, yc, yd, wout, g, b)


def _token_tile(m_rows):
    tm = min(m_rows, TOKEN_TILE)
    assert m_rows % tm == 0
    return tm


def _router_kernel(x_ref, whi_ref, wlo_ref, br_ref, tri_ref, o_ref, cnt_ref, base_sc):
    tm = x_ref.shape[0]

    @pl.when(pl.program_id(0) == 0)
    def _():
        base_sc[...] = jnp.zeros(base_sc.shape, F32)

    x = x_ref[...]
    x_hi = x.astype(BF16)
    x_lo = (x - x_hi.astype(F32)).astype(BF16)
    logits = _dot(x_hi, whi_ref[...]) + (_dot(x_lo, whi_ref[...]) + _dot(x_hi, wlo_ref[...]))
    scores = _sigmoid(logits.T[0:N_EXPERTS, :])
    sel = scores + br_ref[...]

    def row(a, e):
        return a[e:e + 1, :]

    best_gs = None
    grp = None
    for gidx in range(N_GROUPS):
        v = [row(sel, gidx * EXPERTS_PER_GROUP + i) for i in range(EXPERTS_PER_GROUP)]
        gs = None
        for i in range(EXPERTS_PER_GROUP):
            for j in range(i + 1, EXPERTS_PER_GROUP):
                pair = v[i] + v[j]
                gs = pair if gs is None else jnp.maximum(gs, pair)
        if best_gs is None:
            best_gs, grp = gs, jnp.zeros(gs.shape, jnp.int32)
        else:
            better = gs > best_gs
            grp = jnp.where(better, gidx, grp)
            best_gs = jnp.where(better, gs, best_gs)

    def pick(a, i):
        out = row(a, i)
        for gidx in range(1, N_GROUPS):
            out = jnp.where(grp == gidx, row(a, gidx * EXPERTS_PER_GROUP + i), out)
        return out

    cand = [pick(sel, i) for i in range(EXPERTS_PER_GROUP)]
    cand_score = [pick(scores, i) for i in range(EXPERTS_PER_GROUP)]

    def argmax_first(vals, exclude=None):
        best, idx = None, None
        for i, v in enumerate(vals):
            if exclude is not None:
                v = jnp.where(exclude == i, NEG_INF, v)
            if best is None:
                best, idx = v, jnp.zeros(v.shape, jnp.int32)
            else:
                better = v > best
                idx = jnp.where(better, i, idx)
                best = jnp.where(better, v, best)
        return idx

    loc1 = argmax_first(cand)
    loc2 = argmax_first(cand, exclude=loc1)

    def take(vals, idx):
        out = vals[0]
        for i in range(1, len(vals)):
            out = jnp.where(idx == i, vals[i], out)
        return out

    g1 = take(cand_score, loc1)
    g2 = take(cand_score, loc2)
    gsum = g1 + g2
    first_is_a = loc1 < loc2
    a = jnp.minimum(loc1, loc2)
    b = jnp.maximum(loc1, loc2)
    pair = jnp.where(a == 0, b - 1, jnp.where(a == 1, b + 1, PAIRS_PER_GROUP - 1))
    bucket = grp * PAIRS_PER_GROUP + pair
    gate_a = jnp.where(first_is_a, g1, g2) / gsum
    gate_b = jnp.where(first_is_a, g2, g1) / gsum

    n_rows = base_sc.shape[0]
    hit = lax.broadcasted_iota(jnp.int32, (n_rows, tm), 0) == bucket
    ones = jnp.where(hit, 1.0, 0.0)
    before = _dot(ones.astype(BF16), tri_ref[...]) + base_sc[...]
    rank = jnp.sum(jnp.where(hit, before, 0.0), axis=0, keepdims=True)
    base_sc[...] = base_sc[...] + jnp.sum(ones, axis=1, keepdims=True)
    cnt_ref[...] = jnp.broadcast_to(base_sc[...], cnt_ref.shape)

    zero = jnp.zeros((1, tm), F32)
    o_ref[...] = jnp.concatenate([bucket.astype(F32), rank, gate_a, gate_b, zero, zero, zero, zero], axis=0)


def _router(x, whi, wlo, br):
    m_rows, d = x.shape
    tm = _token_tile(m_rows)
    tri = jnp.triu(jnp.ones((tm, tm), BF16), k=1)
    return pl.pallas_call(
        _router_kernel, grid=(m_rows // tm,),
        in_specs=[pl.BlockSpec((tm, d), lambda i: (i, 0)), _full(whi.shape), _full(wlo.shape), _full(br.shape),
                  _full(tri.shape)],
        out_specs=(pl.BlockSpec((SUBLANES, tm), lambda i: (0, i)), _full((N_BUCKETS, LANES))),
        out_shape=(jax.ShapeDtypeStruct((SUBLANES, m_rows), F32), jax.ShapeDtypeStruct((N_BUCKETS, LANES), F32)),
        scratch_shapes=[pltpu.VMEM((N_BUCKETS, 1), F32)],
        compiler_params=_params(("arbitrary",)), name="router",
    )(x, whi, wlo, br, tri)


def _row_copy(src_ref, src_row, dst_ref, dst_row, sem):
    return pltpu.make_async_copy(src_ref.at[pl.ds(src_row, 1), :], dst_ref.at[pl.ds(dst_row, 1), :], sem)


HI_HALF = 0xFFFF0000


def _pack_bf16_pairs(x):
    n = x.shape[1] // 2
    lo = pltpu.bitcast(x[:, :n].astype(BF16).astype(F32), jnp.uint32)
    hi = pltpu.bitcast(x[:, n:].astype(BF16).astype(F32), jnp.uint32)
    return (lo >> 16) | (hi & jnp.uint32(HI_HALF))


def _unpack_bf16_pairs(p):
    lo = pltpu.bitcast(p << 16, F32).astype(BF16)
    hi = pltpu.bitcast(p & jnp.uint32(HI_HALF), F32).astype(BF16)
    return jnp.concatenate([lo, hi], axis=1)


def _dispatch_kernel(dest_ref, x_ref, gate_ref, buf_in_ref, buf_ref, xp_sc, sem):
    del buf_in_ref
    tm, d = x_ref.shape
    step = pl.program_id(0)
    slot = step % 2

    def wait_slot(sl):
        pltpu.make_async_copy(xp_sc.at[sl], buf_ref.at[pl.ds(0, tm), :], sem.at[sl]).wait()

    @pl.when(step >= 2)
    def _():
        wait_slot(slot)

    stage = xp_sc.at[slot]
    stage[:, 0:d // 2] = _pack_bf16_pairs(x_ref[...])
    gate = gate_ref[...]
    lane = lax.broadcasted_iota(jnp.int32, (tm, LANES), 1)
    stage[:, d // 2:] = pltpu.bitcast(jnp.where(lane == 0, gate[:, 0:1], jnp.where(lane == 1, gate[:, 1:2], 0.0)),
                                      jnp.uint32)

    for r in range(tm):
        _row_copy(stage, r, buf_ref, dest_ref[0, 0, r], sem.at[slot]).start(priority=r % 2)

    @pl.when(step == pl.num_programs(0) - 1)
    def _():
        wait_slot(slot)

        @pl.when(step >= 1)
        def _():
            wait_slot(1 - slot)


def _dispatch(x, gates, dest_tiles, n_slots):
    m_rows, d = x.shape
    tm = dest_tiles.shape[2]
    width = d // 2 + LANES
    buf0 = jnp.zeros((n_slots, width), jnp.uint32)
    return pl.pallas_call(
        _dispatch_kernel, grid=(m_rows // tm,),
        in_specs=[pl.BlockSpec((1, 1, tm), lambda i: (i, 0, 0), memory_space=pltpu.SMEM),
                  pl.BlockSpec((tm, d), lambda i: (i, 0)),
                  pl.BlockSpec((tm, 2), lambda i: (i, 0)),
                  pl.BlockSpec(memory_space=pl.ANY)],
        out_specs=pl.BlockSpec(memory_space=pl.ANY),
        out_shape=jax.ShapeDtypeStruct((n_slots, width), jnp.uint32),
        scratch_shapes=[pltpu.VMEM((2, tm, width), jnp.uint32), pltpu.SemaphoreType.DMA((2,))],
        input_output_aliases={3: 0},
        compiler_params=_params(("arbitrary",)), name="moe_dispatch",
    )(dest_tiles, x, gates, buf0)


def _expert_kernel(tea_ref, teb_ref, tv_ref, x_ref, win_a_ref, wout_a_ref, win_b_ref, wout_b_ref, o_ref):
    del tea_ref, teb_ref
    i = pl.program_id(0)
    de = wout_a_ref.shape[0]
    half = win_a_ref.shape[0] // 2

    @pl.when(tv_ref[i] != 0)
    def _():
        xb = _unpack_bf16_pairs(x_ref[:, 0:half])
        gate = pltpu.bitcast(x_ref[:, half:], F32)

        def expert(win_ref, wout_ref):
            h = _dot(xb, win_ref[...])
            hg = h[:, 0:de]
            act = hg * _sigmoid(hg) * h[:, de:2 * de]
            return _dot(act.astype(BF16), wout_ref[...])

        o_ref[...] = gate[:, 0:1] * expert(win_a_ref, wout_a_ref) + gate[:, 1:2] * expert(win_b_ref, wout_b_ref)

    @pl.when(tv_ref[i] == 0)
    def _():
        o_ref[...] = jnp.zeros(o_ref.shape, F32)


def _experts(buf, tile_ea, tile_eb, tile_valid, w_in, w_out, layer, tile_rows):
    n_slots, width = buf.shape
    d, de2 = w_in.shape[2], w_in.shape[3]
    grid_spec = pltpu.PrefetchScalarGridSpec(
        num_scalar_prefetch=3, grid=(n_slots // tile_rows,),
        in_specs=[pl.BlockSpec((tile_rows, width), lambda i, ea, eb, tv: (i, 0)),
                  pl.BlockSpec((None, None, d, de2), lambda i, ea, eb, tv: (layer, ea[i], 0, 0)),
                  pl.BlockSpec((None, None, de2 // 2, d), lambda i, ea, eb, tv: (layer, ea[i], 0, 0)),
                  pl.BlockSpec((None, None, d, de2), lambda i, ea, eb, tv: (layer, eb[i], 0, 0)),
                  pl.BlockSpec((None, None, de2 // 2, d), lambda i, ea, eb, tv: (layer, eb[i], 0, 0))],
        out_specs=pl.BlockSpec((tile_rows, d), lambda i, ea, eb, tv: (i, 0)))
    return pl.pallas_call(
        _expert_kernel, grid_spec=grid_spec, out_shape=jax.ShapeDtypeStruct((n_slots, d), F32),
        compiler_params=_params(("arbitrary",)), name="moe_experts",
    )(tile_ea, tile_eb, tile_valid, buf, w_in, w_out, w_in, w_out)


def _combine_kernel(dest_ref, dest_next_ref, x_ref, g_ref, b_ref, y_ref, o_ref, y_sc, sem):
    tm = x_ref.shape[0]
    i = pl.program_id(0)
    slot = i % 2

    def gather(idx_ref, sl):
        for r in range(tm):
            _row_copy(y_ref, idx_ref[0, 0, r], y_sc.at[sl], r, sem.at[sl]).start(priority=r % 2)

    @pl.when(i == 0)
    def _():
        gather(dest_ref, 0)

    @pl.when(i + 1 < pl.num_programs(0))
    def _():
        gather(dest_next_ref, 1 - slot)

    pltpu.make_async_copy(y_ref.at[pl.ds(0, tm), :], y_sc.at[slot], sem.at[slot]).wait()
    o_ref[...] = _layer_norm(ALPHA * x_ref[...] + y_sc[slot], g_ref[...], b_ref[...])


def _combine(x, dest_tiles, y, g, b):
    m_rows, d = x.shape
    tm = dest_tiles.shape[2]
    last = m_rows // tm - 1
    return pl.pallas_call(
        _combine_kernel, grid=(m_rows // tm,),
        in_specs=[pl.BlockSpec((1, 1, tm), lambda i: (i, 0, 0), memory_space=pltpu.SMEM),
                  pl.BlockSpec((1, 1, tm), lambda i: (jnp.minimum(i + 1, last), 0, 0), memory_space=pltpu.SMEM),
                  pl.BlockSpec((tm, d), lambda i: (i, 0)),
                  _full(g.shape), _full(b.shape),
                  pl.BlockSpec(memory_space=pl.ANY)],
        out_specs=pl.BlockSpec((tm, d), lambda i: (i, 0)),
        out_shape=jax.ShapeDtypeStruct((m_rows, d), F32),
        scratch_shapes=[pltpu.VMEM((2, tm, d), F32), pltpu.SemaphoreType.DMA((2,))],
        compiler_params=_params(("arbitrary",)), name="moe_combine",
    )(dest_tiles, dest_tiles, x, g, b, y)


def _moe_layer(x, rw, w_in, w_out, layer, g, b, expert_tile):
    m_rows, _ = x.shape
    tm = min(m_rows, MOE_ROW_TILE)
    assert m_rows % tm == 0
    route, counts = _router(x, rw["whi"], rw["wlo"], rw["br"])
    bucket = route[0].astype(jnp.int32)
    rank = route[1].astype(jnp.int32)
    gates = route[2:4].T
    counts = counts[:, 0].astype(jnp.int32)
    padded = (counts + expert_tile - 1) // expert_tile * expert_tile
    pad_ends = jnp.cumsum(padded)
    pad_starts = pad_ends - padded
    b_range = jnp.arange(N_BUCKETS, dtype=jnp.int32)
    dest = jnp.sum(jnp.where(bucket[:, None] == b_range, pad_starts, 0), axis=-1) + rank
    n_tiles = -(-(m_rows + N_BUCKETS * (expert_tile - 1)) // expert_tile)
    tile_start = jnp.arange(n_tiles, dtype=jnp.int32) * expert_tile
    tile_bucket = jnp.minimum(jnp.sum((tile_start[:, None] >= pad_ends[None, :]).astype(jnp.int32), axis=1),
                              N_BUCKETS - 1)
    tile_valid = (tile_start < pad_ends[-1]).astype(jnp.int32)
    pair = tile_bucket % PAIRS_PER_GROUP
    group_base = tile_bucket // PAIRS_PER_GROUP * EXPERTS_PER_GROUP
    pair_range = jnp.arange(PAIRS_PER_GROUP, dtype=jnp.int32)
    tile_ea = group_base + jnp.sum(jnp.where(pair[:, None] == pair_range, jnp.asarray(PAIR_FIRST, jnp.int32), 0), axis=1)
    tile_eb = group_base + jnp.sum(jnp.where(pair[:, None] == pair_range, jnp.asarray(PAIR_SECOND, jnp.int32), 0), axis=1)
    dest_tiles = dest.reshape(m_rows // tm, 1, tm)
    buf = _dispatch(x, gates, dest_tiles, n_tiles * expert_tile)
    y = _experts(buf, tile_ea, tile_eb, tile_valid, w_in, w_out, layer, expert_tile)
    return _combine(x, dest_tiles, y, g, b)


def _block_diag(w):
    h, n, _ = w.shape
    eye = jnp.eye(h, dtype=w.dtype)
    return (eye[:, None, :, None] * w[:, :, None, :]).reshape(h * n, h * n)


def _pair_blocks(w):
    h, a, b = w.shape
    w = w.reshape(h // 2, 2, a, b)
    eye = jnp.eye(2, dtype=w.dtype)
    return (eye[None, :, None, :, None] * w[:, :, :, None, :]).reshape(h // 2, 2 * a, 2 * b)


def _time_major(a):
    b, t, c = a.shape
    return a.transpose(1, 0, 2).reshape(t * b, c)


def _batch_major(a, nb):
    return a.reshape(-1, nb, a.shape[-1]).transpose(1, 0, 2)


def _rope_tables(pos):
    half = QK_ROPE // 2
    inv = ROPE_THETA ** (-jnp.arange(half, dtype=F32) / half)
    ang = pos.astype(F32)[:, None] * inv
    cos, sin = jnp.cos(ang), jnp.sin(ang)
    cos_t = jnp.tile(jnp.concatenate([cos, cos], axis=1), (1, N_HEADS))
    sin_t = jnp.tile(jnp.concatenate([-sin, sin], axis=1), (1, N_HEADS))
    return cos_t, sin_t


def kernel(x_prompt, x_sample, state_conv_a, state_conv_b, state_rglru_h, state_conv_c, cache_ckv, cache_kpe, page_table, w_in_even, conv_a_w, conv_b_w, conv_b_b, lru_wa, lru_ba, lru_wx, lru_bx, lru_lambda, w_out_even, w_in_odd, conv_c_w, conv_c_b, ln_c_g, ln_c_b, q_norm_g, w_q_b, kv_norm_g, w_uk, w_uv, w_out_odd, ln_mix_g, ln_mix_b, ln_ffn_g, ln_ffn_b, w_router, b_router, w_exp_in, w_exp_out):
    bp, n_tp, d = x_prompt.shape
    bs, n_ts, _ = x_sample.shape
    past_len = page_table.shape[1] * PAGE_SIZE
    dc = conv_c_w.shape[2]
    q_rank = q_norm_g.shape[1]
    kv_rank = kv_norm_g.shape[1]

    xp = _time_major(x_prompt)
    xs = _time_major(x_sample)
    row2 = lambda v: v.reshape(1, -1)

    wr = jnp.pad(w_router, ((0, 0), (0, LANES - N_EXPERTS)))
    wr_hi = wr.astype(BF16)
    router_w = {"whi": wr_hi, "wlo": (wr - wr_hi.astype(F32)).astype(BF16), "br": b_router.reshape(N_EXPERTS, 1)}
    rope_p = _rope_tables(jnp.arange(n_tp))
    rope_s = _rope_tables(past_len + jnp.arange(n_ts))
    we_in, we_out = w_exp_in.astype(BF16), w_exp_out.astype(BF16)
    cache_kpe_t = jnp.swapaxes(cache_kpe, 2, 3)

    outs = {k: [] for k in ("ca_p", "ca_s", "cb_p", "cb_s", "h_p", "h_s", "cc_p", "cc_s", "ckv_p", "ckv_s", "kpe_p", "kpe_s")}
    for l in range(DEPTH):
        j = l // 2
        lg, lb = row2(ln_mix_g[l]), row2(ln_mix_b[l])
        if l % 2 == 0:
            wa_bd, wx_bd = _block_diag(lru_wa[j]), _block_diag(lru_wx[j])
            hc = wa_bd.shape[0] // 2
            wg = jnp.stack([jnp.concatenate([wa_bd[s * hc:(s + 1) * hc, s * hc:(s + 1) * hc],
                                             wx_bd[s * hc:(s + 1) * hc, s * hc:(s + 1) * hc]], axis=1)
                            for s in range(2)]).astype(BF16)
            w = {"win": w_in_even[j].astype(BF16), "caw": conv_a_w[j], "cbw": conv_b_w[j], "cbb": row2(conv_b_b[j]),
                 "wg": wg, "ba": row2(lru_ba[j]), "bx": row2(lru_bx[j]), "lam": row2(lru_lambda[j]),
                 "wout": w_out_even[j].astype(BF16), "g": lg, "b": lb}
            da = conv_a_w.shape[2]
            db = conv_b_w.shape[2]
            xp, a1, b1, h1 = _even_layer(xp, w, jnp.zeros((2 * bp, da), F32), jnp.zeros((3 * bp, db), F32),
                                         jnp.zeros((bp, db), F32), nb=bp, n_t=n_tp)
            xs, a2, b2, h2 = _even_layer(xs, w, _time_major(state_conv_a[j]), _time_major(state_conv_b[j]),
                                         state_rglru_h[j], nb=bs, n_t=n_ts)
            outs["ca_p"].append(_batch_major(a1, bp)); outs["ca_s"].append(_batch_major(a2, bs))
            outs["cb_p"].append(_batch_major(b1, bp)); outs["cb_s"].append(_batch_major(b2, bs))
            outs["h_p"].append(h1); outs["h_s"].append(h2)
        else:
            nope_all = N_HEADS * QK_NOPE
            wq = w_q_b[j].reshape(q_rank, N_HEADS, QK_NOPE + QK_ROPE)
            wqb = jnp.concatenate([wq[:, :, :QK_NOPE].reshape(q_rank, nope_all),
                                   wq[:, :, QK_NOPE:].reshape(q_rank, N_HEADS * QK_ROPE)], axis=1).astype(BF16)
            o_pe = 2 * dc + q_rank + kv_rank
            win = jnp.concatenate([w_in_odd[j][:, :o_pe], jnp.tile(w_in_odd[j][:, o_pe:], (1, N_HEADS))], axis=1).astype(BF16)
            wuk = _pair_blocks(w_uk[j].transpose(1, 2, 0)).astype(BF16)
            wuv = _pair_blocks(w_uv[j].transpose(1, 0, 2)).astype(BF16)
            w = {"win": win, "ccw": conv_c_w[j], "ccb": row2(conv_c_b[j]), "lcg": row2(ln_c_g[j]), "lcb": row2(ln_c_b[j]),
                 "qg": row2(q_norm_g[j]), "wqb": wqb, "kvg": row2(kv_norm_g[j]), "wuk": wuk}
            width = conv_c_w.shape[1]
            wout = w_out_odd[j].astype(BF16)
            ycp, qp, kp, ckv1, kpe1, c1 = _odd_pre(xp, w, jnp.zeros(((width - 1) * bp, dc), F32), *rope_p, nb=bp, n_t=n_tp)
            ycs, qs, ks, ckv2, kpe2, c2 = _odd_pre(xs, w, _time_major(state_conv_c[j]), *rope_s, nb=bs, n_t=n_ts)
            ydp = _prompt_attention(qp, kp, wuv)
            yds = _sample_attention(qs, ks, wuv, cache_ckv, cache_kpe_t, page_table, j)
            xp = _odd_post(xp, ycp, ydp, wout, lg, lb, nb=bp, n_t=n_tp)
            xs = _odd_post(xs, ycs, yds, wout, lg, lb, nb=bs, n_t=n_ts)
            outs["cc_p"].append(_batch_major(c1, bp)); outs["cc_s"].append(_batch_major(c2, bs))
            outs["ckv_p"].append(ckv1); outs["ckv_s"].append(ckv2)
            outs["kpe_p"].append(kpe1); outs["kpe_s"].append(kpe2)
        fg, fb = row2(ln_ffn_g[l]), row2(ln_ffn_b[l])
        xp = _moe_layer(xp, router_w, we_in, we_out, l, fg, fb, expert_tile=512)
        xs = _moe_layer(xs, router_w, we_in, we_out, l, fg, fb, expert_tile=64)

    y_prompt = _batch_major(xp, bp)
    y_sample = _batch_major(xs, bs)
    st = lambda k: jnp.stack(outs[k])
    st1 = lambda k: jnp.stack(outs[k], axis=1)
    return (y_prompt, y_sample, st("ca_p"), st("ca_s"), st("cb_p"), st("cb_s"), st("h_p"), st("h_s"),
            st("cc_p"), st("cc_s"), st1("ckv_p"), st1("ckv_s"), st1("kpe_p"), st1("kpe_s"))
```

```python
import functools

import jax
import jax.numpy as jnp
from jax import lax
from jax.experimental import pallas as pl
from jax.experimental.pallas import tpu as pltpu

F32 = jnp.float32
BF16 = jnp.bfloat16

DEPTH = 4
N_HEADS = 8
QK_NOPE = 64
QK_ROPE = 32
V_DIM = 64
N_EXPERTS = 16
N_GROUPS = 4
EXPERTS_PER_GROUP = 4
PAIRS_PER_GROUP = 6
N_BUCKETS = N_GROUPS * PAIRS_PER_GROUP
PAIR_FIRST = (0, 0, 0, 1, 1, 2)
PAIR_SECOND = (1, 2, 3, 2, 3, 3)
LRU_C = 8.0
ROPE_THETA = 10000.0
ATTN_SCALE = (QK_NOPE + QK_ROPE) ** -0.5
ALPHA = (2 * DEPTH) ** 0.25
PAGE_SIZE = 128

SUBLANES = 8
BF16_SUBLANES = 16
LANES = 128
VMEM_LIMIT_BYTES = 56 * 1024 * 1024

TIME_STEPS_PER_TILE = 32
TOKEN_TILE = 512
MOE_ROW_TILE = 1024
ATTN_TILE = 512
PAGES_PER_STEP = 64
NEG_INF = float("-inf")


def _params(semantics):
    return pltpu.CompilerParams(dimension_semantics=semantics, vmem_limit_bytes=VMEM_LIMIT_BYTES)


def _full(shape):
    nd = len(shape)
    return pl.BlockSpec(shape, lambda *_: (0,) * nd)


def _layer_norm(x, g, b, eps=1e-5):
    mu = jnp.mean(x, axis=-1, keepdims=True)
    xc = x - mu
    var = jnp.mean(xc * xc, axis=-1, keepdims=True)
    return xc * lax.rsqrt(var + eps) * g + b


def _rms_norm(x, g, eps=1e-6):
    return x * lax.rsqrt(jnp.mean(x * x, axis=-1, keepdims=True) + eps) * g


def _sigmoid(x):
    return 1.0 / (1.0 + jnp.exp(-x))


def _dot(a, b):
    return jnp.dot(a, b, preferred_element_type=F32)


def _dot_nt(a, b):
    return lax.dot_general(a, b, (((1,), (1,)), ((), ())), preferred_element_type=F32)


def _seq_tile(nb, n_t):
    tt = min(n_t, TIME_STEPS_PER_TILE)
    assert n_t % tt == 0 and nb % SUBLANES == 0
    return tt


def _even_kernel(x_ref, win_ref, caw_ref, cbw_ref, cbb_ref, wg_ref, ba_ref, bx_ref, lam_ref, wout_ref,
                 g_ref, b_ref, sa_ref, sb_ref, h0_ref,
                 o_ref, na_ref, nb_ref, hl_ref,
                 ua_ext, vb_ext, h_sc, hs_sc, *, nb, tt):
    tm = tt * nb
    dh = caw_ref.shape[1]
    half = dh // 2

    @pl.when(pl.program_id(0) == 0)
    def _():
        ua_ext[0:2 * nb, :] = sa_ref[...]
        vb_ext[0:3 * nb, :] = sb_ref[...]
        h_sc[...] = h0_ref[...]

    x = x_ref[...]
    xb = x.astype(BF16)

    def proj(j):
        return _dot(xb, win_ref[:, j * dh:(j + 1) * dh])

    ua_ext[2 * nb:2 * nb + tm, :] = proj(1) * proj(2)
    caw = caw_ref[...]
    conv_a = (caw[0:1] * ua_ext[0:tm, :] + caw[1:2] * ua_ext[nb:nb + tm, :]
              + caw[2:3] * ua_ext[2 * nb:2 * nb + tm, :])
    y_a = proj(0) * conv_a
    tail_a = ua_ext[tm:tm + 2 * nb, :]
    na_ref[...] = tail_a
    ua_ext[0:2 * nb, :] = tail_a

    vb_ext[3 * nb:3 * nb + tm, :] = proj(4)
    cbw = cbw_ref[...]
    u_b = (cbb_ref[...] + cbw[0:1] * vb_ext[0:tm, :] + cbw[1:2] * vb_ext[nb:nb + tm, :]
           + cbw[2:3] * vb_ext[2 * nb:2 * nb + tm, :] + cbw[3:4] * vb_ext[3 * nb:3 * nb + tm, :])
    tail_b = vb_ext[tm:tm + 3 * nb, :]
    nb_ref[...] = tail_b
    vb_ext[0:3 * nb, :] = tail_b

    ub16 = u_b.astype(BF16)
    gk0 = _dot(ub16[:, :half], wg_ref[0])
    gk1 = _dot(ub16[:, half:], wg_ref[1])
    r = _sigmoid(jnp.concatenate([gk0[:, :half], gk1[:, :half]], axis=1) + ba_ref[...])
    ig = _sigmoid(jnp.concatenate([gk0[:, half:], gk1[:, half:]], axis=1) + bx_ref[...])
    nlam = -lam_ref[...]
    softplus = jnp.maximum(nlam, 0.0) + jnp.log(1.0 + jnp.exp(-jnp.abs(nlam)))
    log_a = (-LRU_C * r) * softplus
    a = jnp.exp(log_a)
    bterm = jnp.sqrt(1.0 - a * a) * (ig * u_b)

    h = h_sc[...]
    for t in range(tt):
        h = a[t * nb:(t + 1) * nb, :] * h + bterm[t * nb:(t + 1) * nb, :]
        hs_sc[t * nb:(t + 1) * nb, :] = h
    h_sc[...] = h
    hl_ref[...] = h
    y_b = jax.nn.gelu(proj(3), approximate=True) * hs_sc[...]

    m = _dot(y_a.astype(BF16), wout_ref[0:dh, :]) + _dot(y_b.astype(BF16), wout_ref[dh:2 * dh, :])
    o_ref[...] = _layer_norm(ALPHA * x + m, g_ref[...], b_ref[...])


def _even_layer(x, w, sa, sb, h0, *, nb, n_t):
    m_rows, d = x.shape
    dh = w["caw"].shape[1]
    tt = _seq_tile(nb, n_t)
    tm = tt * nb
    kern = functools.partial(_even_kernel, nb=nb, tt=tt)
    row = lambda i: (i, 0)
    in_specs = [pl.BlockSpec((tm, d), row), _full(w["win"].shape), _full(w["caw"].shape), _full(w["cbw"].shape),
                _full(w["cbb"].shape), _full(w["wg"].shape), _full(w["ba"].shape), _full(w["bx"].shape),
                _full(w["lam"].shape), _full(w["wout"].shape), _full(w["g"].shape), _full(w["b"].shape),
                _full(sa.shape), _full(sb.shape), _full(h0.shape)]
    out_shape = (jax.ShapeDtypeStruct((m_rows, d), F32), jax.ShapeDtypeStruct(sa.shape, F32),
                 jax.ShapeDtypeStruct(sb.shape, F32), jax.ShapeDtypeStruct(h0.shape, F32))
    out_specs = (pl.BlockSpec((tm, d), row), _full(sa.shape), _full(sb.shape), _full(h0.shape))
    scratch = [pltpu.VMEM((tm + 2 * nb, dh), F32), pltpu.VMEM((tm + 3 * nb, dh), F32),
               pltpu.VMEM((nb, dh), F32), pltpu.VMEM((tm, dh), F32)]
    return pl.pallas_call(
        kern, grid=(m_rows // tm,), in_specs=in_specs, out_specs=out_specs, out_shape=out_shape,
        scratch_shapes=scratch, compiler_params=_params(("arbitrary",)), name="even_mixer",
    )(x, w["win"], w["caw"], w["cbw"], w["cbb"], w["wg"], w["ba"], w["bx"], w["lam"], w["wout"], w["g"], w["b"],
      sa, sb, h0)


def _odd_pre_kernel(x_ref, win_ref, ccw_ref, ccb_ref, lcg_ref, lcb_ref, qg_ref, wqb_ref, kvg_ref, wuk_ref,
                    cos_ref, sin_ref, sc_ref, perm_ref,
                    yc_ref, q_ref, k_ref, ckv_ref, kpe_ref, nc_ref,
                    c_ext, rl_sc, *, nb, tt, width):
    tm = tt * nb
    dc = ccw_ref.shape[1]
    q_rank = qg_ref.shape[1]
    kv_rank = kvg_ref.shape[1]
    pe_all = N_HEADS * QK_ROPE
    hist = (width - 1) * nb

    @pl.when(pl.program_id(0) == 0)
    def _():
        c_ext[0:hist, :] = sc_ref[...]

    xb = x_ref[...].astype(BF16)
    o_q = 2 * dc
    o_kv = o_q + q_rank
    o_pe = o_kv + kv_rank

    glu = _dot(xb, win_ref[:, 0:dc]) * _sigmoid(_dot(xb, win_ref[:, dc:2 * dc]))
    c_ext[hist:hist + tm, :] = glu
    ccw = ccw_ref[...]
    u_c = ccb_ref[...] + ccw[0:1] * c_ext[0:tm, :]
    for k in range(1, width):
        u_c = u_c + ccw[k:k + 1] * c_ext[k * nb:k * nb + tm, :]
    tail = c_ext[tm:tm + hist, :]
    nc_ref[...] = tail
    c_ext[0:hist, :] = tail
    ln = _layer_norm(u_c, lcg_ref[...], lcb_ref[...])
    yc_ref[...] = ln * _sigmoid(ln)

    def to_sequences(val, store):
        n_lane_tiles = val.shape[1] // LANES
        for c in range(n_lane_tiles):
            rl_sc[c] = val[:, c * LANES:(c + 1) * LANES]

        def body(b, carry):
            store(b, jnp.concatenate([rl_sc[c, pl.ds(b, tt, stride=nb), :] for c in range(n_lane_tiles)], axis=1))
            return carry

        lax.fori_loop(0, nb, body, 0)

    lane = lax.broadcasted_iota(jnp.int32, (1, pe_all), 1)
    first_half = (lane % QK_ROPE) < (QK_ROPE // 2)
    head_of_lane = lane // QK_ROPE

    permute_on_mxu = tt % BF16_SUBLANES == 0
    if permute_on_mxu:
        src = _dot(perm_ref[...], xb).astype(BF16)
        cos = jnp.tile(cos_ref[...], (nb, 1))
        sin = jnp.tile(sin_ref[...], (nb, 1))
    else:
        src = xb
        cos = jnp.broadcast_to(cos_ref[...][:, None, :], (tt, nb, pe_all)).reshape(tm, pe_all)
        sin = jnp.broadcast_to(sin_ref[...][:, None, :], (tt, nb, pe_all)).reshape(tm, pe_all)

    def rope(v):
        swapped = jnp.where(first_half, pltpu.roll(v, pe_all - QK_ROPE // 2, 1), pltpu.roll(v, QK_ROPE // 2, 1))
        return v * cos + swapped * sin

    def emit(val, store):
        if permute_on_mxu:
            for b in range(nb):
                store(b, val[b * tt:(b + 1) * tt])
        else:
            to_sequences(val, store)

    ckv = _rms_norm(_dot(src, win_ref[:, o_kv:o_kv + kv_rank]), kvg_ref[...])
    kpe = rope(_dot(src, win_ref[:, o_pe:o_pe + pe_all]))

    def store_k(b, v):
        k_ref[b] = v.astype(BF16)
        ckv_ref[b] = v[:, 0:kv_rank]
        kpe_ref[b] = v[:, kv_rank:kv_rank + QK_ROPE]

    emit(jnp.concatenate([ckv, kpe], axis=1), store_k)

    qn = _rms_norm(_dot(src, win_ref[:, o_q:o_q + q_rank]), qg_ref[...]).astype(BF16)
    nope_all = N_HEADS * QK_NOPE
    q_nope = (_dot(qn, wqb_ref[:, 0:nope_all]) * ATTN_SCALE).astype(BF16)
    q_pe = rope(_dot(qn, wqb_ref[:, nope_all:nope_all + pe_all]) * ATTN_SCALE)
    for p in range(N_HEADS // 2):
        lat2 = _dot(q_nope[:, p * 2 * QK_NOPE:(p + 1) * 2 * QK_NOPE], wuk_ref[p])
        for s in range(2):
            h = 2 * p + s

            def store_q(b, v, h=h):
                q_ref[b, h] = v.astype(BF16)

            emit(jnp.concatenate([lat2[:, s * kv_rank:(s + 1) * kv_rank],
                                  jnp.where(head_of_lane == h, q_pe, 0.0)], axis=1), store_q)


def _odd_pre(x, w, sc, cos, sin, *, nb, n_t):
    m_rows, d = x.shape
    width, dc = w["ccw"].shape
    kv_rank = w["kvg"].shape[1]
    pe_all = N_HEADS * QK_ROPE
    kq = kv_rank + pe_all
    tt = _seq_tile(nb, n_t)
    tm = tt * nb
    kern = functools.partial(_odd_pre_kernel, nb=nb, tt=tt, width=width)
    row = lambda i: (i, 0)
    seq = lambda i: (0, i, 0)
    in_specs = [pl.BlockSpec((tm, d), row)] + [_full(w[k].shape) for k in
                                                ("win", "ccw", "ccb", "lcg", "lcb", "qg", "wqb", "kvg", "wuk")]
    out_row = jnp.arange(tm, dtype=jnp.int32)
    src_row = (out_row % tt) * nb + out_row // tt
    perm = (src_row[:, None] == out_row[None, :]).astype(BF16)
    in_specs += [pl.BlockSpec((tt, pe_all), row), pl.BlockSpec((tt, pe_all), row), _full(sc.shape), _full(perm.shape)]
    out_shape = (jax.ShapeDtypeStruct((m_rows, dc), F32),
                 jax.ShapeDtypeStruct((nb, N_HEADS, n_t, kq), BF16),
                 jax.ShapeDtypeStruct((nb, n_t, kq), BF16),
                 jax.ShapeDtypeStruct((nb, n_t, kv_rank), F32),
                 jax.ShapeDtypeStruct((nb, n_t, QK_ROPE), F32),
                 jax.ShapeDtypeStruct(sc.shape, F32))
    out_specs = (pl.BlockSpec((tm, dc), row), pl.BlockSpec((nb, N_HEADS, tt, kq), lambda i: (0, 0, i, 0)),
                 pl.BlockSpec((nb, tt, kq), seq), pl.BlockSpec((nb, tt, kv_rank), seq),
                 pl.BlockSpec((nb, tt, QK_ROPE), seq), _full(sc.shape))
    scratch = [pltpu.VMEM((tm + (width - 1) * nb, dc), F32), pltpu.VMEM((kq // LANES, tm, LANES), F32)]
    return pl.pallas_call(
        kern, grid=(m_rows // tm,), in_specs=in_specs, out_specs=out_specs, out_shape=out_shape,
        scratch_shapes=scratch, compiler_params=_params(("arbitrary",)), name="odd_pre",
    )(x, w["win"], w["ccw"], w["ccb"], w["lcg"], w["lcb"], w["qg"], w["wqb"], w["kvg"], w["wuk"], cos, sin, sc, perm)


def _softmax_step(s, v16, m_sc, l_sc, acc_sc):
    m_prev = m_sc[...]
    m_new = jnp.maximum(m_prev, jnp.max(s, axis=-1, keepdims=True))
    alpha = jnp.exp(m_prev - m_new)
    p = jnp.exp(s - m_new)
    l_sc[...] = alpha * l_sc[...] + jnp.sum(p, axis=-1, keepdims=True)
    acc_sc[...] = alpha * acc_sc[...] + _dot(p.astype(BF16), v16)
    m_sc[...] = m_new


def _softmax_init(m_sc, l_sc, acc_sc):
    m_sc[...] = jnp.full(m_sc.shape, NEG_INF, F32)
    l_sc[...] = jnp.zeros(l_sc.shape, F32)
    acc_sc[...] = jnp.zeros(acc_sc.shape, F32)


def _prompt_attn_kernel(qi_ref, ki_ref, q_ref, k_ref, bias_ref, wuv_ref, o_ref, *state, kv_rank):
    m_sc, l_sc, acc_sc = state[0:N_HEADS], state[N_HEADS:2 * N_HEADS], state[2 * N_HEADS:3 * N_HEADS]
    j = pl.program_id(1)
    qi = qi_ref[j]
    ki = ki_ref[j]

    @pl.when(ki == 0)
    def _():
        for h in range(N_HEADS):
            _softmax_init(m_sc[h], l_sc[h], acc_sc[h])

    k = k_ref[...]
    v_t = k[:, 0:kv_rank].T
    diag = (ki == qi).astype(jnp.int32)

    tile = k_ref.shape[0]
    width = tile // 2 if tile % (4 * LANES) == 0 else tile
    units = [(h, c0) for h in range(N_HEADS) for c0 in range(0, tile, width)]

    def scores_t(unit):
        h, c0 = unit
        return _dot_nt(k, q_ref[h, c0:c0 + width, :]) + bias_ref[diag, :, c0:c0 + width]

    s_next = scores_t(units[0])
    for u, (h, c0) in enumerate(units):
        s = s_next
        if u + 1 < len(units):
            s_next = scores_t(units[u + 1])
        cols = slice(c0, c0 + width)
        m_prev = m_sc[h][:, cols]
        m_new = jnp.maximum(m_prev, jnp.max(s, axis=0, keepdims=True))
        alpha = jnp.exp(m_prev - m_new)
        p = jnp.exp(s - m_new)
        l_sc[h][:, cols] = alpha * l_sc[h][:, cols] + jnp.sum(p, axis=0, keepdims=True)
        acc_sc[h][:, cols] = alpha * acc_sc[h][:, cols] + _dot(v_t, p.astype(BF16))
        m_sc[h][:, cols] = m_new

    @pl.when(ki == qi)
    def _():
        for p in range(N_HEADS // 2):
            o2_t = jnp.concatenate([acc_sc[2 * p][...] / l_sc[2 * p][...],
                                    acc_sc[2 * p + 1][...] / l_sc[2 * p + 1][...]], axis=0)
            o_ref[:, p * 2 * V_DIM:(p + 1) * 2 * V_DIM] = _dot(o2_t.T.astype(BF16), wuv_ref[p])


def _prompt_attention(q, k, wuv):
    nb, n_heads, n_t, kq = q.shape
    kv_rank = wuv.shape[1] // 2
    dv = N_HEADS * V_DIM
    tile = min(n_t, ATTN_TILE)
    assert n_t % tile == 0
    nq = n_t // tile
    pairs = [(a, b) for a in range(nq) for b in range(a + 1)]
    qi_tab = jnp.asarray([p[0] for p in pairs], jnp.int32)
    ki_tab = jnp.asarray([p[1] for p in pairs], jnp.int32)
    causal = jnp.where(jnp.arange(tile)[:, None] <= jnp.arange(tile)[None, :], 0.0, NEG_INF).astype(F32)
    bias = jnp.stack([jnp.zeros((tile, tile), F32), causal])
    kern = functools.partial(_prompt_attn_kernel, kv_rank=kv_rank)
    grid_spec = pltpu.PrefetchScalarGridSpec(
        num_scalar_prefetch=2, grid=(nb, len(pairs)),
        in_specs=[pl.BlockSpec((None, n_heads, tile, kq), lambda b, j, qi, ki: (b, 0, qi[j], 0)),
                  pl.BlockSpec((None, tile, kq), lambda b, j, qi, ki: (b, ki[j], 0)),
                  pl.BlockSpec(bias.shape, lambda b, j, qi, ki: (0, 0, 0)),
                  pl.BlockSpec(wuv.shape, lambda b, j, qi, ki: (0, 0, 0))],
        out_specs=pl.BlockSpec((None, tile, dv), lambda b, j, qi, ki: (b, qi[j], 0)),
        scratch_shapes=([pltpu.VMEM((1, tile), F32)] * (2 * n_heads) + [pltpu.VMEM((kv_rank, tile), F32)] * n_heads))
    return pl.pallas_call(
        kern, grid_spec=grid_spec, out_shape=jax.ShapeDtypeStruct((nb, n_t, dv), F32),
        compiler_params=_params(("arbitrary", "arbitrary")), name="prompt_attention",
    )(qi_tab, ki_tab, q, k, bias, wuv)


def _sample_attn_kernel(pt_ref, q_ref, kn_ref, tsel_ref, wuv_ref, ckv_hbm, kpe_hbm, o_ref,
                        ck_buf, kp_buf, sem, qc_sc, m_sc, l_sc, acc_sc, *,
                        n_pages_step, layer, kv_rank, n_new, rows_per_head):
    b = pl.program_id(0)
    g = pl.program_id(1)
    n_g = pl.num_programs(1)
    step = b * n_g + g
    n_steps = pl.num_programs(0) * n_g
    slot = step % 2

    def page_copies(seq, grp, sl):
        copies = []
        for i in range(n_pages_step):
            page = pt_ref[seq, grp * n_pages_step + i]
            copies.append(pltpu.make_async_copy(
                ckv_hbm.at[page, layer], ck_buf.at[sl, pl.ds(i * PAGE_SIZE, PAGE_SIZE), :], sem.at[sl]))
            copies.append(pltpu.make_async_copy(
                kpe_hbm.at[page, layer], kp_buf.at[sl, :, pl.ds(i * PAGE_SIZE, PAGE_SIZE)], sem.at[sl]))
        return copies

    @pl.when(step == 0)
    def _():
        for c in page_copies(0, 0, 0):
            c.start()

    @pl.when(step + 1 < n_steps)
    def _():
        nxt = step + 1
        for c in page_copies(nxt // n_g, nxt % n_g, 1 - slot):
            c.start()

    @pl.when(g == 0)
    def _():
        _softmax_init(m_sc, l_sc, acc_sc)
        qc_sc[...] = _dot(q_ref[:, kv_rank:], tsel_ref[...]).astype(BF16)

    for c in page_copies(b, g, slot):
        c.wait()

    q = q_ref[...]
    ck = ck_buf[slot].astype(BF16)
    kp_t = kp_buf[slot].astype(BF16)
    s = _dot_nt(q[:, 0:kv_rank], ck) + _dot(qc_sc[...], kp_t)
    _softmax_step(s, ck, m_sc, l_sc, acc_sc)

    @pl.when(g == pl.num_programs(1) - 1)
    def _():
        kn = kn_ref[...]
        s_new = _dot_nt(q, kn)
        t_row = lax.broadcasted_iota(jnp.int32, s_new.shape, 0) % rows_per_head
        t_col = lax.broadcasted_iota(jnp.int32, s_new.shape, 1)
        s_new = jnp.where((t_col <= t_row) & (t_col < n_new), s_new, NEG_INF)
        _softmax_step(s_new, kn[:, 0:kv_rank], m_sc, l_sc, acc_sc)
        o = acc_sc[...] / l_sc[...]
        rp = rows_per_head
        for p in range(N_HEADS // 2):
            o2 = jnp.concatenate([o[2 * p * rp:(2 * p + 1) * rp], o[(2 * p + 1) * rp:(2 * p + 2) * rp]], axis=1)
            o_ref[:, p * 2 * V_DIM:(p + 1) * 2 * V_DIM] = _dot(o2.astype(BF16), wuv_ref[p])


def _sample_attention(q, k, wuv, cache_ckv, cache_kpe_t, page_table, layer):
    nb, n_heads, n_t, kq = q.shape
    kv_rank = wuv.shape[1] // 2
    dv = N_HEADS * V_DIM
    n_pages = page_table.shape[1]
    gp = min(PAGES_PER_STEP, n_pages)
    assert n_pages % gp == 0 and n_t <= SUBLANES
    rp = SUBLANES
    qb = jnp.pad(q, ((0, 0), (0, 0), (0, rp - n_t), (0, 0))).reshape(nb, n_heads * rp, kq)
    kb = jnp.pad(k, ((0, 0), (0, rp - n_t), (0, 0)))
    tsel = jnp.tile(jnp.eye(QK_ROPE, dtype=BF16), (N_HEADS, 1))
    kern = functools.partial(_sample_attn_kernel, n_pages_step=gp, layer=layer, kv_rank=kv_rank, n_new=n_t,
                             rows_per_head=rp)
    in_specs = [pl.BlockSpec((None, n_heads * rp, kq), lambda b, g, pt: (b, 0, 0)),
                pl.BlockSpec((None, rp, kq), lambda b, g, pt: (b, 0, 0)),
                pl.BlockSpec(tsel.shape, lambda b, g, pt: (0, 0)),
                pl.BlockSpec(wuv.shape, lambda b, g, pt: (0, 0, 0)),
                pl.BlockSpec(memory_space=pl.ANY), pl.BlockSpec(memory_space=pl.ANY)]
    grid_spec = pltpu.PrefetchScalarGridSpec(
        num_scalar_prefetch=1, grid=(nb, n_pages // gp), in_specs=in_specs,
        out_specs=pl.BlockSpec((None, rp, dv), lambda b, g, pt: (b, 0, 0)),
        scratch_shapes=[pltpu.VMEM((2, gp * PAGE_SIZE, kv_rank), F32), pltpu.VMEM((2, QK_ROPE, gp * PAGE_SIZE), F32),
                        pltpu.SemaphoreType.DMA((2,)),
                        pltpu.VMEM((n_heads * rp, QK_ROPE), BF16), pltpu.VMEM((n_heads * rp, 1), F32),
                        pltpu.VMEM((n_heads * rp, 1), F32), pltpu.VMEM((n_heads * rp, kv_rank), F32)])
    out = pl.pallas_call(
        kern, grid_spec=grid_spec, out_shape=jax.ShapeDtypeStruct((nb, rp, dv), F32),
        compiler_params=_params(("arbitrary", "arbitrary")), name="sample_attention",
    )(page_table, qb, kb, tsel, wuv, cache_ckv, cache_kpe_t)
    return out[:, :n_t]


def _odd_post_kernel(x_ref, yc_ref, yd_ref, wout_ref, g_ref, b_ref, o_ref, rl_sc, *, nb, tt):
    dc = yc_ref.shape[1]

    n_lane_tiles = rl_sc.shape[0]

    def body(b, carry):
        v = yd_ref[b]
        for c in range(n_lane_tiles):
            rl_sc[c, pl.ds(b, tt, stride=nb), :] = v[:, c * LANES:(c + 1) * LANES]
        return carry

    lax.fori_loop(0, nb, body, 0)
    yd = jnp.concatenate([rl_sc[c] for c in range(n_lane_tiles)], axis=1)
    m = _dot(yc_ref[...].astype(BF16), wout_ref[0:dc, :]) + _dot(yd.astype(BF16), wout_ref[dc:, :])
    o_ref[...] = _layer_norm(ALPHA * x_ref[...] + m, g_ref[...], b_ref[...])


def _odd_post(x, yc, yd, wout, g, b, *, nb, n_t):
    m_rows, d = x.shape
    dv = yd.shape[2]
    tt = _seq_tile(nb, n_t)
    tm = tt * nb
    row = lambda i: (i, 0)
    kern = functools.partial(_odd_post_kernel, nb=nb, tt=tt)
    return pl.pallas_call(
        kern, grid=(m_rows // tm,),
        in_specs=[pl.BlockSpec((tm, d), row), pl.BlockSpec((tm, yc.shape[1]), row),
                  pl.BlockSpec((nb, tt, dv), lambda i: (0, i, 0)),
                  _full(wout.shape), _full(g.shape), _full(b.shape)],
        out_specs=pl.BlockSpec((tm, d), row), out_shape=jax.ShapeDtypeStruct((m_rows, d), F32),
        scratch_shapes=[pltpu.VMEM((dv // LANES, tm, LANES), F32)],
        compiler_params=_params(("arbitrary",)), name="odd_post",
    )(x, yc, yd, wout, g, b)


def _token_tile(m_rows):
    tm = min(m_rows, TOKEN_TILE)
    assert m_rows % tm == 0
    return tm


def _router_kernel(x_ref, whi_ref, wlo_ref, br_ref, tri_ref, o_ref, cnt_ref, base_sc):
    tm = x_ref.shape[0]

    @pl.when(pl.program_id(0) == 0)
    def _():
        base_sc[...] = jnp.zeros(base_sc.shape, F32)

    x = x_ref[...]
    x_hi = x.astype(BF16)
    x_lo = (x - x_hi.astype(F32)).astype(BF16)
    logits = _dot(x_hi, whi_ref[...]) + (_dot(x_lo, whi_ref[...]) + _dot(x_hi, wlo_ref[...]))
    scores = _sigmoid(logits.T[0:N_EXPERTS, :])
    sel = scores + br_ref[...]

    def row(a, e):
        return a[e:e + 1, :]

    best_gs = None
    grp = None
    for gidx in range(N_GROUPS):
        v = [row(sel, gidx * EXPERTS_PER_GROUP + i) for i in range(EXPERTS_PER_GROUP)]
        gs = None
        for i in range(EXPERTS_PER_GROUP):
            for j in range(i + 1, EXPERTS_PER_GROUP):
                pair = v[i] + v[j]
                gs = pair if gs is None else jnp.maximum(gs, pair)
        if best_gs is None:
            best_gs, grp = gs, jnp.zeros(gs.shape, jnp.int32)
        else:
            better = gs > best_gs
            grp = jnp.where(better, gidx, grp)
            best_gs = jnp.where(better, gs, best_gs)

    def pick(a, i):
        out = row(a, i)
        for gidx in range(1, N_GROUPS):
            out = jnp.where(grp == gidx, row(a, gidx * EXPERTS_PER_GROUP + i), out)
        return out

    cand = [pick(sel, i) for i in range(EXPERTS_PER_GROUP)]
    cand_score = [pick(scores, i) for i in range(EXPERTS_PER_GROUP)]

    def argmax_first(vals, exclude=None):
        best, idx = None, None
        for i, v in enumerate(vals):
            if exclude is not None:
                v = jnp.where(exclude == i, NEG_INF, v)
            if best is None:
                best, idx = v, jnp.zeros(v.shape, jnp.int32)
            else:
                better = v > best
                idx = jnp.where(better, i, idx)
                best = jnp.where(better, v, best)
        return idx

    loc1 = argmax_first(cand)
    loc2 = argmax_first(cand, exclude=loc1)

    def take(vals, idx):
        out = vals[0]
        for i in range(1, len(vals)):
            out = jnp.where(idx == i, vals[i], out)
        return out

    g1 = take(cand_score, loc1)
    g2 = take(cand_score, loc2)
    gsum = g1 + g2
    first_is_a = loc1 < loc2
    a = jnp.minimum(loc1, loc2)
    b = jnp.maximum(loc1, loc2)
    pair = jnp.where(a == 0, b - 1, jnp.where(a == 1, b + 1, PAIRS_PER_GROUP - 1))
    bucket = grp * PAIRS_PER_GROUP + pair
    gate_a = jnp.where(first_is_a, g1, g2) / gsum
    gate_b = jnp.where(first_is_a, g2, g1) / gsum

    n_rows = base_sc.shape[0]
    hit = lax.broadcasted_iota(jnp.int32, (n_rows, tm), 0) == bucket
    ones = jnp.where(hit, 1.0, 0.0)
    before = _dot(ones.astype(BF16), tri_ref[...]) + base_sc[...]
    rank = jnp.sum(jnp.where(hit, before, 0.0), axis=0, keepdims=True)
    base_sc[...] = base_sc[...] + jnp.sum(ones, axis=1, keepdims=True)
    cnt_ref[...] = jnp.broadcast_to(base_sc[...], cnt_ref.shape)

    zero = jnp.zeros((1, tm), F32)
    o_ref[...] = jnp.concatenate([bucket.astype(F32), rank, gate_a, gate_b, zero, zero, zero, zero], axis=0)


def _router(x, whi, wlo, br):
    m_rows, d = x.shape
    tm = _token_tile(m_rows)
    tri = jnp.triu(jnp.ones((tm, tm), BF16), k=1)
    return pl.pallas_call(
        _router_kernel, grid=(m_rows // tm,),
        in_specs=[pl.BlockSpec((tm, d), lambda i: (i, 0)), _full(whi.shape), _full(wlo.shape), _full(br.shape),
                  _full(tri.shape)],
        out_specs=(pl.BlockSpec((SUBLANES, tm), lambda i: (0, i)), _full((N_BUCKETS, LANES))),
        out_shape=(jax.ShapeDtypeStruct((SUBLANES, m_rows), F32), jax.ShapeDtypeStruct((N_BUCKETS, LANES), F32)),
        scratch_shapes=[pltpu.VMEM((N_BUCKETS, 1), F32)],
        compiler_params=_params(("arbitrary",)), name="router",
    )(x, whi, wlo, br, tri)


def _row_copy(src_ref, src_row, dst_ref, dst_row, sem):
    return pltpu.make_async_copy(src_ref.at[pl.ds(src_row, 1), :], dst_ref.at[pl.ds(dst_row, 1), :], sem)


HI_HALF = 0xFFFF0000


def _pack_bf16_pairs(x):
    n = x.shape[1] // 2
    lo = pltpu.bitcast(x[:, :n].astype(BF16).astype(F32), jnp.uint32)
    hi = pltpu.bitcast(x[:, n:].astype(BF16).astype(F32), jnp.uint32)
    return (lo >> 16) | (hi & jnp.uint32(HI_HALF))


def _unpack_bf16_pairs(p):
    lo = pltpu.bitcast(p << 16, F32).astype(BF16)
    hi = pltpu.bitcast(p & jnp.uint32(HI_HALF), F32).astype(BF16)
    return jnp.concatenate([lo, hi], axis=1)


def _dispatch_kernel(dest_ref, x_ref, gate_ref, buf_in_ref, buf_ref, xp_sc, sem):
    del buf_in_ref
    tm, d = x_ref.shape
    step = pl.program_id(0)
    slot = step % 2

    def wait_slot(sl):
        pltpu.make_async_copy(xp_sc.at[sl], buf_ref.at[pl.ds(0, tm), :], sem.at[sl]).wait()

    @pl.when(step >= 2)
    def _():
        wait_slot(slot)

    stage = xp_sc.at[slot]
    stage[:, 0:d // 2] = _pack_bf16_pairs(x_ref[...])
    gate = gate_ref[...]
    lane = lax.broadcasted_iota(jnp.int32, (tm, LANES), 1)
    stage[:, d // 2:] = pltpu.bitcast(jnp.where(lane == 0, gate[:, 0:1], jnp.where(lane == 1, gate[:, 1:2], 0.0)),
                                      jnp.uint32)

    for r in range(tm):
        _row_copy(stage, r, buf_ref, dest_ref[0, 0, r], sem.at[slot]).start(priority=r % 2)

    @pl.when(step == pl.num_programs(0) - 1)
    def _():
        wait_slot(slot)

        @pl.when(step >= 1)
        def _():
            wait_slot(1 - slot)


def _dispatch(x, gates, dest_tiles, n_slots):
    m_rows, d = x.shape
    tm = dest_tiles.shape[2]
    width = d // 2 + LANES
    buf0 = jnp.zeros((n_slots, width), jnp.uint32)
    return pl.pallas_call(
        _dispatch_kernel, grid=(m_rows // tm,),
        in_specs=[pl.BlockSpec((1, 1, tm), lambda i: (i, 0, 0), memory_space=pltpu.SMEM),
                  pl.BlockSpec((tm, d), lambda i: (i, 0)),
                  pl.BlockSpec((tm, 2), lambda i: (i, 0)),
                  pl.BlockSpec(memory_space=pl.ANY)],
        out_specs=pl.BlockSpec(memory_space=pl.ANY),
        out_shape=jax.ShapeDtypeStruct((n_slots, width), jnp.uint32),
        scratch_shapes=[pltpu.VMEM((2, tm, width), jnp.uint32), pltpu.SemaphoreType.DMA((2,))],
        input_output_aliases={3: 0},
        compiler_params=_params(("arbitrary",)), name="moe_dispatch",
    )(dest_tiles, x, gates, buf0)


def _expert_kernel(tea_ref, teb_ref, tv_ref, x_ref, win_a_ref, wout_a_ref, win_b_ref, wout_b_ref, o_ref):
    del tea_ref, teb_ref
    i = pl.program_id(0)
    de = wout_a_ref.shape[0]
    half = win_a_ref.shape[0] // 2

    @pl.when(tv_ref[i] != 0)
    def _():
        xb = _unpack_bf16_pairs(x_ref[:, 0:half])
        gate = pltpu.bitcast(x_ref[:, half:], F32)

        def expert(win_ref, wout_ref):
            h = _dot(xb, win_ref[...])
            hg = h[:, 0:de]
            act = hg * _sigmoid(hg) * h[:, de:2 * de]
            return _dot(act.astype(BF16), wout_ref[...])

        o_ref[...] = gate[:, 0:1] * expert(win_a_ref, wout_a_ref) + gate[:, 1:2] * expert(win_b_ref, wout_b_ref)

    @pl.when(tv_ref[i] == 0)
    def _():
        o_ref[...] = jnp.zeros(o_ref.shape, F32)


def _experts(buf, tile_ea, tile_eb, tile_valid, w_in, w_out, layer, tile_rows):
    n_slots, width = buf.shape
    d, de2 = w_in.shape[2], w_in.shape[3]
    grid_spec = pltpu.PrefetchScalarGridSpec(
        num_scalar_prefetch=3, grid=(n_slots // tile_rows,),
        in_specs=[pl.BlockSpec((tile_rows, width), lambda i, ea, eb, tv: (i, 0)),
                  pl.BlockSpec((None, None, d, de2), lambda i, ea, eb, tv: (layer, ea[i], 0, 0)),
                  pl.BlockSpec((None, None, de2 // 2, d), lambda i, ea, eb, tv: (layer, ea[i], 0, 0)),
                  pl.BlockSpec((None, None, d, de2), lambda i, ea, eb, tv: (layer, eb[i], 0, 0)),
                  pl.BlockSpec((None, None, de2 // 2, d), lambda i, ea, eb, tv: (layer, eb[i], 0, 0))],
        out_specs=pl.BlockSpec((tile_rows, d), lambda i, ea, eb, tv: (i, 0)))
    return pl.pallas_call(
        _expert_kernel, grid_spec=grid_spec, out_shape=jax.ShapeDtypeStruct((n_slots, d), F32),
        compiler_params=_params(("arbitrary",)), name="moe_experts",
    )(tile_ea, tile_eb, tile_valid, buf, w_in, w_out, w_in, w_out)


def _combine_kernel(dest_ref, dest_next_ref, x_ref, g_ref, b_ref, y_ref, o_ref, y_sc, sem):
    tm = x_ref.shape[0]
    i = pl.program_id(0)
    slot = i % 2

    def gather(idx_ref, sl):
        for r in range(tm):
            _row_copy(y_ref, idx_ref[0, 0, r], y_sc.at[sl], r, sem.at[sl]).start(priority=r % 2)

    @pl.when(i == 0)
    def _():
        gather(dest_ref, 0)

    @pl.when(i + 1 < pl.num_programs(0))
    def _():
        gather(dest_next_ref, 1 - slot)

    pltpu.make_async_copy(y_ref.at[pl.ds(0, tm), :], y_sc.at[slot], sem.at[slot]).wait()
    o_ref[...] = _layer_norm(ALPHA * x_ref[...] + y_sc[slot], g_ref[...], b_ref[...])


def _combine(x, dest_tiles, y, g, b):
    m_rows, d = x.shape
    tm = dest_tiles.shape[2]
    last = m_rows // tm - 1
    return pl.pallas_call(
        _combine_kernel, grid=(m_rows // tm,),
        in_specs=[pl.BlockSpec((1, 1, tm), lambda i: (i, 0, 0), memory_space=pltpu.SMEM),
                  pl.BlockSpec((1, 1, tm), lambda i: (jnp.minimum(i + 1, last), 0, 0), memory_space=pltpu.SMEM),
                  pl.BlockSpec((tm, d), lambda i: (i, 0)),
                  _full(g.shape), _full(b.shape),
                  pl.BlockSpec(memory_space=pl.ANY)],
        out_specs=pl.BlockSpec((tm, d), lambda i: (i, 0)),
        out_shape=jax.ShapeDtypeStruct((m_rows, d), F32),
        scratch_shapes=[pltpu.VMEM((2, tm, d), F32), pltpu.SemaphoreType.DMA((2,))],
        compiler_params=_params(("arbitrary",)), name="moe_combine",
    )(dest_tiles, dest_tiles, x, g, b, y)


def _moe_layer(x, rw, w_in, w_out, layer, g, b, expert_tile):
    m_rows, _ = x.shape
    tm = min(m_rows, MOE_ROW_TILE)
    assert m_rows % tm == 0
    route, counts = _router(x, rw["whi"], rw["wlo"], rw["br"])
    bucket = route[0].astype(jnp.int32)
    rank = route[1].astype(jnp.int32)
    gates = route[2:4].T
    counts = counts[:, 0].astype(jnp.int32)
    padded = (counts + expert_tile - 1) // expert_tile * expert_tile
    pad_ends = jnp.cumsum(padded)
    pad_starts = pad_ends - padded
    b_range = jnp.arange(N_BUCKETS, dtype=jnp.int32)
    dest = jnp.sum(jnp.where(bucket[:, None] == b_range, pad_starts, 0), axis=-1) + rank
    n_tiles = -(-(m_rows + N_BUCKETS * (expert_tile - 1)) // expert_tile)
    tile_start = jnp.arange(n_tiles, dtype=jnp.int32) * expert_tile
    tile_bucket = jnp.minimum(jnp.sum((tile_start[:, None] >= pad_ends[None, :]).astype(jnp.int32), axis=1),
                              N_BUCKETS - 1)
    tile_valid = (tile_start < pad_ends[-1]).astype(jnp.int32)
    pair = tile_bucket % PAIRS_PER_GROUP
    group_base = tile_bucket // PAIRS_PER_GROUP * EXPERTS_PER_GROUP
    pair_range = jnp.arange(PAIRS_PER_GROUP, dtype=jnp.int32)
    tile_ea = group_base + jnp.sum(jnp.where(pair[:, None] == pair_range, jnp.asarray(PAIR_FIRST, jnp.int32), 0), axis=1)
    tile_eb = group_base + jnp.sum(jnp.where(pair[:, None] == pair_range, jnp.asarray(PAIR_SECOND, jnp.int32), 0), axis=1)
    dest_tiles = dest.reshape(m_rows // tm, 1, tm)
    buf = _dispatch(x, gates, dest_tiles, n_tiles * expert_tile)
    y = _experts(buf, tile_ea, tile_eb, tile_valid, w_in, w_out, layer, expert_tile)
    return _combine(x, dest_tiles, y, g, b)


def _block_diag(w):
    h, n, _ = w.shape
    eye = jnp.eye(h, dtype=w.dtype)
    return (eye[:, None, :, None] * w[:, :, None, :]).reshape(h * n, h * n)


def _pair_blocks(w):
    h, a, b = w.shape
    w = w.reshape(h // 2, 2, a, b)
    eye = jnp.eye(2, dtype=w.dtype)
    return (eye[None, :, None, :, None] * w[:, :, :, None, :]).reshape(h // 2, 2 * a, 2 * b)


def _time_major(a):
    b, t, c = a.shape
    return a.transpose(1, 0, 2).reshape(t * b, c)


def _batch_major(a, nb):
    return a.reshape(-1, nb, a.shape[-1]).transpose(1, 0, 2)


def _rope_tables(pos):
    half = QK_ROPE // 2
    inv = ROPE_THETA ** (-jnp.arange(half, dtype=F32) / half)
    ang = pos.astype(F32)[:, None] * inv
    cos, sin = jnp.cos(ang), jnp.sin(ang)
    cos_t = jnp.tile(jnp.concatenate([cos, cos], axis=1), (1, N_HEADS))
    sin_t = jnp.tile(jnp.concatenate([-sin, sin], axis=1), (1, N_HEADS))
    return cos_t, sin_t


def kernel(x_prompt, x_sample, state_conv_a, state_conv_b, state_rglru_h, state_conv_c, cache_ckv, cache_kpe, page_table, w_in_even, conv_a_w, conv_b_w, conv_b_b, lru_wa, lru_ba, lru_wx, lru_bx, lru_lambda, w_out_even, w_in_odd, conv_c_w, conv_c_b, ln_c_g, ln_c_b, q_norm_g, w_q_b, kv_norm_g, w_uk, w_uv, w_out_odd, ln_mix_g, ln_mix_b, ln_ffn_g, ln_ffn_b, w_router, b_router, w_exp_in, w_exp_out):
    bp, n_tp, d = x_prompt.shape
    bs, n_ts, _ = x_sample.shape
    past_len = page_table.shape[1] * PAGE_SIZE
    dc = conv_c_w.shape[2]
    q_rank = q_norm_g.shape[1]
    kv_rank = kv_norm_g.shape[1]

    xp = _time_major(x_prompt)
    xs = _time_major(x_sample)
    row2 = lambda v: v.reshape(1, -1)

    wr = jnp.pad(w_router, ((0, 0), (0, LANES - N_EXPERTS)))
    wr_hi = wr.astype(BF16)
    router_w = {"whi": wr_hi, "wlo": (wr - wr_hi.astype(F32)).astype(BF16), "br": b_router.reshape(N_EXPERTS, 1)}
    rope_p = _rope_tables(jnp.arange(n_tp))
    rope_s = _rope_tables(past_len + jnp.arange(n_ts))
    we_in, we_out = w_exp_in.astype(BF16), w_exp_out.astype(BF16)
    cache_kpe_t = jnp.swapaxes(cache_kpe, 2, 3)

    outs = {k: [] for k in ("ca_p", "ca_s", "cb_p", "cb_s", "h_p", "h_s", "cc_p", "cc_s", "ckv_p", "ckv_s", "kpe_p", "kpe_s")}
    for l in range(DEPTH):
        j = l // 2
        lg, lb = row2(ln_mix_g[l]), row2(ln_mix_b[l])
        if l % 2 == 0:
            wa_bd, wx_bd = _block_diag(lru_wa[j]), _block_diag(lru_wx[j])
            hc = wa_bd.shape[0] // 2
            wg = jnp.stack([jnp.concatenate([wa_bd[s * hc:(s + 1) * hc, s * hc:(s + 1) * hc],
                                             wx_bd[s * hc:(s + 1) * hc, s * hc:(s + 1) * hc]], axis=1)
                            for s in range(2)]).astype(BF16)
            w = {"win": w_in_even[j].astype(BF16), "caw": conv_a_w[j], "cbw": conv_b_w[j], "cbb": row2(conv_b_b[j]),
                 "wg": wg, "ba": row2(lru_ba[j]), "bx": row2(lru_bx[j]), "lam": row2(lru_lambda[j]),
                 "wout": w_out_even[j].astype(BF16), "g": lg, "b": lb}
            da = conv_a_w.shape[2]
            db = conv_b_w.shape[2]
            xp, a1, b1, h1 = _even_layer(xp, w, jnp.zeros((2 * bp, da), F32), jnp.zeros((3 * bp, db), F32),
                                         jnp.zeros((bp, db), F32), nb=bp, n_t=n_tp)
            xs, a2, b2, h2 = _even_layer(xs, w, _time_major(state_conv_a[j]), _time_major(state_conv_b[j]),
                                         state_rglru_h[j], nb=bs, n_t=n_ts)
            outs["ca_p"].append(_batch_major(a1, bp)); outs["ca_s"].append(_batch_major(a2, bs))
            outs["cb_p"].append(_batch_major(b1, bp)); outs["cb_s"].append(_batch_major(b2, bs))
            outs["h_p"].append(h1); outs["h_s"].append(h2)
        else:
            nope_all = N_HEADS * QK_NOPE
            wq = w_q_b[j].reshape(q_rank, N_HEADS, QK_NOPE + QK_ROPE)
            wqb = jnp.concatenate([wq[:, :, :QK_NOPE].reshape(q_rank, nope_all),
                                   wq[:, :, QK_NOPE:].reshape(q_rank, N_HEADS * QK_ROPE)], axis=1).astype(BF16)
            o_pe = 2 * dc + q_rank + kv_rank
            win = jnp.concatenate([w_in_odd[j][:, :o_pe], jnp.tile(w_in_odd[j][:, o_pe:], (1, N_HEADS))], axis=1).astype(BF16)
            wuk = _pair_blocks(w_uk[j].transpose(1, 2, 0)).astype(BF16)
            wuv = _pair_blocks(w_uv[j].transpose(1, 0, 2)).astype(BF16)
            w = {"win": win, "ccw": conv_c_w[j], "ccb": row2(conv_c_b[j]), "lcg": row2(ln_c_g[j]), "lcb": row2(ln_c_b[j]),
                 "qg": row2(q_norm_g[j]), "wqb": wqb, "kvg": row2(kv_norm_g[j]), "wuk": wuk}
            width = conv_c_w.shape[1]
            wout = w_out_odd[j].astype(BF16)
            ycp, qp, kp, ckv1, kpe1, c1 = _odd_pre(xp, w, jnp.zeros(((width - 1) * bp, dc), F32), *rope_p, nb=bp, n_t=n_tp)
            ycs, qs, ks, ckv2, kpe2, c2 = _odd_pre(xs, w, _time_major(state_conv_c[j]), *rope_s, nb=bs, n_t=n_ts)
            ydp = _prompt_attention(qp, kp, wuv)
            yds = _sample_attention(qs, ks, wuv, cache_ckv, cache_kpe_t, page_table, j)
            xp = _odd_post(xp, ycp, ydp, wout, lg, lb, nb=bp, n_t=n_tp)
            xs = _odd_post(xs, ycs, yds, wout, lg, lb, nb=bs, n_t=n_ts)
            outs["cc_p"].append(_batch_major(c1, bp)); outs["cc_s"].append(_batch_major(c2, bs))
            outs["ckv_p"].append(ckv1); outs["ckv_s"].append(ckv2)
            outs["kpe_p"].append(kpe1); outs["kpe_s"].append(kpe2)
        fg, fb = row2(ln_ffn_g[l]), row2(ln_ffn_b[l])
        xp = _moe_layer(xp, router_w, we_in, we_out, l, fg, fb, expert_tile=512)
        xs = _moe_layer(xs, router_w, we_in, we_out, l, fg, fb, expert_tile=64)

    y_prompt = _batch_major(xp, bp)
    y_sample = _batch_major(xs, bs)
    st = lambda k: jnp.stack(outs[k])
    st1 = lambda k: jnp.stack(outs[k], axis=1)
    return (y_prompt, y_sample, st("ca_p"), st("ca_s"), st("cb_p"), st("cb_s"), st("h_p"), st("h_s"),
            st("cc_p"), st("cc_s"), st1("ckv_p"), st1("ckv_s"), st1("kpe_p"), st1("kpe_s"))
```
